```python
import math, functools
import jax, jax.numpy as jnp
from jax import lax
import numpy as np

D_MODEL = 2048
BATCH = 2
SEQ = 16384
DEPTH = 2

GRID_W = 64
CTX_LEN = 256

HY_WIDTH = D_MODEL // 2
HY_ORDER = 2
HY_SHORT = 3
HY_BANDS = 16
HY_EMB = 1 + 2 * HY_BANDS
HY_FILTER_HIDDEN = 64
HY_FILTER_GAIN = 0.03
HY_MIN_DECAY = math.log(1e-2) / 1.5
HY_MAX_DECAY = math.log(1e-2) / 0.3
GLA_HEADS = 4
GLA_DK = D_MODEL // 4
GLA_DV = D_MODEL // 2
GLA_HEAD_K = GLA_DK // GLA_HEADS
GLA_HEAD_V = GLA_DV // GLA_HEADS
GLA_RANK = 16
GLA_TAU = 16.0
GLA_CHUNK = 64
AB_SIZES = ((HY_ORDER + 1) * HY_WIDTH, GLA_DK, GLA_DK, GLA_DV, GLA_DV, GLA_RANK, GLA_RANK)
AB_IN = sum(AB_SIZES)
AB_MIX = HY_WIDTH + GLA_DV
SSD_D_INNER = 2 * D_MODEL
SSD_HEADDIM = 64
SSD_HEADS = SSD_D_INNER // SSD_HEADDIM
SSD_GROUPS = 8
SSD_HPG = SSD_HEADS // SSD_GROUPS
SSD_STATE = 128
SSD_CONV = 3
SSD_CHUNK = 128
SSD_CONV_DIM = SSD_D_INNER + 2 * SSD_GROUPS * SSD_STATE
SSD_SIZES = (SSD_D_INNER, SSD_CONV_DIM, SSD_HEADS, SSD_HEADS)
SSD_IN = sum(SSD_SIZES)
N_EXPERTS = 16
N_EXPERT_GROUPS = 4
EXPERTS_PER_GROUP = N_EXPERTS // N_EXPERT_GROUPS
TOP_K = 2
D_EXPERT = D_MODEL // 2
MOE_BLOCK = 256
ALPHA = (2 * DEPTH) ** 0.25
BETA = (8 * DEPTH) ** -0.25
EPS = 1e-6

kernel_name = 'hybrid_diffusion_hyena_gla_ssd_moe'


def _split(t, sizes):
    return jnp.split(t, np.cumsum(sizes)[:-1].tolist(), axis=-1)


def _layer_norm(x, g, b):
    xf = x.astype(jnp.float32)
    xc = xf - jnp.mean(xf, -1, keepdims=True)
    var = jnp.mean(xc * xc, -1, keepdims=True)
    return (xc * lax.rsqrt(var + EPS) * g.astype(jnp.float32) + b.astype(jnp.float32)).astype(x.dtype)


def _rms_norm(x, g):
    xf = x.astype(jnp.float32)
    return xf * lax.rsqrt(jnp.mean(xf * xf, -1, keepdims=True) + EPS) * g.astype(jnp.float32)


def _dwconv1d(x, w, b):
    k, L = w.shape[0], x.shape[1]
    xp = jnp.pad(x, ((0, 0), (k // 2, k // 2), (0, 0)))
    return sum(xp[:, j:j + L] * w[j] for j in range(k)) + b


def _dwconv_grid(x, w, b):
    bsz, L, C = x.shape
    rows = L // GRID_W
    k = w.shape[0]
    p = k // 2
    xg = jnp.pad(x.reshape(bsz, rows, GRID_W, C), ((0, 0), (p, p), (p, p), (0, 0)))
    out = sum(xg[:, i:i + rows, j:j + GRID_W] * w[i, j] for i in range(k) for j in range(k))
    return (out + b).reshape(bsz, L, C)


def _hyena_spectra(L, w1, b1, fr1, w2, b2, fr2, w3):
    f32 = jnp.float32
    pos = jnp.arange(L, dtype=f32)[:, None]
    t = pos / max(L - 1, 1)
    bands = jnp.linspace(1e-4, HY_BANDS - 1, HY_BANDS, dtype=f32)
    ang = 2.0 * math.pi * bands * pos / L
    z = jnp.concatenate([t, jnp.cos(ang), -jnp.sin(ang)], axis=-1)
    hid = jnp.sin(fr1.astype(f32) * (z @ w1.astype(f32) + b1.astype(f32)))
    hid = jnp.sin(fr2.astype(f32) * (hid @ w2.astype(f32) + b2.astype(f32)))
    h = (hid @ w3.astype(f32)).reshape(L, HY_ORDER, 2, HY_WIDTH)
    deltas = jnp.abs(jnp.linspace(HY_MIN_DECAY, HY_MAX_DECAY, HY_WIDTH, dtype=f32))
    h = h * jnp.exp(-t * deltas)[:, None, None, :]
    causal, anti = h[:, :, 0], h[:, :, 1]
    full = jnp.concatenate([causal, jnp.zeros_like(causal[:1]), jnp.flip(anti[1:], axis=0)], axis=0)
    return jnp.fft.rfft(full, axis=0)


def _long_conv(u, spec, d):
    L = u.shape[1]
    uf = u.astype(jnp.float32)
    y = jnp.fft.irfft(jnp.fft.rfft(uf, n=2 * L, axis=1) * spec, n=2 * L, axis=1)[:, :L]
    return y + uf * d.astype(jnp.float32)


def _hyena_branch(u, conv_w, conv_b, f_w1, f_b1, f_fr1, f_w2, f_b2, f_fr2, f_w3, long_bias):
    L = u.shape[1]
    parts = jnp.split(_dwconv1d(u, conv_w, conv_b), HY_ORDER + 1, axis=-1)
    spec = _hyena_spectra(L, f_w1, f_b1, f_fr1, f_w2, f_b2, f_fr2, f_w3)
    y = parts[0].astype(jnp.float32)
    for o in range(HY_ORDER):
        y = parts[o + 1].astype(jnp.float32) * _long_conv(y, spec[:, o], long_bias[o])
    return y.astype(u.dtype)


def _gla_scan(q, k, v, g, s0):
    f32 = jnp.float32
    bsz, L = q.shape[:2]
    n = L // GLA_CHUNK

    def chunks(t):
        return jnp.moveaxis(t.astype(f32).reshape((bsz, n, GLA_CHUNK) + t.shape[2:]), 1, 0)

    causal = jnp.tril(jnp.ones((GLA_CHUNK, GLA_CHUNK), bool))[None, :, :, None, None]

    def step(state, inp):
        qc, kc, vc, gc = inp
        b = jnp.cumsum(gc, axis=1)
        decay = jnp.exp(jnp.where(causal, b[:, :, None] - b[:, None, :], -jnp.inf))
        att = jnp.einsum('bihk,bjhk,bijhk->bhij', qc, kc, decay)
        o = jnp.einsum('bhij,bjhv->bihv', att, vc) + jnp.einsum('bihk,bhkv->bihv', qc * jnp.exp(b), state)
        tail = jnp.exp(b[:, -1:] - b)
        state = jnp.exp(b[:, -1])[..., None] * state + jnp.einsum('bjhk,bjhv->bhkv', kc * tail, vc)
        return state, o

    state, o = lax.scan(step, s0, (chunks(q), chunks(k), chunks(v), chunks(g)))
    return jnp.moveaxis(o, 0, 1).reshape(bsz, L, GLA_HEADS, GLA_HEAD_V), state


def _ssd_scan(x, dt, bmat, cmat, s0, a_neg, d_skip):
    f32 = jnp.float32
    bsz, L = x.shape[:2]
    n = L // SSD_CHUNK
    x = x.astype(f32)
    a = dt * a_neg
    xdt = x * dt[..., None]

    def chunks(t):
        return jnp.moveaxis(t.astype(f32).reshape((bsz, n, SSD_CHUNK) + t.shape[2:]), 1, 0)

    causal = jnp.tril(jnp.ones((SSD_CHUNK, SSD_CHUNK), bool))[None, :, :, None, None]

    def step(state, inp):
        xc, ac, bc, cc = inp
        acs = jnp.cumsum(ac, axis=1)
        decay = jnp.exp(jnp.where(causal, acs[:, :, None] - acs[:, None, :], -jnp.inf))
        cb = jnp.einsum('bign,bjgn->bijg', cc, bc)
        y = jnp.einsum('bijg,bijgr,bjgrp->bigrp', cb, decay, xc)
        y = y + jnp.einsum('bign,bgrpn->bigrp', cc, state) * jnp.exp(acs)[..., None]
        tail = jnp.exp(acs[:, -1:] - acs)
        state = jnp.exp(acs[:, -1])[..., None, None] * state + jnp.einsum('bjgn,bjgr,bjgrp->bgrpn', bc, tail, xc)
        return state, y

    state, y = lax.scan(step, s0, (chunks(xdt), chunks(a), chunks(bmat), chunks(cmat)))
    y = jnp.moveaxis(y, 0, 1).reshape(x.shape) + d_skip[..., None] * x
    return y, state


def _two_way(scan_f, scan_b, ctx_f, ctx_b, lat_f, lat_b, s0):
    def flip(args):
        return tuple(jnp.flip(a, axis=1) for a in args)
    yc_f, st_f = scan_f(*ctx_f, s0)
    yc_b, st_b = scan_b(*flip(ctx_b), s0)
    yl_f, _ = scan_f(*lat_f, st_f)
    yl_b, _ = scan_b(*flip(lat_b), st_b)
    return yc_f + jnp.flip(yc_b, axis=1), yl_f + jnp.flip(yl_b, axis=1)


def _hyena_gla_mixer(h_ctx, h_lat, w_in, w_out, hy_p, gate_w2, gate_b, norm_g, need_ctx):
    f32 = jnp.float32

    def project(h):
        bsz, L = h.shape[:2]
        u, q, k, v, r, g1f, g1b = _split(h @ w_in, AB_SIZES)

        def log_gate(g1, d):
            z = (g1 @ gate_w2[d] + gate_b[d]).astype(f32)
            return (jax.nn.log_sigmoid(z) / GLA_TAU).reshape(bsz, L, GLA_HEADS, GLA_HEAD_K)

        q = q.reshape(bsz, L, GLA_HEADS, GLA_HEAD_K) * GLA_HEAD_K ** -0.5
        k = k.reshape(bsz, L, GLA_HEADS, GLA_HEAD_K)
        v = v.reshape(bsz, L, GLA_HEADS, GLA_HEAD_V)
        return u, (q, k, v), r, log_gate(g1f, 0), log_gate(g1b, 1)

    u_c, qkv_c, r_c, gf_c, gb_c = project(h_ctx)
    u_l, qkv_l, r_l, gf_l, gb_l = project(h_lat)
    s0 = jnp.zeros((h_lat.shape[0], GLA_HEADS, GLA_HEAD_K, GLA_HEAD_V), f32)
    o_c, o_l = _two_way(_gla_scan, _gla_scan, (*qkv_c, gf_c), (*qkv_c, gb_c), (*qkv_l, gf_l), (*qkv_l, gb_l), s0)

    def finish(u, o, r):
        bsz, L = u.shape[:2]
        o = _rms_norm(o, norm_g.reshape(GLA_HEADS, GLA_HEAD_V)).reshape(bsz, L, GLA_DV)
        o = (o * jax.nn.silu(r.astype(f32))).astype(u.dtype)
        return jnp.concatenate([_hyena_branch(u, *hy_p), o], axis=-1) @ w_out

    y_lat = finish(u_l, o_l, r_l)
    y_ctx = finish(u_c, o_c, r_c) if need_ctx else None
    return y_ctx, y_lat


def _mamba_mixer(h_ctx, h_lat, w_in, conv_w, conv_b, dt_bias, a_log, d_skip, norm_g, w_out, need_ctx):
    f32 = jnp.float32

    def project(h, conv):
        bsz, L = h.shape[:2]
        z, xbc, dt_f, dt_b = _split(h @ w_in, SSD_SIZES)
        xbc = jax.nn.silu(conv(xbc))
        xs, bm, cm = _split(xbc, (SSD_D_INNER, SSD_GROUPS * SSD_STATE, SSD_GROUPS * SSD_STATE))
        xs = xs.reshape(bsz, L, SSD_GROUPS, SSD_HPG, SSD_HEADDIM)
        bm = bm.reshape(bsz, L, SSD_GROUPS, SSD_STATE)
        cm = cm.reshape(bsz, L, SSD_GROUPS, SSD_STATE)
        dts = [jax.nn.softplus(t.astype(f32) + dt_bias[d].astype(f32)).reshape(bsz, L, SSD_GROUPS, SSD_HPG)
               for d, t in enumerate((dt_f, dt_b))]
        return z, (xs, dts[0], bm, cm), (xs, dts[1], bm, cm)

    z_c, ctx_f, ctx_b = project(h_ctx, lambda t: _dwconv1d(t, conv_w[SSD_CONV // 2], conv_b))
    z_l, lat_f, lat_b = project(h_lat, lambda t: _dwconv_grid(t, conv_w, conv_b))
    a_neg = -jnp.exp(a_log.astype(f32)).reshape(2, SSD_GROUPS, SSD_HPG)
    dsk = d_skip.astype(f32).reshape(2, SSD_GROUPS, SSD_HPG)
    scan_f = functools.partial(_ssd_scan, a_neg=a_neg[0], d_skip=dsk[0])
    scan_b = functools.partial(_ssd_scan, a_neg=a_neg[1], d_skip=dsk[1])
    s0 = jnp.zeros((h_lat.shape[0], SSD_GROUPS, SSD_HPG, SSD_HEADDIM, SSD_STATE), f32)
    y_c, y_l = _two_way(scan_f, scan_b, ctx_f, ctx_b, lat_f, lat_b, s0)

    def finish(y, z):
        bsz, L = z.shape[:2]
        y = y.reshape(bsz, L, SSD_D_INNER) * jax.nn.silu(z.astype(f32))
        y = _rms_norm(y.reshape(bsz, L, SSD_GROUPS, SSD_D_INNER // SSD_GROUPS), norm_g.reshape(SSD_GROUPS, -1))
        return y.reshape(bsz, L, SSD_D_INNER).astype(z.dtype) @ w_out

    y_lat = finish(y_l, z_l)
    y_ctx = finish(y_c, z_c) if need_ctx else None
    return y_ctx, y_lat


def _moe(h, router_w, router_b, w_gate, w_up, w_down):
    f32 = jnp.float32
    t = h.shape[0]
    scores = jax.nn.sigmoid(h.astype(f32) @ router_w.astype(f32))
    sel = (scores + router_b.astype(f32)).reshape(t, N_EXPERT_GROUPS, EXPERTS_PER_GROUP)
    group = jnp.argmax(lax.top_k(sel, 2)[0].sum(-1), axis=-1)
    in_group = jnp.take_along_axis(sel, group[:, None, None], axis=1)[:, 0]
    experts = group[:, None] * EXPERTS_PER_GROUP + lax.top_k(in_group, TOP_K)[1]
    gates = jnp.take_along_axis(scores, experts, axis=1)
    gates = gates / jnp.sum(gates, -1, keepdims=True)
    e_flat = experts.reshape(-1)
    tk = e_flat.shape[0]
    order = jnp.argsort(e_flat)
    e_s = e_flat[order]
    tok_s = order // TOP_K
    g_s = gates.reshape(-1)[order]
    counts = jnp.bincount(e_flat, length=N_EXPERTS)
    padded = (counts + MOE_BLOCK - 1) // MOE_BLOCK * MOE_BLOCK
    starts = jnp.cumsum(counts) - counts
    pstarts = jnp.cumsum(padded) - padded
    dest = pstarts[e_s] + jnp.arange(tk) - starts[e_s]
    n_blocks = -(-tk // MOE_BLOCK) + N_EXPERTS
    buf = jnp.zeros((n_blocks * MOE_BLOCK, h.shape[1]), h.dtype).at[dest].set(h[tok_s])
    block_expert = jnp.minimum(
        jnp.searchsorted(pstarts + padded, jnp.arange(n_blocks) * MOE_BLOCK, side='right'), N_EXPERTS - 1)

    def expert_block(args):
        xb, e = args
        return (jax.nn.silu(xb @ w_gate[e]) * (xb @ w_up[e])) @ w_down[e]

    out_blocks = lax.map(expert_block, (buf.reshape(n_blocks, MOE_BLOCK, -1), block_expert))
    rows = (out_blocks.reshape(n_blocks * MOE_BLOCK, -1)[dest] * g_s[:, None]).astype(h.dtype)
    return jnp.zeros_like(h).at[tok_s].add(rows)


def setup_inputs(seed: int = 0) -> dict:
    key = jax.random.key(seed)
    keys = iter(jax.random.split(key, 48))

    def nrm(shape, scale):
        return scale * jax.random.normal(next(keys), shape, jnp.float32)

    ne, no = (DEPTH + 1) // 2, DEPTH // 2
    fh = HY_FILTER_HIDDEN
    dt0 = jnp.exp(jax.random.uniform(next(keys), (no, 2, SSD_HEADS), jnp.float32, math.log(1e-3), math.log(1e-1)))
    a0 = jax.random.uniform(next(keys), (no, 2, SSD_HEADS), jnp.float32, 1.0, 16.0)
    return {
        'x': nrm((BATCH, SEQ, D_MODEL), 1.0),
        'c': nrm((BATCH, D_MODEL), 1.0),
        'ctx': nrm((BATCH, CTX_LEN, D_MODEL), 1.0),
        'c_ctx': nrm((D_MODEL,), 1.0),
        'router_w': nrm((D_MODEL, N_EXPERTS), D_MODEL ** -0.5),
        'router_b': nrm((N_EXPERTS,), 0.01),
        'mod_w': nrm((DEPTH, D_MODEL, 6 * D_MODEL), 0.5 * D_MODEL ** -0.5),
        'mod_b': nrm((DEPTH, 6 * D_MODEL), 0.02),
        'ln_g': 1.0 + nrm((DEPTH, 2, D_MODEL), 0.02),
        'ln_b': nrm((DEPTH, 2, D_MODEL), 0.02),
        'exp_w_gate': nrm((DEPTH, N_EXPERTS, D_MODEL, D_EXPERT), D_MODEL ** -0.5),
        'exp_w_up': nrm((DEPTH, N_EXPERTS, D_MODEL, D_EXPERT), D_MODEL ** -0.5),
        'exp_w_down': nrm((DEPTH, N_EXPERTS, D_EXPERT, D_MODEL), BETA * D_EXPERT ** -0.5),
        'ab_w_in': nrm((ne, D_MODEL, AB_IN), D_MODEL ** -0.5),
        'ab_w_out': nrm((ne, AB_MIX, D_MODEL), BETA * AB_MIX ** -0.5),
        'hy_conv_w': nrm((ne, HY_SHORT, (HY_ORDER + 1) * HY_WIDTH), HY_SHORT ** -0.5),
        'hy_conv_b': nrm((ne, (HY_ORDER + 1) * HY_WIDTH), 0.02),
        'hy_f_w1': nrm((ne, HY_EMB, fh), HY_EMB ** -0.5),
        'hy_f_b1': nrm((ne, fh), 0.02),
        'hy_f_fr1': 1.0 + nrm((ne, fh), 0.1),
        'hy_f_w2': nrm((ne, fh, fh), fh ** -0.5),
        'hy_f_b2': nrm((ne, fh), 0.02),
        'hy_f_fr2': 1.0 + nrm((ne, fh), 0.1),
        'hy_f_w3': nrm((ne, fh, HY_ORDER * 2 * HY_WIDTH), HY_FILTER_GAIN * fh ** -0.5),
        'hy_long_bias': nrm((ne, HY_ORDER, HY_WIDTH), 1.0),
        'gla_gate_w2': nrm((ne, 2, GLA_RANK, GLA_DK), GLA_RANK ** -0.5),
        'gla_gate_b': nrm((ne, 2, GLA_DK), 0.5),
        'gla_norm_g': 1.0 + nrm((ne, GLA_DV), 0.02),
        'ssd_w_in': nrm((no, D_MODEL, SSD_IN), D_MODEL ** -0.5),
        'ssd_conv_w': nrm((no, SSD_CONV, SSD_CONV, SSD_CONV_DIM), 1.0 / SSD_CONV),
        'ssd_conv_b': nrm((no, SSD_CONV_DIM), 0.02),
        'ssd_dt_bias': dt0 + jnp.log(-jnp.expm1(-dt0)),
        'ssd_a_log': jnp.log(a0),
        'ssd_d': 1.0 + nrm((no, 2, SSD_HEADS), 0.1),
        'ssd_norm_g': 1.0 + nrm((no, SSD_D_INNER), 0.02),
        'ssd_w_out': nrm((no, SSD_D_INNER, D_MODEL), BETA * SSD_D_INNER ** -0.5),
    }


def reference(x, c, ctx, c_ctx, router_w, router_b, mod_w, mod_b, ln_g, ln_b,
              exp_w_gate, exp_w_up, exp_w_down, ab_w_in, ab_w_out,
              hy_conv_w, hy_conv_b, hy_f_w1, hy_f_b1, hy_f_fr1, hy_f_w2, hy_f_b2, hy_f_fr2, hy_f_w3, hy_long_bias,
              gla_gate_w2, gla_gate_b, gla_norm_g,
              ssd_w_in, ssd_conv_w, ssd_conv_b, ssd_dt_bias, ssd_a_log, ssd_d, ssd_norm_g, ssd_w_out):
    n_lat = x.shape[0] * x.shape[1]
    for i in range(DEPTH):
        last = i == DEPTH - 1
        j = i // 2
        mod_l = jnp.split((jax.nn.silu(c) @ mod_w[i] + mod_b[i])[:, None, :], 6, axis=-1)
        mod_c = jnp.split(jax.nn.silu(c_ctx) @ mod_w[i] + mod_b[i], 6, axis=-1)
        h_l = x * (1 + mod_l[1]) + mod_l[0]
        h_c = ctx * (1 + mod_c[1]) + mod_c[0]
        if i % 2 == 0:
            hy_p = (hy_conv_w[j], hy_conv_b[j], hy_f_w1[j], hy_f_b1[j], hy_f_fr1[j], hy_f_w2[j], hy_f_b2[j],
                    hy_f_fr2[j], hy_f_w3[j], hy_long_bias[j])
            y_c, y_l = _hyena_gla_mixer(h_c, h_l, ab_w_in[j], ab_w_out[j], hy_p, gla_gate_w2[j], gla_gate_b[j],
                                        gla_norm_g[j], not last)
        else:
            y_c, y_l = _mamba_mixer(h_c, h_l, ssd_w_in[j], ssd_conv_w[j], ssd_conv_b[j], ssd_dt_bias[j],
                                    ssd_a_log[j], ssd_d[j], ssd_norm_g[j], ssd_w_out[j], not last)
        x = _layer_norm(ALPHA * x + mod_l[2] * y_l, ln_g[i, 0], ln_b[i, 0])
        tokens = (x * (1 + mod_l[4]) + mod_l[3]).reshape(n_lat, D_MODEL)
        if not last:
            ctx = _layer_norm(ALPHA * ctx + mod_c[2] * y_c, ln_g[i, 0], ln_b[i, 0])
            h_c = ctx * (1 + mod_c[4]) + mod_c[3]
            tokens = jnp.concatenate([tokens, h_c.reshape(-1, D_MODEL)], axis=0)
        y = _moe(tokens, router_w, router_b, exp_w_gate[i], exp_w_up[i], exp_w_down[i])
        x = _layer_norm(ALPHA * x + mod_l[5] * y[:n_lat].reshape(x.shape), ln_g[i, 1], ln_b[i, 1])
        if not last:
            ctx = _layer_norm(ALPHA * ctx + mod_c[5] * y[n_lat:].reshape(ctx.shape), ln_g[i, 1], ln_b[i, 1])
    return x
```

```python
import functools
import math

import numpy as np
import jax
import jax.numpy as jnp
from jax import lax
from jax.experimental import pallas as pl
from jax.experimental.pallas import tpu as pltpu

F32 = jnp.float32
BF16 = jnp.bfloat16
HIGHEST = lax.Precision.HIGHEST

D_MODEL = 2048
DEPTH = 2
GRID_W = 64
HY_WIDTH = D_MODEL // 2
HY_ORDER = 2
HY_BANDS = 16
HY_MIN_DECAY = math.log(1e-2) / 1.5
HY_MAX_DECAY = math.log(1e-2) / 0.3
GLA_HEADS = 4
GLA_DK = D_MODEL // 4
GLA_DV = D_MODEL // 2
GLA_HEAD_K = GLA_DK // GLA_HEADS
GLA_HEAD_V = GLA_DV // GLA_HEADS
GLA_RANK = 16
GLA_TAU = 16.0
AB_SIZES = ((HY_ORDER + 1) * HY_WIDTH, GLA_DK, GLA_DK, GLA_DV, GLA_DV, GLA_RANK, GLA_RANK)
AB_IN = sum(AB_SIZES)
SSD_D_INNER = 2 * D_MODEL
SSD_HEADDIM = 64
SSD_HEADS = SSD_D_INNER // SSD_HEADDIM
SSD_GROUPS = 8
SSD_HPG = SSD_HEADS // SSD_GROUPS
SSD_STATE = 128
SSD_CONV_DIM = SSD_D_INNER + 2 * SSD_GROUPS * SSD_STATE
SSD_IN = SSD_D_INNER + SSD_CONV_DIM + 2 * SSD_HEADS
N_EXPERTS = 16
N_EXPERT_GROUPS = 4
EXPERTS_PER_GROUP = N_EXPERTS // N_EXPERT_GROUPS
TOP_K = 2
D_EXPERT = D_MODEL // 2
ALPHA = (2 * DEPTH) ** 0.25
EPS = 1e-6

LANE = 128
ROW_TILE = 256
MOE_BLOCK = 256
VMEM_LIMIT = 56 * 1024 * 1024


def _cparams(*sem):
    return pltpu.CompilerParams(dimension_semantics=sem, vmem_limit_bytes=VMEM_LIMIT)


def _silu(x):
    return x * (1.0 / (1.0 + jnp.exp(-x)))


def _mod_kernel(c_ref, w_ref, b_ref, o_ref):
    o_ref[...] = jnp.dot(_silu(c_ref[...]), w_ref[...], precision=HIGHEST, preferred_element_type=F32) + b_ref[...]


def _mod_vectors(c8, w, b):
    d, n = w.shape
    tn = 1536
    return pl.pallas_call(
        _mod_kernel,
        grid=(n // tn,),
        in_specs=[pl.BlockSpec((8, d), lambda j: (0, 0)),
                  pl.BlockSpec((d, tn), lambda j: (0, j)),
                  pl.BlockSpec((1, tn), lambda j: (0, j))],
        out_specs=pl.BlockSpec((8, tn), lambda j: (0, j)),
        out_shape=jax.ShapeDtypeStruct((8, n), F32),
        compiler_params=_cparams("arbitrary"),
    )(c8, w, b.reshape(1, n))


def _modulate_kernel(n_lat_tiles, x_ref, p_ref, o_ref):
    is_ctx = pl.program_id(1) >= n_lat_tiles
    p = p_ref[0]
    shift = jnp.where(is_ctx, p[2:3], p[0:1])
    scale = jnp.where(is_ctx, p[3:4], p[1:2])
    o_ref[0] = (x_ref[0] * (1.0 + scale) + shift).astype(o_ref.dtype)


def _modulate(x, params, n_lat):
    bsz, t, d = x.shape
    return pl.pallas_call(
        functools.partial(_modulate_kernel, n_lat // ROW_TILE),
        grid=(bsz, t // ROW_TILE),
        in_specs=[pl.BlockSpec((1, ROW_TILE, d), lambda b, i: (b, i, 0)),
                  pl.BlockSpec((1, 4, d), lambda b, i: (b, 0, 0))],
        out_specs=pl.BlockSpec((1, ROW_TILE, d), lambda b, i: (b, i, 0)),
        out_shape=jax.ShapeDtypeStruct((bsz, t, d), BF16),
        compiler_params=_cparams("parallel", "parallel"),
    )(x, params)


def _matmul_kernel(a_ref, w_ref, o_ref):
    o_ref[...] = jnp.dot(a_ref[...], w_ref[...], preferred_element_type=F32).astype(o_ref.dtype)


def _matmul(a, w, tm, tn, out_dtype=F32):
    m, k = a.shape
    n = w.shape[1]
    return pl.pallas_call(
        _matmul_kernel,
        grid=(m // tm, n // tn),
        in_specs=[pl.BlockSpec((tm, k), lambda i, j: (i, 0)),
                  pl.BlockSpec((k, tn), lambda i, j: (0, j))],
        out_specs=pl.BlockSpec((tm, tn), lambda i, j: (i, j)),
        out_shape=jax.ShapeDtypeStruct((m, n), out_dtype),
        compiler_params=_cparams("parallel", "arbitrary"),
    )(a, w)


def _layer_norm_rows(v, g, b):
    mu = jnp.mean(v, axis=-1, keepdims=True)
    vc = v - mu
    var = jnp.mean(vc * vc, axis=-1, keepdims=True)
    return vc * lax.rsqrt(var + EPS) * g + b


def _post_update(n_lat_tiles, y, x, p):
    is_ctx = pl.program_id(1) >= n_lat_tiles
    gate = jnp.where(is_ctx, p[3:4], p[0:1])
    scale = jnp.where(is_ctx, p[4:5], p[1:2])
    shift = jnp.where(is_ctx, p[5:6], p[2:3])
    xn = _layer_norm_rows(ALPHA * x + gate * y, p[6:7], p[7:8])
    return xn, xn * (1.0 + scale) + shift


def _proj_update_kernel(n_lat_tiles, n_a, *refs):
    a_refs = refs[:n_a]
    w_refs = refs[n_a:2 * n_a]
    x_ref, p_ref, rw_ref, xo_ref, tok_ref, lg_ref = refs[2 * n_a:]
    y = jnp.dot(a_refs[0][0], w_refs[0][...], preferred_element_type=F32)
    for a_ref, w_ref in zip(a_refs[1:], w_refs[1:]):
        y = y + jnp.dot(a_ref[0], w_ref[...], preferred_element_type=F32)
    xn, tok = _post_update(n_lat_tiles, y, x_ref[0], p_ref[0])
    xo_ref[0] = xn
    tok_ref[0] = tok.astype(tok_ref.dtype)
    lg_ref[0] = jnp.dot(tok, rw_ref[...], precision=HIGHEST, preferred_element_type=F32)


def _proj_update(a_list, w_list, x, params, router_w_pad, n_lat):
    bsz, t, d = x.shape
    n_a = len(a_list)
    in_specs = [pl.BlockSpec((1, ROW_TILE, a.shape[-1]), lambda b, i: (b, i, 0)) for a in a_list]
    in_specs += [pl.BlockSpec(w.shape, lambda b, i: (0, 0)) for w in w_list]
    in_specs += [pl.BlockSpec((1, ROW_TILE, d), lambda b, i: (b, i, 0)),
                 pl.BlockSpec((1, 8, d), lambda b, i: (b, 0, 0)),
                 pl.BlockSpec((d, LANE), lambda b, i: (0, 0))]
    return pl.pallas_call(
        functools.partial(_proj_update_kernel, n_lat // ROW_TILE, n_a),
        grid=(bsz, t // ROW_TILE),
        in_specs=in_specs,
        out_specs=[pl.BlockSpec((1, ROW_TILE, d), lambda b, i: (b, i, 0)),
                   pl.BlockSpec((1, ROW_TILE, d), lambda b, i: (b, i, 0)),
                   pl.BlockSpec((1, ROW_TILE, LANE), lambda b, i: (b, i, 0))],
        out_shape=[jax.ShapeDtypeStruct((bsz, t, d), F32),
                   jax.ShapeDtypeStruct((bsz, t, d), BF16),
                   jax.ShapeDtypeStruct((bsz, t, LANE), F32)],
        compiler_params=_cparams("parallel", "parallel"),
    )(*a_list, *w_list, x, params, router_w_pad)


def _expert_kernel(be_ref, nb_ref, x_ref, wg_ref, wu_ref, wd_ref, o_ref):
    @pl.when(pl.program_id(0) < nb_ref[0])
    def _():
        x = x_ref[...]
        hg = jnp.dot(x, wg_ref[0], preferred_element_type=F32)
        hu = jnp.dot(x, wu_ref[0], preferred_element_type=F32)
        h = (_silu(hg) * hu).astype(BF16)
        o_ref[...] = jnp.dot(h, wd_ref[0], preferred_element_type=F32)

    @pl.when(pl.program_id(0) >= nb_ref[0])
    def _():
        o_ref[...] = jnp.zeros_like(o_ref)


def _expert_blocks(buf, block_expert, n_used, wg, wu, wd):
    rows, d = buf.shape
    n_blocks = rows // MOE_BLOCK
    de = wg.shape[-1]
    grid_spec = pltpu.PrefetchScalarGridSpec(
        num_scalar_prefetch=2,
        grid=(n_blocks,),
        in_specs=[pl.BlockSpec((MOE_BLOCK, d), lambda i, be, nb: (i, 0)),
                  pl.BlockSpec((1, d, de), lambda i, be, nb: (be[i], 0, 0)),
                  pl.BlockSpec((1, d, de), lambda i, be, nb: (be[i], 0, 0)),
                  pl.BlockSpec((1, de, d), lambda i, be, nb: (be[i], 0, 0))],
        out_specs=pl.BlockSpec((MOE_BLOCK, d), lambda i, be, nb: (i, 0)),
    )
    return pl.pallas_call(
        _expert_kernel,
        grid_spec=grid_spec,
        out_shape=jax.ShapeDtypeStruct((rows, d), F32),
        compiler_params=_cparams("arbitrary"),
    )(block_expert, n_used, buf, wg, wu, wd)


def _combine_update_kernel(n_lat_tiles, y0_ref, y1_ref, g_ref, x_ref, p_ref, xo_ref, h_ref):
    g = g_ref[0]
    y = y0_ref[0] * g[:, 0:1] + y1_ref[0] * g[:, 1:2]
    xn, h = _post_update(n_lat_tiles, y, x_ref[0], p_ref[0])
    xo_ref[0] = xn
    h_ref[0] = h.astype(h_ref.dtype)


def _combine_update(y0, y1, gates, x, params, n_lat):
    bsz, t, d = x.shape
    row = pl.BlockSpec((1, ROW_TILE, d), lambda b, i: (b, i, 0))
    return pl.pallas_call(
        functools.partial(_combine_update_kernel, n_lat // ROW_TILE),
        grid=(bsz, t // ROW_TILE),
        in_specs=[row, row, pl.BlockSpec((1, ROW_TILE, LANE), lambda b, i: (b, i, 0)), row,
                  pl.BlockSpec((1, 8, d), lambda b, i: (b, 0, 0))],
        out_specs=[row, row],
        out_shape=[jax.ShapeDtypeStruct((bsz, t, d), F32), jax.ShapeDtypeStruct((bsz, t, d), BF16)],
        compiler_params=_cparams("parallel", "parallel"),
    )(y0, y1, gates, x, params)


def _route(logits, router_b):
    n = logits.shape[0]
    scores = jax.nn.sigmoid(logits)
    sel = (scores + router_b.astype(F32)).reshape(n, N_EXPERT_GROUPS, EXPERTS_PER_GROUP)
    group = jnp.argmax(lax.top_k(sel, 2)[0].sum(-1), axis=-1)
    in_group = jnp.take_along_axis(sel, group[:, None, None], axis=1)[:, 0]
    experts = group[:, None] * EXPERTS_PER_GROUP + lax.top_k(in_group, TOP_K)[1]
    gates = jnp.take_along_axis(scores, experts, axis=1)
    gates = gates / jnp.sum(gates, -1, keepdims=True)
    return experts.astype(jnp.int32), gates


def _moe_dispatch(experts):
    e_flat = experts.reshape(-1)
    tk = e_flat.shape[0]
    order = jnp.argsort(e_flat)
    e_s = e_flat[order]
    counts = jnp.bincount(e_flat, length=N_EXPERTS)
    padded = (counts + MOE_BLOCK - 1) // MOE_BLOCK * MOE_BLOCK
    starts = jnp.cumsum(counts) - counts
    pstarts = jnp.cumsum(padded) - padded
    dest_sorted = pstarts[e_s] + jnp.arange(tk) - starts[e_s]
    n_blocks = -(-tk // MOE_BLOCK) + N_EXPERTS
    block_expert = jnp.minimum(
        jnp.searchsorted(pstarts + padded, jnp.arange(n_blocks) * MOE_BLOCK, side='right'), N_EXPERTS - 1)
    n_used = (jnp.sum(padded) // MOE_BLOCK).astype(jnp.int32).reshape(1)
    dest = jnp.zeros((tk,), jnp.int32).at[order].set(dest_sorted.astype(jnp.int32))
    return order, dest_sorted, dest.reshape(-1, TOP_K), block_expert.astype(jnp.int32), n_used, n_blocks


def _moe(tok, logits, router_b, wg, wu, wd):
    bsz, t, d = tok.shape
    n = bsz * t
    experts, gates = _route(logits.reshape(n, LANE)[:, :N_EXPERTS], router_b)
    order, dest_sorted, dest, block_expert, n_used, n_blocks = _moe_dispatch(experts)
    tok_flat = tok.reshape(n, d)
    buf = jnp.zeros((n_blocks * MOE_BLOCK, d), tok.dtype).at[dest_sorted].set(tok_flat[order // TOP_K])
    out = _expert_blocks(buf, block_expert, n_used, wg, wu, wd)
    y0 = out[dest[:, 0]].reshape(bsz, t, d)
    y1 = out[dest[:, 1]].reshape(bsz, t, d)
    gpad = jnp.zeros((n, LANE), F32).at[:, :TOP_K].set(gates).reshape(bsz, t, LANE)
    return y0, y1, gpad


GLA_CS = 64
EXP_CLAMP = 80.0


def _gla_kernel(reverse, has_prev, nsub, *refs):
    if has_prev:
        (q_ref, k_ref, v_ref, g1_ref, w2_ref, gb_ref, prev_ref, r_ref, ng_ref, o_ref, s_ref) = refs
    else:
        (q_ref, k_ref, v_ref, g1_ref, w2_ref, gb_ref, o_ref, s_ref) = refs

    @pl.when(pl.program_id(1) == 0)
    def _():
        s_ref[...] = jnp.zeros_like(s_ref)

    cs = GLA_CS
    row = lax.broadcasted_iota(jnp.int32, (cs, cs), 0)
    col = lax.broadcasted_iota(jnp.int32, (cs, cs), 1)
    keep = (col >= row) if reverse else (col <= row)
    cum_m = keep.astype(F32)
    z = jnp.dot(g1_ref[0], w2_ref[0], precision=HIGHEST, preferred_element_type=F32) + gb_ref[0]
    log_gate = (jnp.minimum(z, 0.0) - jnp.log(1.0 + jnp.exp(-jnp.abs(z)))) * (1.0 / GLA_TAU)
    q_all = q_ref[0] * (GLA_HEAD_K ** -0.5)
    k_all = k_ref[0]
    v_all = v_ref[0]
    last = 0 if reverse else cs - 1
    for h in range(GLA_HEADS):
        ks = slice(h * GLA_HEAD_K, (h + 1) * GLA_HEAD_K)
        vs = slice(h * GLA_HEAD_V, (h + 1) * GLA_HEAD_V)
        state = s_ref[h]
        outs = [None] * nsub
        for c in (range(nsub - 1, -1, -1) if reverse else range(nsub)):
            rs = slice(c * cs, (c + 1) * cs)
            b = jnp.dot(cum_m, log_gate[rs, ks], precision=HIGHEST, preferred_element_type=F32)
            b_last = b[last:last + 1]
            ref = 0.5 * b_last
            qc, kc, vc = q_all[rs, ks], k_all[rs, ks], v_all[rs, vs].astype(BF16)
            q_in = (qc * jnp.exp(jnp.minimum(b - ref, EXP_CLAMP))).astype(BF16)
            k_in = (kc * jnp.exp(jnp.minimum(ref - b, EXP_CLAMP))).astype(BF16)
            att = lax.dot_general(q_in, k_in, (((1,), (1,)), ((), ())), preferred_element_type=F32)
            att = jnp.where(keep, att, 0.0).astype(BF16)
            o = jnp.dot(att, vc, preferred_element_type=F32)
            o = o + jnp.dot((qc * jnp.exp(b)).astype(BF16), state.astype(BF16), preferred_element_type=F32)
            k_tail = (kc * jnp.exp(b_last - b)).astype(BF16)
            upd = lax.dot_general(k_tail, vc, (((0,), (0,)), ((), ())), preferred_element_type=F32)
            state = jnp.exp(b_last).reshape(GLA_HEAD_K, 1) * state + upd
            outs[c] = o
        s_ref[h] = state
        o_h = jnp.concatenate(outs, axis=0)
        if has_prev:
            o_h = o_h + prev_ref[0][:, vs]
            o_h = o_h * lax.rsqrt(jnp.mean(o_h * o_h, axis=-1, keepdims=True) + EPS) * ng_ref[:, vs]
            o_h = o_h * _silu(r_ref[0][:, vs])
        o_ref[0, :, vs] = o_h.astype(o_ref.dtype)


def _gla_direction(u_all, w2pad, gbias, n_lat, reverse, prev=None, norm_g=None):
    bsz, t, _ = u_all.shape
    n = t // ROW_TILE
    nl = n_lat // ROW_TILE
    nsub = ROW_TILE // GLA_CS
    if reverse:
        blk = lambda s: n - 1 - s
    else:
        blk = lambda s: (s + nl) % n
    q0, k0, v0, r0, g0 = (AB_SIZES[0] // GLA_DK, (AB_SIZES[0] + GLA_DK) // GLA_DK, (AB_SIZES[0] + 2 * GLA_DK) // GLA_DV,
                          (AB_SIZES[0] + 2 * GLA_DK + GLA_DV) // GLA_DV, (AB_IN - 2 * GLA_RANK) // LANE)
    d = 1 if reverse else 0
    in_specs = [pl.BlockSpec((1, ROW_TILE, GLA_DK), lambda b, s: (b, blk(s), q0)),
                pl.BlockSpec((1, ROW_TILE, GLA_DK), lambda b, s: (b, blk(s), k0)),
                pl.BlockSpec((1, ROW_TILE, GLA_DV), lambda b, s: (b, blk(s), v0)),
                pl.BlockSpec((1, ROW_TILE, LANE), lambda b, s: (b, blk(s), g0)),
                pl.BlockSpec((1, LANE, GLA_DK), lambda b, s: (d, 0, 0)),
                pl.BlockSpec((1, 1, GLA_DK), lambda b, s: (d, 0, 0))]
    args = [u_all, u_all, u_all, u_all, w2pad, gbias]
    has_prev = prev is not None
    if has_prev:
        in_specs += [pl.BlockSpec((1, ROW_TILE, GLA_DV), lambda b, s: (b, blk(s), 0)),
                     pl.BlockSpec((1, ROW_TILE, GLA_DV), lambda b, s: (b, blk(s), r0)),
                     pl.BlockSpec((1, GLA_DV), lambda b, s: (0, 0))]
        args += [prev, u_all, norm_g.reshape(1, GLA_DV)]
    return pl.pallas_call(
        functools.partial(_gla_kernel, reverse, has_prev, nsub),
        grid=(bsz, n),
        in_specs=in_specs,
        out_specs=pl.BlockSpec((1, ROW_TILE, GLA_DV), lambda b, s: (b, blk(s), 0)),
        out_shape=jax.ShapeDtypeStruct((bsz, t, GLA_DV), BF16 if has_prev else F32),
        scratch_shapes=[pltpu.VMEM((GLA_HEADS, GLA_HEAD_K, GLA_HEAD_V), F32)],
        compiler_params=_cparams("parallel", "arbitrary"),
    )(*args)


def _gla(u_all, gate_w2, gate_b, norm_g, n_lat):
    w2pad = jnp.zeros((2, LANE, GLA_DK), F32)
    w2pad = w2pad.at[0, :GLA_RANK].set(gate_w2[0]).at[1, GLA_RANK:2 * GLA_RANK].set(gate_w2[1])
    gbias = gate_b.reshape(2, 1, GLA_DK).astype(F32)
    o_f = _gla_direction(u_all, w2pad, gbias, n_lat, False)
    return _gla_direction(u_all, w2pad, gbias, n_lat, True, prev=o_f, norm_g=norm_g.astype(F32))


CONV_CHUNK = 512


def _shift_rows(v, dh):
    return v if dh == 0 else pltpu.roll(v, (-dh) % v.shape[0], 0)


def _ssd_conv_kernel(n_lat, x_ref, w_ref, b_ref, o_ref, pad_ref):
    t = x_ref.shape[1]
    n_ctx = t - n_lat
    pad_ref[0:GRID_W, :] = jnp.zeros((GRID_W, LANE), F32)
    pad_ref[GRID_W + n_lat:2 * GRID_W + n_lat, :] = jnp.zeros((GRID_W, LANE), F32)
    pad_ref[GRID_W:GRID_W + n_lat, :] = x_ref[0, 0:n_lat, :]
    w = w_ref[...]
    bias = b_ref[...]
    col = lax.broadcasted_iota(jnp.int32, (CONV_CHUNK, LANE), 0) % GRID_W
    masks = {-1: col >= 1, 0: None, 1: col <= GRID_W - 2}

    def body(ci, carry):
        t0 = pl.multiple_of(ci * CONV_CHUNK, CONV_CHUNK)
        acc = jnp.zeros((CONV_CHUNK, LANE), F32) + bias
        for i in range(3):
            slab = pad_ref[pl.ds(t0 + i * GRID_W, CONV_CHUNK), :]
            for j in range(3):
                sh = _shift_rows(slab, j - 1)
                if masks[j - 1] is not None:
                    sh = jnp.where(masks[j - 1], sh, 0.0)
                acc = acc + sh * w[i * 3 + j:i * 3 + j + 1]
        o_ref[0, 0, pl.ds(t0, CONV_CHUNK), :] = _silu(acc)
        return carry

    lax.fori_loop(0, n_lat // CONV_CHUNK, body, 0)
    xc = x_ref[0, n_lat:t, :]
    pos = lax.broadcasted_iota(jnp.int32, (n_ctx, LANE), 0)
    acc = xc * w[4:5] + bias
    acc = acc + jnp.where(pos >= 1, _shift_rows(xc, -1), 0.0) * w[3:4]
    acc = acc + jnp.where(pos <= n_ctx - 2, _shift_rows(xc, 1), 0.0) * w[5:6]
    o_ref[0, 0, n_lat:t, :] = _silu(acc)


def _ssd_conv(u2, conv_w, conv_b, n_lat):
    bsz, t, _ = u2.shape
    n_tiles = SSD_CONV_DIM // LANE
    c0 = SSD_D_INNER // LANE
    return pl.pallas_call(
        functools.partial(_ssd_conv_kernel, n_lat),
        grid=(bsz, n_tiles),
        in_specs=[pl.BlockSpec((1, t, LANE), lambda b, j: (b, 0, c0 + j)),
                  pl.BlockSpec((9, LANE), lambda b, j: (0, j)),
                  pl.BlockSpec((1, LANE), lambda b, j: (0, j))],
        out_specs=pl.BlockSpec((1, 1, t, LANE), lambda b, j: (b, j, 0, 0)),
        out_shape=jax.ShapeDtypeStruct((bsz, n_tiles, t, LANE), F32),
        scratch_shapes=[pltpu.VMEM((n_lat + 2 * GRID_W, LANE), F32)],
        compiler_params=_cparams("parallel", "parallel"),
    )(u2, conv_w.reshape(9, SSD_CONV_DIM), conv_b.reshape(1, SSD_CONV_DIM))


SSD_CS = 128


def _softplus(x):
    return jnp.maximum(x, 0.0) + jnp.log(1.0 + jnp.exp(-jnp.abs(x)))


def _ssd_kernel(reverse, has_prev, nsub, *refs):
    if has_prev:
        (x_ref, b_ref, c_ref, dt_ref, p_ref, sel_ref, prev_ref, z_ref, ng_ref, o_ref, s_ref) = refs
    else:
        (x_ref, b_ref, c_ref, dt_ref, p_ref, sel_ref, o_ref, s_ref) = refs
    g = pl.program_id(2)

    @pl.when(pl.program_id(1) == 0)
    def _():
        s_ref[g] = jnp.zeros(s_ref.shape[1:], F32)

    cs = SSD_CS
    hw = SSD_HEADDIM
    row = lax.broadcasted_iota(jnp.int32, (cs, cs), 0)
    col = lax.broadcasted_iota(jnp.int32, (cs, cs), 1)
    keep = (col >= row) if reverse else (col <= row)
    cum_m = keep.astype(F32)
    last = 0 if reverse else cs - 1
    sel = sel_ref[0]
    pg = jnp.dot(p_ref[...], sel, precision=HIGHEST, preferred_element_type=F32)
    dtv = _softplus(jnp.dot(dt_ref[0], sel, precision=HIGHEST, preferred_element_type=F32) + pg[0:1])
    a = dtv * pg[1:2]
    for c in (range(nsub - 1, -1, -1) if reverse else range(nsub)):
        rs = slice(c * cs, (c + 1) * cs)
        acs = jnp.dot(cum_m, a[rs], precision=HIGHEST, preferred_element_type=F32)
        acs_t = acs.T
        dt_t = dtv[rs].T
        a_last = acs[last:last + 1]
        tail = jnp.exp(a_last - acs) * dtv[rs]
        e_acs = jnp.exp(acs)
        e_last = jnp.exp(a_last)
        bg = b_ref[0, 0, rs, :].astype(BF16)
        cg = c_ref[0, 0, rs, :].astype(BF16)
        cb = lax.dot_general(cg, bg, (((1,), (1,)), ((), ())), preferred_element_type=F32)
        ys = []
        for r in range(SSD_HPG):
            xh = x_ref[0, r // 2, rs, (r % 2) * hw:(r % 2 + 1) * hw]
            state = s_ref[g, r]
            decay = jnp.where(keep, jnp.exp(jnp.minimum(acs[:, r:r + 1] - acs_t[r:r + 1, :], 0.0)), 0.0)
            m = (cb * decay * dt_t[r:r + 1, :]).astype(BF16)
            y = jnp.dot(m, xh.astype(BF16), preferred_element_type=F32)
            y = y + e_acs[:, r:r + 1] * lax.dot_general(cg, state.astype(BF16), (((1,), (1,)), ((), ())),
                                                        preferred_element_type=F32)
            y = y + pg[2:3, r:r + 1] * xh
            xw = (xh * tail[:, r:r + 1]).astype(BF16)
            upd = lax.dot_general(xw, bg, (((0,), (0,)), ((), ())), preferred_element_type=F32)
            s_ref[g, r] = e_last[:, r:r + 1] * state + upd
            ys.append(y)
        if has_prev:
            yt = [jnp.concatenate(ys[2 * k:2 * k + 2], axis=1) + prev_ref[0, k, rs, :] for k in range(SSD_HPG // 2)]
            z = z_ref[0, rs, :]
            yt = [yt[k] * _silu(z[:, k * LANE:(k + 1) * LANE]) for k in range(SSD_HPG // 2)]
            ss = sum(jnp.sum(v * v, axis=-1, keepdims=True) for v in yt)
            inv = lax.rsqrt(ss * (1.0 / (SSD_HPG * hw)) + EPS)
            ng = ng_ref[...]
            for k in range(SSD_HPG // 2):
                o_ref[0, rs, k * LANE:(k + 1) * LANE] = (yt[k] * inv * ng[:, k * LANE:(k + 1) * LANE]).astype(o_ref.dtype)
        else:
            for k in range(SSD_HPG // 2):
                o_ref[0, k, rs, :] = jnp.concatenate(ys[2 * k:2 * k + 2], axis=1)


def _ssd_direction(xbc_t, u2, pvec, sel, n_lat, reverse, prev=None, norm_g=None):
    bsz, _, t, _ = xbc_t.shape
    n = t // ROW_TILE
    nl = n_lat // ROW_TILE
    nsub = ROW_TILE // SSD_CS
    tpg = SSD_HPG * SSD_HEADDIM // LANE
    gw = SSD_HPG * SSD_HEADDIM
    if reverse:
        blk = lambda s: n - 1 - s
    else:
        blk = lambda s: (s + nl) % n
    b0 = SSD_D_INNER // LANE
    c0 = b0 + SSD_GROUPS
    d = 1 if reverse else 0
    in_specs = [pl.BlockSpec((1, tpg, ROW_TILE, LANE), lambda b, s, g: (b, g, blk(s), 0)),
                pl.BlockSpec((1, 1, ROW_TILE, LANE), lambda b, s, g: (b, b0 + g, blk(s), 0)),
                pl.BlockSpec((1, 1, ROW_TILE, LANE), lambda b, s, g: (b, c0 + g, blk(s), 0)),
                pl.BlockSpec((1, ROW_TILE, LANE), lambda b, s, g: (b, blk(s), (SSD_IN - 2 * SSD_HEADS) // LANE)),
                pl.BlockSpec((8, LANE), lambda b, s, g: (0, 0)),
                pl.BlockSpec((1, LANE, LANE), lambda b, s, g: (d * SSD_GROUPS + g, 0, 0))]
    args = [xbc_t, xbc_t, xbc_t, u2, pvec, sel]
    has_prev = prev is not None
    if has_prev:
        in_specs += [pl.BlockSpec((1, tpg, ROW_TILE, LANE), lambda b, s, g: (b, g, blk(s), 0)),
                     pl.BlockSpec((1, ROW_TILE, gw), lambda b, s, g: (b, blk(s), g)),
                     pl.BlockSpec((1, gw), lambda b, s, g: (0, g))]
        args += [prev, u2, norm_g.reshape(1, SSD_D_INNER)]
        out_spec = pl.BlockSpec((1, ROW_TILE, gw), lambda b, s, g: (b, blk(s), g))
        out_shape = jax.ShapeDtypeStruct((bsz, t, SSD_D_INNER), BF16)
    else:
        out_spec = pl.BlockSpec((1, tpg, ROW_TILE, LANE), lambda b, s, g: (b, g, blk(s), 0))
        out_shape = jax.ShapeDtypeStruct((bsz, SSD_D_INNER // LANE, t, LANE), F32)
    return pl.pallas_call(
        functools.partial(_ssd_kernel, reverse, has_prev, nsub),
        grid=(bsz, n, SSD_GROUPS),
        in_specs=in_specs,
        out_specs=out_spec,
        out_shape=out_shape,
        scratch_shapes=[pltpu.VMEM((SSD_GROUPS, SSD_HPG, SSD_HEADDIM, SSD_STATE), F32)],
        compiler_params=_cparams("parallel", "arbitrary", "arbitrary"),
    )(*args)


def _ssd(xbc_t, u2, dt_bias, a_log, d_skip, norm_g, n_lat):
    pvec = jnp.zeros((8, LANE), F32)
    pvec = pvec.at[0].set(dt_bias.reshape(-1)).at[1].set(-jnp.exp(a_log.astype(F32)).reshape(-1))
    pvec = pvec.at[2].set(d_skip.reshape(-1))
    lane = np.arange(LANE)
    sel_np = np.zeros((2 * SSD_GROUPS, LANE, LANE), np.float32)
    for dd in range(2):
        for gg in range(SSD_GROUPS):
            for r in range(SSD_HPG):
                sel_np[dd * SSD_GROUPS + gg, dd * SSD_HEADS + gg * SSD_HPG + r, r] = 1.0
    sel = jnp.asarray(sel_np)
    y_f = _ssd_direction(xbc_t, u2, pvec, sel, n_lat, False)
    return _ssd_direction(xbc_t, u2, pvec, sel, n_lat, True, prev=y_f, norm_g=norm_g.astype(F32))


HY_PAD = 8


def _hy_conv_kernel(n_lat, x_ref, w_ref, b_ref, o_ref, pad_ref):
    t = x_ref.shape[1]
    w = w_ref[...]
    bias = b_ref[...]
    ch = CONV_CHUNK
    pad_ref[0:HY_PAD, :] = jnp.zeros((HY_PAD, LANE), F32)
    pad_ref[HY_PAD + n_lat:2 * HY_PAD + n_lat, :] = jnp.zeros((HY_PAD, LANE), F32)
    pad_ref[HY_PAD:HY_PAD + n_lat, :] = x_ref[0, 0:n_lat, :]
    rowi = lax.broadcasted_iota(jnp.int32, (ch, LANE), 0)

    def body(ci, carry):
        t0 = pl.multiple_of(ci * ch, ch)
        cur = pad_ref[pl.ds(t0 + HY_PAD, ch), :]
        before = pad_ref[pl.ds(t0, HY_PAD), :][HY_PAD - 1:HY_PAD]
        after = pad_ref[pl.ds(t0 + HY_PAD + ch, HY_PAD), :][0:1]
        down = jnp.where(rowi == 0, before, _shift_rows(cur, -1))
        up = jnp.where(rowi == ch - 1, after, _shift_rows(cur, 1))
        o_ref[0, 0, pl.ds(t0, ch), :] = down * w[0:1] + cur * w[1:2] + up * w[2:3] + bias
        return carry

    lax.fori_loop(0, n_lat // ch, body, 0)
    n_ctx = t - n_lat
    xc = x_ref[0, n_lat:t, :]
    pos = lax.broadcasted_iota(jnp.int32, (n_ctx, LANE), 0)
    acc = xc * w[1:2] + bias
    acc = acc + jnp.where(pos >= 1, _shift_rows(xc, -1), 0.0) * w[0:1]
    acc = acc + jnp.where(pos <= n_ctx - 2, _shift_rows(xc, 1), 0.0) * w[2:3]
    o_ref[0, 0, n_lat:t, :] = acc


def _hy_conv(u_all, conv_w, conv_b, n_lat):
    bsz, t, _ = u_all.shape
    nch = (HY_ORDER + 1) * HY_WIDTH
    tpp = HY_WIDTH // LANE
    return pl.pallas_call(
        functools.partial(_hy_conv_kernel, n_lat),
        grid=(bsz, nch // LANE),
        in_specs=[pl.BlockSpec((1, t, LANE), lambda b, j: (b, 0, j)),
                  pl.BlockSpec((3, LANE), lambda b, j: (0, j)),
                  pl.BlockSpec((1, LANE), lambda b, j: (0, j))],
        out_specs=pl.BlockSpec((1, 1, t, LANE), lambda b, j: (j // tpp, b, 0, j % tpp)),
        out_shape=jax.ShapeDtypeStruct((HY_ORDER + 1, bsz, t, HY_WIDTH), F32),
        scratch_shapes=[pltpu.VMEM((n_lat + 2 * HY_PAD, LANE), F32)],
        compiler_params=_cparams("parallel", "parallel"),
    )(u_all, conv_w, conv_b.reshape(1, nch))


FILT_ROWS = 256


def _hy_filter_kernel(seq_len, bands_ref, w1_ref, b1_ref, fr1_ref, w2_ref, b2_ref, fr2_ref, w3_ref, dl_ref, o_ref):
    i = pl.program_id(1)
    n = (i * FILT_ROWS + lax.broadcasted_iota(jnp.int32, (FILT_ROWS, 1), 0))
    lag = jnp.where(n < seq_len, n, 2 * seq_len - n)
    pos = lag.astype(F32)
    tt = pos / max(seq_len - 1, 1)
    ang = 2.0 * math.pi * bands_ref[...] * pos / seq_len
    w1 = w1_ref[...]
    nb = HY_BANDS
    pre = tt * w1[0:1] + jnp.dot(jnp.cos(ang), w1[1:1 + nb], precision=HIGHEST, preferred_element_type=F32)
    pre = pre - jnp.dot(jnp.sin(ang), w1[1 + nb:1 + 2 * nb], precision=HIGHEST, preferred_element_type=F32)
    hid = jnp.sin(fr1_ref[...] * (pre + b1_ref[...]))
    hid = jnp.sin(fr2_ref[...] * (jnp.dot(hid, w2_ref[...], precision=HIGHEST, preferred_element_type=F32) + b2_ref[...]))
    h = jnp.dot(hid, w3_ref[...], precision=HIGHEST, preferred_element_type=F32)
    h = h * jnp.exp(-tt * dl_ref[...])
    o_ref[0] = jnp.where(n == seq_len, 0.0, h)


def _hy_filter(seq_len, f_w1, f_b1, f_fr1, f_w2, f_b2, f_fr2, f_w3):
    fh = f_w1.shape[1]
    emb = f_w1.shape[0]
    half = seq_len // FILT_ROWS
    bands = jnp.asarray(np.linspace(1e-4, HY_BANDS - 1, HY_BANDS, dtype=np.float32)).reshape(1, HY_BANDS)
    deltas = jnp.asarray(np.abs(np.linspace(HY_MIN_DECAY, HY_MAX_DECAY, HY_WIDTH, dtype=np.float32))).reshape(1, HY_WIDTH)
    vec = lambda v: v.reshape(1, fh).astype(F32)
    full = lambda shape: pl.BlockSpec(shape, lambda o, i: (0,) * len(shape))
    return pl.pallas_call(
        functools.partial(_hy_filter_kernel, seq_len),
        grid=(HY_ORDER, 2 * half),
        in_specs=[full((1, HY_BANDS)), full((emb, fh)), full((1, fh)), full((1, fh)), full((fh, fh)), full((1, fh)),
                  full((1, fh)),
                  pl.BlockSpec((fh, HY_WIDTH), lambda o, i: (0, 2 * o + i // half)),
                  full((1, HY_WIDTH))],
        out_specs=pl.BlockSpec((1, FILT_ROWS, HY_WIDTH), lambda o, i: (o, i, 0)),
        out_shape=jax.ShapeDtypeStruct((HY_ORDER, 2 * seq_len, HY_WIDTH), F32),
        compiler_params=_cparams("parallel", "parallel"),
    )(bands, f_w1.astype(F32), vec(f_b1), vec(f_fr1), f_w2.astype(F32), vec(f_b2), vec(f_fr2), f_w3.astype(F32), deltas)


DFT_N2 = 256
DFT_S = 4


def _dft_tables(n1):
    n = n1 * DFT_N2
    k1h = n1 // 2 + 1
    k1p = -(-k1h // 8) * 8
    k1 = np.arange(k1p)[:, None].astype(np.float64)
    valid = (np.arange(k1p) < k1h)[:, None]
    th1 = 2.0 * np.pi * k1 * np.arange(n1)[None, :] / n1
    f1 = np.concatenate([np.where(valid, np.cos(th1), 0.0), np.where(valid, -np.sin(th1), 0.0)], axis=0)
    tw = 2.0 * np.pi * k1 * np.arange(DFT_N2)[None, :] / n
    tw_re = np.repeat(np.where(valid, np.cos(tw), 0.0)[:, :, None], LANE, axis=2)
    tw_im = np.repeat(np.where(valid, -np.sin(tw), 0.0)[:, :, None], LANE, axis=2)
    ph = 2.0 * np.pi * np.outer(np.arange(DFT_N2), np.arange(DFT_N2)) / DFT_N2
    f2re, f2im = np.cos(ph), -np.sin(ph)
    w_fwd = np.block([[f2re, -f2im], [f2im, f2re]])
    w_inv = np.block([[f2re, f2im], [-f2im, f2re]])
    wgt = np.where((np.arange(k1p) == 0) | (np.arange(k1p) == n1 // 2), 1.0, 2.0) * (np.arange(k1p) < k1h)
    th_i = 2.0 * np.pi * np.arange(n1 // 2)[:, None] * np.arange(k1p)[None, :] / n1
    g = np.concatenate([wgt * np.cos(th_i), -wgt * np.sin(th_i)], axis=1) / n
    f = lambda a: jnp.asarray(a.astype(np.float32))
    return dict(k1h=k1h, k1p=k1p, f1=f(f1), tw_re=f(tw_re), tw_im=f(tw_im), w_fwd=f(w_fwd), w_inv=f(w_inv), g=f(g))


def _dft_dot(a, b):
    return jnp.dot(a, b, precision=HIGHEST, preferred_element_type=F32)


def _dft_first_kernel(k1p, x_ref, f1_ref, twr_ref, twi_ref, ore_ref, oim_ref):
    a = _dft_dot(f1_ref[...], x_ref[0])
    are, aim = a[:k1p], a[k1p:]
    twr, twi = twr_ref[...], twi_ref[...]
    cw = x_ref.shape[2] // DFT_S
    for j in range(DFT_S):
        tr, ti = twr[:, j * LANE:(j + 1) * LANE], twi[:, j * LANE:(j + 1) * LANE]
        for c in range(cw // LANE):
            sl = slice(j * cw + c * LANE, j * cw + (c + 1) * LANE)
            ore_ref[0, :, sl] = are[:, sl] * tr - aim[:, sl] * ti
            oim_ref[0, :, sl] = are[:, sl] * ti + aim[:, sl] * tr


def _dft_first(x3, f1, tab):
    bsz = x3.shape[0]
    rows = f1.shape[1]
    ncol = x3.shape[2]
    cw = ncol // DFT_N2
    k1p = tab['k1p']
    out = jax.ShapeDtypeStruct((bsz, k1p, ncol), F32)
    ospec = pl.BlockSpec((1, k1p, DFT_S * cw), lambda b, j: (b, 0, j))
    return pl.pallas_call(
        functools.partial(_dft_first_kernel, k1p),
        grid=(bsz, DFT_N2 // DFT_S),
        in_specs=[pl.BlockSpec((1, rows, DFT_S * cw), lambda b, j: (b, 0, j)),
                  pl.BlockSpec((2 * k1p, rows), lambda b, j: (0, 0)),
                  pl.BlockSpec((k1p, DFT_S * LANE), lambda b, j: (0, j)),
                  pl.BlockSpec((k1p, DFT_S * LANE), lambda b, j: (0, j))],
        out_specs=[ospec, ospec],
        out_shape=[out, out],
        compiler_params=_cparams("parallel", "parallel"),
    )(x3, f1, tab['tw_re'].reshape(k1p, DFT_N2 * LANE), tab['tw_im'].reshape(k1p, DFT_N2 * LANE))


DFT_TC = 512


def _dft_mid_kernel(k1h, fused, *refs):
    if fused:
        are_ref, aim_ref, hre_ref, him_ref, wf_ref, wi_ref, twr_ref, twi_ref, ore_ref, oim_ref = refs
    else:
        are_ref, aim_ref, wf_ref, ore_ref, oim_ref = refs
    n2 = DFT_N2

    @pl.when(pl.program_id(0) < k1h)
    def _():
        a = jnp.concatenate([are_ref[0, 0], aim_ref[0, 0]], axis=0)
        x = _dft_dot(wf_ref[...], a)
        xre, xim = x[:n2], x[n2:]
        if not fused:
            ore_ref[0, 0] = xre
            oim_ref[0, 0] = xim
        else:
            hre, him = hre_ref[0], him_ref[0]
            y = jnp.concatenate([xre * hre - xim * him, xre * him + xim * hre], axis=0)
            bb = _dft_dot(wi_ref[...], y)
            bre, bim = bb[:n2], bb[n2:]
            twr, twi = twr_ref[0], twi_ref[0]
            for c in range(bre.shape[1] // LANE):
                sl = slice(c * LANE, (c + 1) * LANE)
                ore_ref[0, 0, :, sl] = bre[:, sl] * twr + bim[:, sl] * twi
                oim_ref[0, 0, :, sl] = bim[:, sl] * twr - bre[:, sl] * twi

    @pl.when(pl.program_id(0) >= k1h)
    def _():
        ore_ref[...] = jnp.zeros_like(ore_ref)
        oim_ref[...] = jnp.zeros_like(oim_ref)


def _dft_mid(are, aim, tab, spec=None):
    bsz, k1p, ncol = are.shape
    cw = ncol // DFT_N2
    tc = min(DFT_TC, cw)
    a4 = lambda v: v.reshape(bsz, k1p, DFT_N2, cw)
    blk = pl.BlockSpec((1, 1, DFT_N2, tc), lambda k, c, b: (b, k, 0, c))
    wspec = pl.BlockSpec((2 * DFT_N2, 2 * DFT_N2), lambda k, c, b: (0, 0))
    fused = spec is not None
    if fused:
        hspec = pl.BlockSpec((1, DFT_N2, tc), lambda k, c, b: (k, 0, c))
        tspec = pl.BlockSpec((1, DFT_N2, LANE), lambda k, c, b: (k, 0, 0))
        in_specs = [blk, blk, hspec, hspec, wspec, wspec, tspec, tspec]
        args = [a4(are), a4(aim), spec[0], spec[1], tab['w_fwd'], tab['w_inv'], tab['tw_re'], tab['tw_im']]
    else:
        in_specs = [blk, blk, wspec]
        args = [a4(are), a4(aim), tab['w_fwd']]
    out = jax.ShapeDtypeStruct((bsz, k1p, DFT_N2, cw), F32)
    ore, oim = pl.pallas_call(
        functools.partial(_dft_mid_kernel, tab['k1h'], fused),
        grid=(k1p, cw // tc, bsz),
        in_specs=in_specs,
        out_specs=[blk, blk],
        out_shape=[out, out],
        compiler_params=_cparams("parallel", "parallel", "arbitrary"),
    )(*args)
    return ore, oim


def _dft_last_kernel(bre_ref, bim_ref, g_ref, u_ref, x_ref, d_ref, o_ref):
    bb = jnp.concatenate([bre_ref[0], bim_ref[0]], axis=0)
    y = _dft_dot(g_ref[...], bb)
    u = u_ref[0]
    o_ref[0] = (x_ref[0] * (y + u * d_ref[...])).astype(o_ref.dtype)


def _dft_last(bre, bim, tab, u3, x3, dvec, out_dtype):
    bsz, k1p, n2, cw = bre.shape
    ncol = n2 * cw
    rows = tab['g'].shape[0]
    dt = jnp.tile(dvec.reshape(1, cw).astype(F32), (1, DFT_S))
    bspec = pl.BlockSpec((1, k1p, DFT_S * cw), lambda b, j: (b, 0, j))
    rspec = pl.BlockSpec((1, rows, DFT_S * cw), lambda b, j: (b, 0, j))
    return pl.pallas_call(
        _dft_last_kernel,
        grid=(bsz, DFT_N2 // DFT_S),
        in_specs=[bspec, bspec, pl.BlockSpec((rows, 2 * k1p), lambda b, j: (0, 0)), rspec, rspec,
                  pl.BlockSpec((1, DFT_S * cw), lambda b, j: (0, 0))],
        out_specs=rspec,
        out_shape=jax.ShapeDtypeStruct((bsz, rows, ncol), out_dtype),
        compiler_params=_cparams("parallel", "parallel"),
    )(bre.reshape(bsz, k1p, ncol), bim.reshape(bsz, k1p, ncol), tab['g'], u3, x3, dt)


def _hy_ctx_kernel(p_ref, f_ref, d_ref, o_ref):
    n = p_ref.shape[2]
    pos = lax.broadcasted_iota(jnp.int32, (n, LANE), 0)
    y = p_ref[0, 0]
    for o in range(HY_ORDER):
        def body(m, acc):
            hc = f_ref[o, pl.ds(m, 1), :]
            ha = f_ref[o, pl.ds(2 * n - m, 1), :]
            acc = acc + jnp.where(pos >= m, pltpu.roll(y, m, 0), 0.0) * hc
            return acc + jnp.where(pos < n - m, pltpu.roll(y, n - m, 0), 0.0) * ha
        conv = lax.fori_loop(1, n, body, y * f_ref[o, 0:1, :])
        y = p_ref[o + 1, 0] * (conv + y * d_ref[o:o + 1, :])
    o_ref[0] = y.astype(o_ref.dtype)


def _hy_ctx(parts, filt_c, long_bias, n_lat):
    nparts, bsz, t, cw = parts.shape
    n_ctx = t - n_lat
    return pl.pallas_call(
        _hy_ctx_kernel,
        grid=(bsz, cw // LANE),
        in_specs=[pl.BlockSpec((nparts, 1, n_ctx, LANE), lambda b, c: (0, b, n_lat // n_ctx, c)),
                  pl.BlockSpec((HY_ORDER, 2 * n_ctx, LANE), lambda b, c: (0, 0, c)),
                  pl.BlockSpec((HY_ORDER, LANE), lambda b, c: (0, c))],
        out_specs=pl.BlockSpec((1, n_ctx, LANE), lambda b, c: (b, 0, c)),
        out_shape=jax.ShapeDtypeStruct((bsz, n_ctx, cw), BF16),
        compiler_params=_cparams("parallel", "parallel"),
    )(parts, filt_c, long_bias.astype(F32))


def _hyena(u_all, hy_p, n_lat):
    conv_w, conv_b, f_w1, f_b1, f_fr1, f_w2, f_b2, f_fr2, f_w3, long_bias = hy_p
    bsz, t, _ = u_all.shape
    n_ctx = t - n_lat
    assert n_ctx == DFT_N2 and n_lat % (2 * DFT_N2) == 0
    n1 = 2 * n_lat // DFT_N2
    tab = _dft_tables(n1)
    cw = HY_WIDTH
    parts = _hy_conv(u_all, conv_w, conv_b, n_lat)
    filt_l = _hy_filter(n_lat, f_w1, f_b1, f_fr1, f_w2, f_b2, f_fr2, f_w3)
    filt_c = _hy_filter(n_ctx, f_w1, f_b1, f_fr1, f_w2, f_b2, f_fr2, f_w3)
    hre, him = _dft_mid(*_dft_first(filt_l.reshape(HY_ORDER, n1, DFT_N2 * cw), tab['f1'], tab), tab)
    f1_half = tab['f1'][:, :n1 // 2]
    rows3 = lambda v: v.reshape(bsz, -1, DFT_N2 * cw)
    y = rows3(parts[0])
    for o in range(HY_ORDER):
        are, aim = _dft_first(y, f1_half, tab)
        bre, bim = _dft_mid(are, aim, tab, spec=(hre[o], him[o]))
        y = _dft_last(bre, bim, tab, y, rows3(parts[o + 1]), long_bias[o], F32 if o + 1 < HY_ORDER else BF16)
    y_ctx = _hy_ctx(parts, filt_c, long_bias, n_lat)
    return jnp.concatenate([y.reshape(bsz, n_lat, cw), y_ctx], axis=1)


def _row_tile(m):
    for tm in (1280, 1024, 512, 256):
        if m % tm == 0:
            return tm
    raise ValueError(m)


def _col_tile(n):
    for k in range(n // LANE, 0, -1):
        if n % (k * LANE) == 0 and k * LANE <= 1280:
            return k * LANE
    raise ValueError(n)


def _in_proj(h, w):
    bsz, t, d = h.shape
    n = w.shape[1]
    return _matmul(h.reshape(bsz * t, d), w, _row_tile(bsz * t), _col_tile(n)).reshape(bsz, t, n)


def kernel(x, c, ctx, c_ctx, router_w, router_b, mod_w, mod_b, ln_g, ln_b, exp_w_gate, exp_w_up, exp_w_down, ab_w_in, ab_w_out, hy_conv_w, hy_conv_b, hy_f_w1, hy_f_b1, hy_f_fr1, hy_f_w2, hy_f_b2, hy_f_fr2, hy_f_w3, hy_long_bias, gla_gate_w2, gla_gate_b, gla_norm_g, ssd_w_in, ssd_conv_w, ssd_conv_b, ssd_dt_bias, ssd_a_log, ssd_d, ssd_norm_g, ssd_w_out):
    bsz, n_lat, d = x.shape
    n_ctx = ctx.shape[1]
    assert bsz < 8 and n_lat % ROW_TILE == 0 and n_ctx % ROW_TILE == 0 and d == D_MODEL
    xa = jnp.concatenate([x, ctx], axis=1).astype(F32)
    c8 = jnp.zeros((8, d), F32).at[:bsz].set(c).at[bsz].set(c_ctx)
    router_w_pad = jnp.zeros((d, LANE), F32).at[:, :N_EXPERTS].set(router_w)
    mods = [_mod_vectors(c8, mod_w[i].astype(F32), mod_b[i].astype(F32)).reshape(8, 6, d) for i in range(DEPTH)]

    def rows(i, idx_l):
        lat = jnp.stack([mods[i][:bsz, k] for k in idx_l], axis=1)
        cx = jnp.broadcast_to(jnp.stack([mods[i][bsz, k] for k in idx_l], axis=0)[None], lat.shape)
        return jnp.concatenate([lat, cx], axis=1)

    def with_ln(p, g, b):
        extra = jnp.broadcast_to(jnp.stack([g, b], axis=0).astype(F32)[None], (bsz, 2, d))
        return jnp.concatenate([p, extra], axis=1)

    h = _modulate(xa, rows(0, (0, 1)), n_lat)
    for i in range(DEPTH):
        j = i // 2
        if i % 2 == 0:
            w_pad = -(-AB_IN // LANE) * LANE
            w_in = jnp.zeros((d, w_pad), BF16).at[:, :AB_IN].set(ab_w_in[j].astype(BF16))
            u_all = _in_proj(h, w_in)
            hy_p = (hy_conv_w[j], hy_conv_b[j], hy_f_w1[j], hy_f_b1[j], hy_f_fr1[j], hy_f_w2[j], hy_f_b2[j],
                    hy_f_fr2[j], hy_f_w3[j], hy_long_bias[j])
            y_hy = _hyena(u_all, hy_p, n_lat)
            y_gla = _gla(u_all, gla_gate_w2[j], gla_gate_b[j], gla_norm_g[j], n_lat)
            w_out = ab_w_out[j].astype(BF16)
            a_list, w_list = [y_hy, y_gla], [w_out[:HY_WIDTH], w_out[HY_WIDTH:]]
        else:
            u2 = _in_proj(h, ssd_w_in[j].astype(BF16))
            xbc_t = _ssd_conv(u2, ssd_conv_w[j].astype(F32), ssd_conv_b[j].astype(F32), n_lat)
            y_ssd = _ssd(xbc_t, u2, ssd_dt_bias[j].astype(F32), ssd_a_log[j], ssd_d[j].astype(F32), ssd_norm_g[j], n_lat)
            a_list, w_list = [y_ssd], [ssd_w_out[j].astype(BF16)]
        p1 = with_ln(rows(i, (2, 4, 3)), ln_g[i, 0], ln_b[i, 0])
        xa, tok, logits = _proj_update(a_list, w_list, xa, p1, router_w_pad, n_lat)
        y0, y1, gates = _moe(tok, logits, router_b, exp_w_gate[i].astype(BF16), exp_w_up[i].astype(BF16),
                             exp_w_down[i].astype(BF16))
        gate2 = rows(i, (5,))
        if i + 1 < DEPTH:
            nxt = rows(i + 1, (1, 0))
        else:
            nxt = jnp.zeros((bsz, 4, d), F32)
        p2 = jnp.concatenate([gate2[:, 0:1], nxt[:, 0:2], gate2[:, 1:2], nxt[:, 2:4]], axis=1)
        xa, h = _combine_update(y0, y1, gates, xa, with_ln(p2, ln_g[i, 1], ln_b[i, 1]), n_lat)
    return xa[:, :n_lat].astype(x.dtype)
```

```python
import functools
import math

import numpy as np
import jax
import jax.numpy as jnp
from jax import lax
from jax.experimental import pallas as pl
from jax.experimental.pallas import tpu as pltpu

F32 = jnp.float32
BF16 = jnp.bfloat16
HIGHEST = lax.Precision.HIGHEST

D_MODEL = 2048
DEPTH = 2
GRID_W = 64
HY_WIDTH = D_MODEL // 2
HY_ORDER = 2
HY_BANDS = 16
HY_MIN_DECAY = math.log(1e-2) / 1.5
HY_MAX_DECAY = math.log(1e-2) / 0.3
GLA_HEADS = 4
GLA_DK = D_MODEL // 4
GLA_DV = D_MODEL // 2
GLA_HEAD_K = GLA_DK // GLA_HEADS
GLA_HEAD_V = GLA_DV // GLA_HEADS
GLA_RANK = 16
GLA_TAU = 16.0
AB_SIZES = ((HY_ORDER + 1) * HY_WIDTH, GLA_DK, GLA_DK, GLA_DV, GLA_DV, GLA_RANK, GLA_RANK)
AB_IN = sum(AB_SIZES)
SSD_D_INNER = 2 * D_MODEL
SSD_HEADDIM = 64
SSD_HEADS = SSD_D_INNER // SSD_HEADDIM
SSD_GROUPS = 8
SSD_HPG = SSD_HEADS // SSD_GROUPS
SSD_STATE = 128
SSD_CONV_DIM = SSD_D_INNER + 2 * SSD_GROUPS * SSD_STATE
SSD_IN = SSD_D_INNER + SSD_CONV_DIM + 2 * SSD_HEADS
N_EXPERTS = 16
N_EXPERT_GROUPS = 4
EXPERTS_PER_GROUP = N_EXPERTS // N_EXPERT_GROUPS
TOP_K = 2
D_EXPERT = D_MODEL // 2
ALPHA = (2 * DEPTH) ** 0.25
EPS = 1e-6

LANE = 128
ROW_TILE = 256
MOE_BLOCK = 256
VMEM_LIMIT = 56 * 1024 * 1024


def _cparams(*sem):
    return pltpu.CompilerParams(dimension_semantics=sem, vmem_limit_bytes=VMEM_LIMIT)


def _silu(x):
    return x * (1.0 / (1.0 + jnp.exp(-x)))


def _mod_kernel(c_ref, w_ref, b_ref, o_ref):
    o_ref[...] = jnp.dot(_silu(c_ref[...]), w_ref[...], precision=HIGHEST, preferred_element_type=F32) + b_ref[...]


def _mod_vectors(c8, w, b):
    d, n = w.shape
    tn = 1536
    return pl.pallas_call(
        _mod_kernel,
        grid=(n // tn,),
        in_specs=[pl.BlockSpec((8, d), lambda j: (0, 0)),
                  pl.BlockSpec((d, tn), lambda j: (0, j)),
                  pl.BlockSpec((1, tn), lambda j: (0, j))],
        out_specs=pl.BlockSpec((8, tn), lambda j: (0, j)),
        out_shape=jax.ShapeDtypeStruct((8, n), F32),
        compiler_params=_cparams("arbitrary"),
        name="adaln_vectors",
    )(c8, w, b.reshape(1, n))


def _modulate_kernel(n_lat_tiles, x_ref, p_ref, o_ref):
    is_ctx = pl.program_id(1) >= n_lat_tiles
    p = p_ref[0]
    shift = jnp.where(is_ctx, p[2:3], p[0:1])
    scale = jnp.where(is_ctx, p[3:4], p[1:2])
    o_ref[0] = (x_ref[0] * (1.0 + scale) + shift).astype(o_ref.dtype)


def _modulate(x, params, n_lat):
    bsz, t, d = x.shape
    return pl.pallas_call(
        functools.partial(_modulate_kernel, n_lat // ROW_TILE),
        grid=(bsz, t // ROW_TILE),
        in_specs=[pl.BlockSpec((1, ROW_TILE, d), lambda b, i: (b, i, 0)),
                  pl.BlockSpec((1, 4, d), lambda b, i: (b, 0, 0))],
        out_specs=pl.BlockSpec((1, ROW_TILE, d), lambda b, i: (b, i, 0)),
        out_shape=jax.ShapeDtypeStruct((bsz, t, d), BF16),
        compiler_params=_cparams("parallel", "parallel"),
        name="modulate",
    )(x, params)


def _matmul_kernel(a_ref, w_ref, o_ref):
    o_ref[...] = jnp.dot(a_ref[...], w_ref[...], preferred_element_type=F32).astype(o_ref.dtype)


def _matmul(a, w, tm, tn, out_dtype=F32):
    m, k = a.shape
    n = w.shape[1]
    return pl.pallas_call(
        _matmul_kernel,
        grid=(m // tm, n // tn),
        in_specs=[pl.BlockSpec((tm, k), lambda i, j: (i, 0)),
                  pl.BlockSpec((k, tn), lambda i, j: (0, j))],
        out_specs=pl.BlockSpec((tm, tn), lambda i, j: (i, j)),
        out_shape=jax.ShapeDtypeStruct((m, n), out_dtype),
        compiler_params=_cparams("parallel", "arbitrary"),
        name="in_proj",
    )(a, w)


def _layer_norm_rows(v, g, b):
    mu = jnp.mean(v, axis=-1, keepdims=True)
    vc = v - mu
    var = jnp.mean(vc * vc, axis=-1, keepdims=True)
    return vc * lax.rsqrt(var + EPS) * g + b


def _post_update(n_lat_tiles, y, x, p):
    is_ctx = pl.program_id(1) >= n_lat_tiles
    gate = jnp.where(is_ctx, p[3:4], p[0:1])
    scale = jnp.where(is_ctx, p[4:5], p[1:2])
    shift = jnp.where(is_ctx, p[5:6], p[2:3])
    xn = _layer_norm_rows(ALPHA * x + gate * y, p[6:7], p[7:8])
    return xn, xn * (1.0 + scale) + shift


def _proj_update_kernel(n_lat_tiles, n_a, *refs):
    a_refs = refs[:n_a]
    w_refs = refs[n_a:2 * n_a]
    x_ref, p_ref, rw_ref, xo_ref, tok_ref, lg_ref = refs[2 * n_a:]
    y = jnp.dot(a_refs[0][0].astype(BF16), w_refs[0][...], preferred_element_type=F32)
    for a_ref, w_ref in zip(a_refs[1:], w_refs[1:]):
        y = y + jnp.dot(a_ref[0].astype(BF16), w_ref[...], preferred_element_type=F32)
    xn, tok = _post_update(n_lat_tiles, y, x_ref[0], p_ref[0])
    xo_ref[0] = xn
    tok_ref[0] = tok.astype(tok_ref.dtype)
    lg_ref[0] = jnp.dot(tok, rw_ref[...], precision=HIGHEST, preferred_element_type=F32)


def _proj_update(a_list, w_list, x, params, router_w_pad, n_lat):
    bsz, t, d = x.shape
    n_a = len(a_list)
    in_specs = [pl.BlockSpec((1, ROW_TILE, a.shape[-1]), lambda b, i: (b, i, 0)) for a in a_list]
    in_specs += [pl.BlockSpec(w.shape, lambda b, i: (0, 0)) for w in w_list]
    in_specs += [pl.BlockSpec((1, ROW_TILE, d), lambda b, i: (b, i, 0)),
                 pl.BlockSpec((1, 8, d), lambda b, i: (b, 0, 0)),
                 pl.BlockSpec((d, LANE), lambda b, i: (0, 0))]
    return pl.pallas_call(
        functools.partial(_proj_update_kernel, n_lat // ROW_TILE, n_a),
        grid=(bsz, t // ROW_TILE),
        in_specs=in_specs,
        out_specs=[pl.BlockSpec((1, ROW_TILE, d), lambda b, i: (b, i, 0)),
                   pl.BlockSpec((1, ROW_TILE, d), lambda b, i: (b, i, 0)),
                   pl.BlockSpec((1, ROW_TILE, LANE), lambda b, i: (b, i, 0))],
        out_shape=[jax.ShapeDtypeStruct((bsz, t, d), F32),
                   jax.ShapeDtypeStruct((bsz, t, d), BF16),
                   jax.ShapeDtypeStruct((bsz, t, LANE), F32)],
        compiler_params=_cparams("parallel", "parallel"),
        name="out_proj_ln",
    )(*a_list, *w_list, x, params, router_w_pad)


def _route_kernel(lg_ref, rb_ref, tri_ref, oi_ref, og_ref, cnt_ref, carry_ref):
    @pl.when(pl.program_id(0) == 0)
    def _():
        carry_ref[...] = jnp.zeros_like(carry_ref)

    tm = lg_ref.shape[0]
    epg = EXPERTS_PER_GROUP
    lt = lg_ref[...].T[:N_EXPERTS]
    score = 1.0 / (1.0 + jnp.exp(-lt))
    sel = score + rb_ref[...]
    best_g = None
    for q in range(N_EXPERT_GROUPS):
        rows = [sel[q * epg + r:q * epg + r + 1] for r in range(epg)]
        gs = None
        for a in range(epg):
            for b in range(a + 1, epg):
                ps = rows[a] + rows[b]
                gs = ps if gs is None else jnp.maximum(gs, ps)
        if best_g is None:
            best_g, grp = gs, jnp.zeros_like(gs, dtype=jnp.int32)
        else:
            better = gs > best_g
            grp = jnp.where(better, q, grp)
            best_g = jnp.where(better, gs, best_g)
    in_sel, in_score = [], []
    for r in range(epg):
        v = sel[r:r + 1]
        s = score[r:r + 1]
        for q in range(1, N_EXPERT_GROUPS):
            v = jnp.where(grp == q, sel[q * epg + r:q * epg + r + 1], v)
            s = jnp.where(grp == q, score[q * epg + r:q * epg + r + 1], s)
        in_sel.append(v)
        in_score.append(s)
    i1, v1 = jnp.zeros_like(grp), in_sel[0]
    for r in range(1, epg):
        better = in_sel[r] > v1
        i1 = jnp.where(better, r, i1)
        v1 = jnp.where(better, in_sel[r], v1)
    i2, v2 = None, None
    for r in range(epg):
        cand = jnp.where(i1 == r, -jnp.inf, in_sel[r])
        if i2 is None:
            i2, v2 = jnp.zeros_like(grp), cand
        else:
            better = cand > v2
            i2 = jnp.where(better, r, i2)
            v2 = jnp.where(better, cand, v2)
    s1, s2 = in_score[0], in_score[0]
    for r in range(1, epg):
        s1 = jnp.where(i1 == r, in_score[r], s1)
        s2 = jnp.where(i2 == r, in_score[r], s2)
    e1 = grp * epg + i1
    e2 = grp * epg + i2
    tot = s1 + s2
    g1, g2 = s1 / tot, s2 / tot
    eid = lax.broadcasted_iota(jnp.int32, (N_EXPERTS, tm), 0)
    oh1 = eid == e1
    oh2 = eid == e2
    both = jnp.where(oh1 | oh2, 1.0, 0.0)
    before = jnp.dot(both.astype(BF16), tri_ref[...], preferred_element_type=F32) + carry_ref[:, 0:1]
    r1 = jnp.sum(jnp.where(oh1, before, 0.0), axis=0, keepdims=True)
    r2 = jnp.sum(jnp.where(oh2, before, 0.0), axis=0, keepdims=True)
    new_carry = carry_ref[...] + jnp.sum(both, axis=1, keepdims=True)
    carry_ref[...] = new_carry
    cnt_ref[...] = new_carry.astype(jnp.int32)
    zi = jnp.zeros((4, tm), jnp.int32)
    oi_ref[...] = jnp.concatenate([e1, e2, r1.astype(jnp.int32), r2.astype(jnp.int32), zi], axis=0)
    gt = jnp.concatenate([g1, g2, jnp.zeros((LANE - 2, tm), F32)], axis=0)
    og_ref[...] = gt.T


def _route(logits, router_b):
    n = logits.shape[0]
    tm = ROW_TILE
    tri = jnp.asarray(np.triu(np.ones((tm, tm), np.float32), 1)).astype(BF16)
    oi, og, cnt = pl.pallas_call(
        _route_kernel,
        grid=(n // tm,),
        in_specs=[pl.BlockSpec((tm, LANE), lambda i: (i, 0)),
                  pl.BlockSpec((N_EXPERTS, 1), lambda i: (0, 0)),
                  pl.BlockSpec((tm, tm), lambda i: (0, 0))],
        out_specs=[pl.BlockSpec((8, tm), lambda i: (0, i)),
                   pl.BlockSpec((tm, LANE), lambda i: (i, 0)),
                   pl.BlockSpec((N_EXPERTS, LANE), lambda i: (0, 0))],
        out_shape=[jax.ShapeDtypeStruct((8, n), jnp.int32),
                   jax.ShapeDtypeStruct((n, LANE), F32),
                   jax.ShapeDtypeStruct((N_EXPERTS, LANE), jnp.int32)],
        scratch_shapes=[pltpu.VMEM((N_EXPERTS, LANE), F32)],
        compiler_params=_cparams("arbitrary"),
        name="moe_route",
    )(logits, router_b.reshape(N_EXPERTS, 1).astype(F32), tri)
    return oi, og, cnt[:, 0]


W_CONV_ROWS = 256


def _expert_kernel(layer, be_ref, nb_ref, first_ref, nxt_ref, x_ref, wg_hbm, wu_hbm, wd_hbm, o_ref,
                   sg_ref, su_ref, sd_ref, wg_ref, wu_ref, wd_ref, sem):
    i = pl.program_id(0)

    def copies(e):
        return (pltpu.make_async_copy(wg_hbm.at[layer, e], sg_ref, sem.at[0]),
                pltpu.make_async_copy(wu_hbm.at[layer, e], su_ref, sem.at[1]),
                pltpu.make_async_copy(wd_hbm.at[layer, e], sd_ref, sem.at[2]))

    def convert(src, dst):
        def body(r, carry):
            r0 = pl.multiple_of(r * W_CONV_ROWS, W_CONV_ROWS)
            dst[pl.ds(r0, W_CONV_ROWS), :] = src[pl.ds(r0, W_CONV_ROWS), :].astype(BF16)
            return carry
        lax.fori_loop(0, src.shape[0] // W_CONV_ROWS, body, 0)

    active = i < nb_ref[0]

    @pl.when(active & (i == 0))
    def _():
        for cp in copies(be_ref[0]):
            cp.start()

    @pl.when(active & (first_ref[i] == 1))
    def _():
        for cp in copies(be_ref[i]):
            cp.wait()
        convert(sg_ref, wg_ref)
        convert(su_ref, wu_ref)
        convert(sd_ref, wd_ref)

        @pl.when(nxt_ref[i] >= 0)
        def _():
            for cp in copies(nxt_ref[i]):
                cp.start()

    @pl.when(active)
    def _():
        x = x_ref[...]
        hg = jnp.dot(x, wg_ref[...], preferred_element_type=F32)
        hu = jnp.dot(x, wu_ref[...], preferred_element_type=F32)
        h = (_silu(hg) * hu).astype(BF16)
        o_ref[...] = jnp.dot(h, wd_ref[...], preferred_element_type=F32)

    @pl.when(jnp.logical_not(active))
    def _():
        o_ref[...] = jnp.zeros_like(o_ref)


def _expert_blocks(buf, block_expert, n_used, first, nxt, wg, wu, wd, layer):
    rows, d = buf.shape
    n_blocks = rows // MOE_BLOCK
    de = wg.shape[-1]
    grid_spec = pltpu.PrefetchScalarGridSpec(
        num_scalar_prefetch=4,
        grid=(n_blocks,),
        in_specs=[pl.BlockSpec((MOE_BLOCK, d), lambda i, *_: (i, 0)),
                  pl.BlockSpec(memory_space=pl.ANY),
                  pl.BlockSpec(memory_space=pl.ANY),
                  pl.BlockSpec(memory_space=pl.ANY)],
        out_specs=pl.BlockSpec((MOE_BLOCK, d), lambda i, *_: (i, 0)),
        scratch_shapes=[pltpu.VMEM((d, de), F32), pltpu.VMEM((d, de), F32), pltpu.VMEM((de, d), F32),
                        pltpu.VMEM((d, de), BF16), pltpu.VMEM((d, de), BF16), pltpu.VMEM((de, d), BF16),
                        pltpu.SemaphoreType.DMA((3,))],
    )
    return pl.pallas_call(
        functools.partial(_expert_kernel, layer),
        grid_spec=grid_spec,
        out_shape=jax.ShapeDtypeStruct((rows, d), F32),
        compiler_params=_cparams("arbitrary"),
        name="moe_experts",
    )(block_expert, n_used, first, nxt, buf, wg, wu, wd)


def _combine_update_kernel(n_lat_tiles, y0_ref, y1_ref, g_ref, x_ref, p_ref, xo_ref, h_ref):
    g = g_ref[0]
    y = y0_ref[0] * g[:, 0:1] + y1_ref[0] * g[:, 1:2]
    xn, h = _post_update(n_lat_tiles, y, x_ref[0], p_ref[0])
    xo_ref[0] = xn
    h_ref[0] = h.astype(h_ref.dtype)


def _combine_update(y0, y1, gates, x, params, n_lat):
    bsz, t, d = x.shape
    row = pl.BlockSpec((1, ROW_TILE, d), lambda b, i: (b, i, 0))
    return pl.pallas_call(
        functools.partial(_combine_update_kernel, n_lat // ROW_TILE),
        grid=(bsz, t // ROW_TILE),
        in_specs=[row, row, pl.BlockSpec((1, ROW_TILE, LANE), lambda b, i: (b, i, 0)), row,
                  pl.BlockSpec((1, 8, d), lambda b, i: (b, 0, 0))],
        out_specs=[row, row],
        out_shape=[jax.ShapeDtypeStruct((bsz, t, d), F32), jax.ShapeDtypeStruct((bsz, t, d), BF16)],
        compiler_params=_cparams("parallel", "parallel"),
        name="moe_combine_ln",
    )(y0, y1, gates, x, params)


def _moe(tok, logits, router_b, wg, wu, wd, layer):
    bsz, t, d = tok.shape
    n = bsz * t
    tk = n * TOP_K
    oi, gates, counts = _route(logits.reshape(n, LANE), router_b)
    experts, ranks = oi[0:2], oi[2:4]
    padded = (counts + MOE_BLOCK - 1) // MOE_BLOCK * MOE_BLOCK
    pends = jnp.cumsum(padded)
    pstarts = pends - padded
    dest = pstarts[experts] + ranks
    n_blocks = -(-tk // MOE_BLOCK) + N_EXPERTS
    blk_start = jnp.arange(n_blocks, dtype=jnp.int32) * MOE_BLOCK
    block_expert = jnp.minimum(jnp.searchsorted(pends, blk_start, side='right'), N_EXPERTS - 1).astype(jnp.int32)
    n_used = (pends[-1] // MOE_BLOCK).astype(jnp.int32).reshape(1)
    prev_e = jnp.concatenate([jnp.full((1,), -1, jnp.int32), block_expert[:-1]])
    first = (block_expert != prev_e).astype(jnp.int32)
    later = jnp.where((counts[None, :] > 0) & (jnp.arange(N_EXPERTS)[None, :] > jnp.arange(N_EXPERTS)[:, None]),
                      jnp.arange(N_EXPERTS)[None, :], N_EXPERTS)
    nxt_e = jnp.min(later, axis=1)
    nxt_e = jnp.where(nxt_e >= N_EXPERTS, -1, nxt_e).astype(jnp.int32)
    nxt = nxt_e[block_expert]
    tok_id = jnp.broadcast_to(jnp.arange(n, dtype=jnp.int32)[None], (TOP_K, n))
    src = jnp.zeros((n_blocks * MOE_BLOCK,), jnp.int32).at[dest.reshape(-1)].set(tok_id.reshape(-1))
    buf = tok.reshape(n, d)[src]
    out = _expert_blocks(buf, block_expert, n_used, first, nxt, wg, wu, wd, layer)
    y0 = out[dest[0]].reshape(bsz, t, d)
    y1 = out[dest[1]].reshape(bsz, t, d)
    return y0, y1, gates.reshape(bsz, t, LANE)


GLA_CS = 64
EXP_CLAMP = 80.0


def _gla_kernel(reverse, has_prev, nsub, *refs):
    if has_prev:
        (q_ref, k_ref, v_ref, g1_ref, w2_ref, gb_ref, prev_ref, r_ref, ng_ref, o_ref, s_ref) = refs
    else:
        (q_ref, k_ref, v_ref, g1_ref, w2_ref, gb_ref, o_ref, s_ref) = refs

    @pl.when(pl.program_id(1) == 0)
    def _():
        s_ref[...] = jnp.zeros_like(s_ref)

    cs = GLA_CS
    row = lax.broadcasted_iota(jnp.int32, (cs, cs), 0)
    col = lax.broadcasted_iota(jnp.int32, (cs, cs), 1)
    keep = (col >= row) if reverse else (col <= row)
    cum_m = keep.astype(F32)
    z = jnp.dot(g1_ref[0], w2_ref[0], precision=HIGHEST, preferred_element_type=F32) + gb_ref[0]
    log_gate = (jnp.minimum(z, 0.0) - jnp.log(1.0 + jnp.exp(-jnp.abs(z)))) * (1.0 / GLA_TAU)
    q_all = q_ref[0] * (GLA_HEAD_K ** -0.5)
    k_all = k_ref[0]
    v_all = v_ref[0]
    last = 0 if reverse else cs - 1
    for h in range(GLA_HEADS):
        ks = slice(h * GLA_HEAD_K, (h + 1) * GLA_HEAD_K)
        vs = slice(h * GLA_HEAD_V, (h + 1) * GLA_HEAD_V)
        state = s_ref[h]
        outs = [None] * nsub
        for c in (range(nsub - 1, -1, -1) if reverse else range(nsub)):
            rs = slice(c * cs, (c + 1) * cs)
            b = jnp.dot(cum_m, log_gate[rs, ks], precision=HIGHEST, preferred_element_type=F32)
            b_last = b[last:last + 1]
            ref = 0.5 * b_last
            qc, kc, vc = q_all[rs, ks], k_all[rs, ks], v_all[rs, vs].astype(BF16)
            q_in = (qc * jnp.exp(jnp.minimum(b - ref, EXP_CLAMP))).astype(BF16)
            k_in = (kc * jnp.exp(jnp.minimum(ref - b, EXP_CLAMP))).astype(BF16)
            att = lax.dot_general(q_in, k_in, (((1,), (1,)), ((), ())), preferred_element_type=F32)
            att = jnp.where(keep, att, 0.0).astype(BF16)
            o = jnp.dot(att, vc, preferred_element_type=F32)
            o = o + jnp.dot((qc * jnp.exp(b)).astype(BF16), state.astype(BF16), preferred_element_type=F32)
            k_tail = (kc * jnp.exp(b_last - b)).astype(BF16)
            upd = lax.dot_general(k_tail, vc, (((0,), (0,)), ((), ())), preferred_element_type=F32)
            state = jnp.exp(b_last).reshape(GLA_HEAD_K, 1) * state + upd
            outs[c] = o
        s_ref[h] = state
        o_h = jnp.concatenate(outs, axis=0)
        if has_prev:
            o_h = o_h + prev_ref[0][:, vs]
            o_h = o_h * lax.rsqrt(jnp.mean(o_h * o_h, axis=-1, keepdims=True) + EPS) * ng_ref[:, vs]
            o_h = o_h * _silu(r_ref[0][:, vs])
        o_ref[0, :, vs] = o_h.astype(o_ref.dtype)


def _gla_direction(u_all, w2pad, gbias, n_lat, reverse, prev=None, norm_g=None):
    bsz, t, _ = u_all.shape
    n = t // ROW_TILE
    nl = n_lat // ROW_TILE
    nsub = ROW_TILE // GLA_CS
    if reverse:
        blk = lambda s: n - 1 - s
    else:
        blk = lambda s: (s + nl) % n
    q0, k0, v0, r0, g0 = (AB_SIZES[0] // GLA_DK, (AB_SIZES[0] + GLA_DK) // GLA_DK, (AB_SIZES[0] + 2 * GLA_DK) // GLA_DV,
                          (AB_SIZES[0] + 2 * GLA_DK + GLA_DV) // GLA_DV, (AB_IN - 2 * GLA_RANK) // LANE)
    d = 1 if reverse else 0
    in_specs = [pl.BlockSpec((1, ROW_TILE, GLA_DK), lambda b, s: (b, blk(s), q0)),
                pl.BlockSpec((1, ROW_TILE, GLA_DK), lambda b, s: (b, blk(s), k0)),
                pl.BlockSpec((1, ROW_TILE, GLA_DV), lambda b, s: (b, blk(s), v0)),
                pl.BlockSpec((1, ROW_TILE, LANE), lambda b, s: (b, blk(s), g0)),
                pl.BlockSpec((1, LANE, GLA_DK), lambda b, s: (d, 0, 0)),
                pl.BlockSpec((1, 1, GLA_DK), lambda b, s: (d, 0, 0))]
    args = [u_all, u_all, u_all, u_all, w2pad, gbias]
    has_prev = prev is not None
    if has_prev:
        in_specs += [pl.BlockSpec((1, ROW_TILE, GLA_DV), lambda b, s: (b, blk(s), 0)),
                     pl.BlockSpec((1, ROW_TILE, GLA_DV), lambda b, s: (b, blk(s), r0)),
                     pl.BlockSpec((1, GLA_DV), lambda b, s: (0, 0))]
        args += [prev, u_all, norm_g.reshape(1, GLA_DV)]
    return pl.pallas_call(
        functools.partial(_gla_kernel, reverse, has_prev, nsub),
        grid=(bsz, n),
        in_specs=in_specs,
        out_specs=pl.BlockSpec((1, ROW_TILE, GLA_DV), lambda b, s: (b, blk(s), 0)),
        out_shape=jax.ShapeDtypeStruct((bsz, t, GLA_DV), BF16 if has_prev else F32),
        scratch_shapes=[pltpu.VMEM((GLA_HEADS, GLA_HEAD_K, GLA_HEAD_V), F32)],
        compiler_params=_cparams("parallel", "arbitrary"),
        name="gla_bwd_norm" if reverse else "gla_fwd",
    )(*args)


def _gla(u_all, gate_w2, gate_b, norm_g, n_lat):
    w2pad = jnp.zeros((2, LANE, GLA_DK), F32)
    w2pad = w2pad.at[0, :GLA_RANK].set(gate_w2[0]).at[1, GLA_RANK:2 * GLA_RANK].set(gate_w2[1])
    gbias = gate_b.reshape(2, 1, GLA_DK).astype(F32)
    o_f = _gla_direction(u_all, w2pad, gbias, n_lat, False)
    return _gla_direction(u_all, w2pad, gbias, n_lat, True, prev=o_f, norm_g=norm_g.astype(F32))


CONV_CHUNK = 512


def _shift_rows(v, dh):
    return v if dh == 0 else pltpu.roll(v, (-dh) % v.shape[0], 0)


def _ssd_conv_kernel(n_lat, x_ref, w_ref, b_ref, o_ref, pad_ref):
    t = x_ref.shape[1]
    n_ctx = t - n_lat
    pad_ref[0:GRID_W, :] = jnp.zeros((GRID_W, LANE), F32)
    pad_ref[GRID_W + n_lat:2 * GRID_W + n_lat, :] = jnp.zeros((GRID_W, LANE), F32)
    pad_ref[GRID_W:GRID_W + n_lat, :] = x_ref[0, 0:n_lat, :]
    w = w_ref[...]
    bias = b_ref[...]
    col = lax.broadcasted_iota(jnp.int32, (CONV_CHUNK, LANE), 0) % GRID_W
    masks = {-1: col >= 1, 0: None, 1: col <= GRID_W - 2}

    def body(ci, carry):
        t0 = pl.multiple_of(ci * CONV_CHUNK, CONV_CHUNK)
        acc = jnp.zeros((CONV_CHUNK, LANE), F32) + bias
        for i in range(3):
            slab = pad_ref[pl.ds(t0 + i * GRID_W, CONV_CHUNK), :]
            for j in range(3):
                sh = _shift_rows(slab, j - 1)
                if masks[j - 1] is not None:
                    sh = jnp.where(masks[j - 1], sh, 0.0)
                acc = acc + sh * w[i * 3 + j:i * 3 + j + 1]
        o_ref[0, 0, pl.ds(t0, CONV_CHUNK), :] = _silu(acc)
        return carry

    lax.fori_loop(0, n_lat // CONV_CHUNK, body, 0)
    xc = x_ref[0, n_lat:t, :]
    pos = lax.broadcasted_iota(jnp.int32, (n_ctx, LANE), 0)
    acc = xc * w[4:5] + bias
    acc = acc + jnp.where(pos >= 1, _shift_rows(xc, -1), 0.0) * w[3:4]
    acc = acc + jnp.where(pos <= n_ctx - 2, _shift_rows(xc, 1), 0.0) * w[5:6]
    o_ref[0, 0, n_lat:t, :] = _silu(acc)


def _ssd_conv(u2, conv_w, conv_b, n_lat):
    bsz, t, _ = u2.shape
    n_tiles = SSD_CONV_DIM // LANE
    c0 = SSD_D_INNER // LANE
    return pl.pallas_call(
        functools.partial(_ssd_conv_kernel, n_lat),
        grid=(bsz, n_tiles),
        in_specs=[pl.BlockSpec((1, t, LANE), lambda b, j: (b, 0, c0 + j)),
                  pl.BlockSpec((9, LANE), lambda b, j: (0, j)),
                  pl.BlockSpec((1, LANE), lambda b, j: (0, j))],
        out_specs=pl.BlockSpec((1, 1, t, LANE), lambda b, j: (b, j, 0, 0)),
        out_shape=jax.ShapeDtypeStruct((bsz, n_tiles, t, LANE), F32),
        scratch_shapes=[pltpu.VMEM((n_lat + 2 * GRID_W, LANE), F32)],
        compiler_params=_cparams("parallel", "parallel"),
        name="ssd_conv",
    )(u2, conv_w.reshape(9, SSD_CONV_DIM), conv_b.reshape(1, SSD_CONV_DIM))


SSD_CS = 128


def _softplus(x):
    return jnp.maximum(x, 0.0) + jnp.log(1.0 + jnp.exp(-jnp.abs(x)))


def _ssd_kernel(reverse, has_prev, nsub, *refs):
    if has_prev:
        (x_ref, b_ref, c_ref, dt_ref, p_ref, sel_ref, prev_ref, z_ref, ng_ref, o_ref, s_ref) = refs
    else:
        (x_ref, b_ref, c_ref, dt_ref, p_ref, sel_ref, o_ref, s_ref) = refs
    g = pl.program_id(2)

    @pl.when(pl.program_id(1) == 0)
    def _():
        s_ref[g] = jnp.zeros(s_ref.shape[1:], F32)

    cs = SSD_CS
    hw = SSD_HEADDIM
    row = lax.broadcasted_iota(jnp.int32, (cs, cs), 0)
    col = lax.broadcasted_iota(jnp.int32, (cs, cs), 1)
    keep = (col >= row) if reverse else (col <= row)
    cum_m = keep.astype(F32)
    last = 0 if reverse else cs - 1
    sel = sel_ref[0]
    pg = jnp.dot(p_ref[...], sel, precision=HIGHEST, preferred_element_type=F32)
    dtv = _softplus(jnp.dot(dt_ref[0], sel, precision=HIGHEST, preferred_element_type=F32) + pg[0:1])
    a = dtv * pg[1:2]
    for c in (range(nsub - 1, -1, -1) if reverse else range(nsub)):
        rs = slice(c * cs, (c + 1) * cs)
        acs = jnp.dot(cum_m, a[rs], precision=HIGHEST, preferred_element_type=F32)
        acs_t = acs.T
        dt_t = dtv[rs].T
        a_last = acs[last:last + 1]
        tail = jnp.exp(a_last - acs) * dtv[rs]
        e_acs = jnp.exp(acs)
        e_last = jnp.exp(a_last)
        bg = b_ref[0, 0, rs, :].astype(BF16)
        cg = c_ref[0, 0, rs, :].astype(BF16)
        cb = lax.dot_general(cg, bg, (((1,), (1,)), ((), ())), preferred_element_type=F32)
        ys = []
        for r in range(SSD_HPG):
            xh = x_ref[0, r // 2, rs, (r % 2) * hw:(r % 2 + 1) * hw]
            state = s_ref[g, r]
            decay = jnp.where(keep, jnp.exp(jnp.minimum(acs[:, r:r + 1] - acs_t[r:r + 1, :], 0.0)), 0.0)
            m = (cb * decay * dt_t[r:r + 1, :]).astype(BF16)
            y = jnp.dot(m, xh.astype(BF16), preferred_element_type=F32)
            y = y + e_acs[:, r:r + 1] * lax.dot_general(cg, state.astype(BF16), (((1,), (1,)), ((), ())),
                                                        preferred_element_type=F32)
            y = y + pg[2:3, r:r + 1] * xh
            xw = (xh * tail[:, r:r + 1]).astype(BF16)
            upd = lax.dot_general(xw, bg, (((0,), (0,)), ((), ())), preferred_element_type=F32)
            s_ref[g, r] = e_last[:, r:r + 1] * state + upd
            ys.append(y)
        if has_prev:
            yt = [jnp.concatenate(ys[2 * k:2 * k + 2], axis=1) + prev_ref[0, k, rs, :] for k in range(SSD_HPG // 2)]
            z = z_ref[0, rs, :]
            yt = [yt[k] * _silu(z[:, k * LANE:(k + 1) * LANE]) for k in range(SSD_HPG // 2)]
            ss = sum(jnp.sum(v * v, axis=-1, keepdims=True) for v in yt)
            inv = lax.rsqrt(ss * (1.0 / (SSD_HPG * hw)) + EPS)
            ng = ng_ref[...]
            for k in range(SSD_HPG // 2):
                o_ref[0, rs, k * LANE:(k + 1) * LANE] = (yt[k] * inv * ng[:, k * LANE:(k + 1) * LANE]).astype(o_ref.dtype)
        else:
            for k in range(SSD_HPG // 2):
                o_ref[0, k, rs, :] = jnp.concatenate(ys[2 * k:2 * k + 2], axis=1)


def _ssd_direction(xbc_t, u2, pvec, sel, n_lat, reverse, prev=None, norm_g=None):
    bsz, _, t, _ = xbc_t.shape
    n = t // ROW_TILE
    nl = n_lat // ROW_TILE
    nsub = ROW_TILE // SSD_CS
    tpg = SSD_HPG * SSD_HEADDIM // LANE
    gw = SSD_HPG * SSD_HEADDIM
    if reverse:
        blk = lambda s: n - 1 - s
    else:
        blk = lambda s: (s + nl) % n
    b0 = SSD_D_INNER // LANE
    c0 = b0 + SSD_GROUPS
    d = 1 if reverse else 0
    in_specs = [pl.BlockSpec((1, tpg, ROW_TILE, LANE), lambda b, s, g: (b, g, blk(s), 0)),
                pl.BlockSpec((1, 1, ROW_TILE, LANE), lambda b, s, g: (b, b0 + g, blk(s), 0)),
                pl.BlockSpec((1, 1, ROW_TILE, LANE), lambda b, s, g: (b, c0 + g, blk(s), 0)),
                pl.BlockSpec((1, ROW_TILE, LANE), lambda b, s, g: (b, blk(s), (SSD_IN - 2 * SSD_HEADS) // LANE)),
                pl.BlockSpec((8, LANE), lambda b, s, g: (0, 0)),
                pl.BlockSpec((1, LANE, LANE), lambda b, s, g: (d * SSD_GROUPS + g, 0, 0))]
    args = [xbc_t, xbc_t, xbc_t, u2, pvec, sel]
    has_prev = prev is not None
    if has_prev:
        in_specs += [pl.BlockSpec((1, tpg, ROW_TILE, LANE), lambda b, s, g: (b, g, blk(s), 0)),
                     pl.BlockSpec((1, ROW_TILE, gw), lambda b, s, g: (b, blk(s), g)),
                     pl.BlockSpec((1, gw), lambda b, s, g: (0, g))]
        args += [prev, u2, norm_g.reshape(1, SSD_D_INNER)]
        out_spec = pl.BlockSpec((1, ROW_TILE, gw), lambda b, s, g: (b, blk(s), g))
        out_shape = jax.ShapeDtypeStruct((bsz, t, SSD_D_INNER), BF16)
    else:
        out_spec = pl.BlockSpec((1, tpg, ROW_TILE, LANE), lambda b, s, g: (b, g, blk(s), 0))
        out_shape = jax.ShapeDtypeStruct((bsz, SSD_D_INNER // LANE, t, LANE), F32)
    return pl.pallas_call(
        functools.partial(_ssd_kernel, reverse, has_prev, nsub),
        grid=(bsz, n, SSD_GROUPS),
        in_specs=in_specs,
        out_specs=out_spec,
        out_shape=out_shape,
        scratch_shapes=[pltpu.VMEM((SSD_GROUPS, SSD_HPG, SSD_HEADDIM, SSD_STATE), F32)],
        compiler_params=_cparams("parallel", "arbitrary", "arbitrary"),
        name="ssd_bwd_norm" if reverse else "ssd_fwd",
    )(*args)


def _ssd(xbc_t, u2, dt_bias, a_log, d_skip, norm_g, n_lat):
    pvec = jnp.zeros((8, LANE), F32)
    pvec = pvec.at[0].set(dt_bias.reshape(-1)).at[1].set(-jnp.exp(a_log.astype(F32)).reshape(-1))
    pvec = pvec.at[2].set(d_skip.reshape(-1))
    lane = np.arange(LANE)
    sel_np = np.zeros((2 * SSD_GROUPS, LANE, LANE), np.float32)
    for dd in range(2):
        for gg in range(SSD_GROUPS):
            for r in range(SSD_HPG):
                sel_np[dd * SSD_GROUPS + gg, dd * SSD_HEADS + gg * SSD_HPG + r, r] = 1.0
    sel = jnp.asarray(sel_np)
    y_f = _ssd_direction(xbc_t, u2, pvec, sel, n_lat, False)
    return _ssd_direction(xbc_t, u2, pvec, sel, n_lat, True, prev=y_f, norm_g=norm_g.astype(F32))


HY_PAD = 8


def _hy_conv_kernel(n_lat, x_ref, w_ref, b_ref, o_ref, pad_ref):
    t = x_ref.shape[1]
    w = w_ref[...]
    bias = b_ref[...]
    ch = CONV_CHUNK
    pad_ref[0:HY_PAD, :] = jnp.zeros((HY_PAD, LANE), F32)
    pad_ref[HY_PAD + n_lat:2 * HY_PAD + n_lat, :] = jnp.zeros((HY_PAD, LANE), F32)
    pad_ref[HY_PAD:HY_PAD + n_lat, :] = x_ref[0, 0:n_lat, :]
    rowi = lax.broadcasted_iota(jnp.int32, (ch, LANE), 0)

    def body(ci, carry):
        t0 = pl.multiple_of(ci * ch, ch)
        cur = pad_ref[pl.ds(t0 + HY_PAD, ch), :]
        before = pad_ref[pl.ds(t0, HY_PAD), :][HY_PAD - 1:HY_PAD]
        after = pad_ref[pl.ds(t0 + HY_PAD + ch, HY_PAD), :][0:1]
        down = jnp.where(rowi == 0, before, _shift_rows(cur, -1))
        up = jnp.where(rowi == ch - 1, after, _shift_rows(cur, 1))
        o_ref[0, 0, pl.ds(t0, ch), :] = down * w[0:1] + cur * w[1:2] + up * w[2:3] + bias
        return carry

    lax.fori_loop(0, n_lat // ch, body, 0)
    n_ctx = t - n_lat
    xc = x_ref[0, n_lat:t, :]
    pos = lax.broadcasted_iota(jnp.int32, (n_ctx, LANE), 0)
    acc = xc * w[1:2] + bias
    acc = acc + jnp.where(pos >= 1, _shift_rows(xc, -1), 0.0) * w[0:1]
    acc = acc + jnp.where(pos <= n_ctx - 2, _shift_rows(xc, 1), 0.0) * w[2:3]
    o_ref[0, 0, n_lat:t, :] = acc


def _hy_conv(u_all, conv_w, conv_b, n_lat):
    bsz, t, _ = u_all.shape
    nch = (HY_ORDER + 1) * HY_WIDTH
    tpp = HY_WIDTH // LANE
    return pl.pallas_call(
        functools.partial(_hy_conv_kernel, n_lat),
        grid=(bsz, nch // LANE),
        in_specs=[pl.BlockSpec((1, t, LANE), lambda b, j: (b, 0, j)),
                  pl.BlockSpec((3, LANE), lambda b, j: (0, j)),
                  pl.BlockSpec((1, LANE), lambda b, j: (0, j))],
        out_specs=pl.BlockSpec((1, 1, t, LANE), lambda b, j: (j // tpp, b, 0, j % tpp)),
        out_shape=jax.ShapeDtypeStruct((HY_ORDER + 1, bsz, t, HY_WIDTH), F32),
        scratch_shapes=[pltpu.VMEM((n_lat + 2 * HY_PAD, LANE), F32)],
        compiler_params=_cparams("parallel", "parallel"),
        name="hy_short_conv",
    )(u_all, conv_w, conv_b.reshape(1, nch))


FILT_ROWS = 256


def _hy_filter_kernel(seq_len, bands_ref, w1_ref, b1_ref, fr1_ref, w2_ref, b2_ref, fr2_ref, w3_ref, dl_ref, o_ref):
    i = pl.program_id(1)
    n = (i * FILT_ROWS + lax.broadcasted_iota(jnp.int32, (FILT_ROWS, 1), 0))
    lag = jnp.where(n < seq_len, n, 2 * seq_len - n)
    pos = lag.astype(F32)
    tt = pos / max(seq_len - 1, 1)
    ang = 2.0 * math.pi * bands_ref[...] * pos / seq_len
    w1 = w1_ref[...]
    nb = HY_BANDS
    pre = tt * w1[0:1] + jnp.dot(jnp.cos(ang), w1[1:1 + nb], precision=HIGHEST, preferred_element_type=F32)
    pre = pre - jnp.dot(jnp.sin(ang), w1[1 + nb:1 + 2 * nb], precision=HIGHEST, preferred_element_type=F32)
    hid = jnp.sin(fr1_ref[...] * (pre + b1_ref[...]))
    hid = jnp.sin(fr2_ref[...] * (jnp.dot(hid, w2_ref[...], precision=HIGHEST, preferred_element_type=F32) + b2_ref[...]))
    h = jnp.dot(hid, w3_ref[...], precision=HIGHEST, preferred_element_type=F32)
    h = h * jnp.exp(-tt * dl_ref[...])
    o_ref[0] = jnp.where(n == seq_len, 0.0, h)


def _hy_filter(seq_len, f_w1, f_b1, f_fr1, f_w2, f_b2, f_fr2, f_w3):
    fh = f_w1.shape[1]
    emb = f_w1.shape[0]
    half = seq_len // FILT_ROWS
    bands = jnp.asarray(np.linspace(1e-4, HY_BANDS - 1, HY_BANDS, dtype=np.float32)).reshape(1, HY_BANDS)
    deltas = jnp.asarray(np.abs(np.linspace(HY_MIN_DECAY, HY_MAX_DECAY, HY_WIDTH, dtype=np.float32))).reshape(1, HY_WIDTH)
    vec = lambda v: v.reshape(1, fh).astype(F32)
    full = lambda shape: pl.BlockSpec(shape, lambda o, i: (0,) * len(shape))
    return pl.pallas_call(
        functools.partial(_hy_filter_kernel, seq_len),
        grid=(HY_ORDER, 2 * half),
        in_specs=[full((1, HY_BANDS)), full((emb, fh)), full((1, fh)), full((1, fh)), full((fh, fh)), full((1, fh)),
                  full((1, fh)),
                  pl.BlockSpec((fh, HY_WIDTH), lambda o, i: (0, 2 * o + i // half)),
                  full((1, HY_WIDTH))],
        out_specs=pl.BlockSpec((1, FILT_ROWS, HY_WIDTH), lambda o, i: (o, i, 0)),
        out_shape=jax.ShapeDtypeStruct((HY_ORDER, 2 * seq_len, HY_WIDTH), F32),
        compiler_params=_cparams("parallel", "parallel"),
        name="hy_filter",
    )(bands, f_w1.astype(F32), vec(f_b1), vec(f_fr1), f_w2.astype(F32), vec(f_b2), vec(f_fr2), f_w3.astype(F32), deltas)


DFT_N2 = 256
DFT_S = 8


def _dft_tables(n1):
    n = n1 * DFT_N2
    k1h = n1 // 2 + 1
    k1p = -(-k1h // 8) * 8
    k1 = np.arange(k1p)[:, None].astype(np.float64)
    valid = (np.arange(k1p) < k1h)[:, None]
    th1 = 2.0 * np.pi * k1 * np.arange(n1)[None, :] / n1
    f1 = np.concatenate([np.where(valid, np.cos(th1), 0.0), np.where(valid, -np.sin(th1), 0.0)], axis=0)
    tw = 2.0 * np.pi * k1 * np.arange(DFT_N2)[None, :] / n
    tw_re = np.repeat(np.where(valid, np.cos(tw), 0.0)[:, :, None], LANE, axis=2)
    tw_im = np.repeat(np.where(valid, -np.sin(tw), 0.0)[:, :, None], LANE, axis=2)
    ph = 2.0 * np.pi * np.outer(np.arange(DFT_N2), np.arange(DFT_N2)) / DFT_N2
    f2re, f2im = np.cos(ph), -np.sin(ph)
    w_fwd = np.block([[f2re, -f2im], [f2im, f2re]])
    w_inv = np.block([[f2re, f2im], [-f2im, f2re]])
    wgt = np.where((np.arange(k1p) == 0) | (np.arange(k1p) == n1 // 2), 1.0, 2.0) * (np.arange(k1p) < k1h)
    th_i = 2.0 * np.pi * np.arange(n1 // 2)[:, None] * np.arange(k1p)[None, :] / n1
    g = np.concatenate([wgt * np.cos(th_i), -wgt * np.sin(th_i)], axis=1) / n
    f = lambda a: jnp.asarray(a.astype(np.float32))
    return dict(k1h=k1h, k1p=k1p, f1=f(f1), tw_re=f(tw_re), tw_im=f(tw_im), w_fwd=f(w_fwd), w_inv=f(w_inv), g=f(g))


def _dft_dot(a, b):
    return jnp.dot(a, b, precision=HIGHEST, preferred_element_type=F32)


def _dft_first_kernel(k1p, x_ref, f1_ref, twr_ref, twi_ref, ore_ref, oim_ref):
    reps = x_ref.shape[3] // LANE
    f1 = f1_ref[...]
    for s in range(DFT_S):
        a = _dft_dot(f1, x_ref[0, :, s, :])
        tr = jnp.concatenate([twr_ref[:, s, :]] * reps, axis=1)
        ti = jnp.concatenate([twi_ref[:, s, :]] * reps, axis=1)
        are, aim = a[:k1p], a[k1p:]
        ore_ref[0, :, s, :] = are * tr - aim * ti
        oim_ref[0, :, s, :] = are * ti + aim * tr


def _dft_first(x4, lead0, nlead, f1, tab):
    rows = f1.shape[1]
    cw = x4.shape[3]
    k1p = tab['k1p']
    out = jax.ShapeDtypeStruct((nlead, k1p, DFT_N2, cw), F32)
    ospec = pl.BlockSpec((1, k1p, DFT_S, cw), lambda b, j: (b, 0, j, 0))
    tspec = pl.BlockSpec((k1p, DFT_S, LANE), lambda b, j: (0, j, 0))
    return pl.pallas_call(
        functools.partial(_dft_first_kernel, k1p),
        grid=(nlead, DFT_N2 // DFT_S),
        in_specs=[pl.BlockSpec((1, rows, DFT_S, cw), lambda b, j: (lead0 + b, 0, j, 0)),
                  pl.BlockSpec((2 * k1p, rows), lambda b, j: (0, 0)), tspec, tspec],
        out_specs=[ospec, ospec],
        out_shape=[out, out],
        compiler_params=_cparams("parallel", "parallel"),
        name="hy_dft_first",
    )(x4, f1, tab['tw_re'], tab['tw_im'])


DFT_TC = 512


def _dft_mid_kernel(k1h, fused, *refs):
    if fused:
        are_ref, aim_ref, hre_ref, him_ref, wf_ref, wi_ref, twr_ref, twi_ref, ore_ref, oim_ref = refs
    else:
        are_ref, aim_ref, wf_ref, ore_ref, oim_ref = refs
    n2 = DFT_N2

    @pl.when(pl.program_id(0) < k1h)
    def _():
        a = jnp.concatenate([are_ref[0, 0], aim_ref[0, 0]], axis=0)
        x = _dft_dot(wf_ref[...], a)
        xre, xim = x[:n2], x[n2:]
        if not fused:
            ore_ref[0, 0] = xre
            oim_ref[0, 0] = xim
        else:
            hre, him = hre_ref[0, 0], him_ref[0, 0]
            y = jnp.concatenate([xre * hre - xim * him, xre * him + xim * hre], axis=0)
            bb = _dft_dot(wi_ref[...], y)
            bre, bim = bb[:n2], bb[n2:]
            twr, twi = twr_ref[0], twi_ref[0]
            for c in range(bre.shape[1] // LANE):
                sl = slice(c * LANE, (c + 1) * LANE)
                ore_ref[0, 0, :, sl] = bre[:, sl] * twr + bim[:, sl] * twi
                oim_ref[0, 0, :, sl] = bim[:, sl] * twr - bre[:, sl] * twi

    @pl.when(pl.program_id(0) >= k1h)
    def _():
        ore_ref[...] = jnp.zeros_like(ore_ref)
        oim_ref[...] = jnp.zeros_like(oim_ref)


def _dft_mid(are, aim, tab, spec=None, order=0):
    bsz, k1p, _, cw = are.shape
    tc = min(DFT_TC, cw)
    blk = pl.BlockSpec((1, 1, DFT_N2, tc), lambda k, c, b: (b, k, 0, c))
    wspec = pl.BlockSpec((2 * DFT_N2, 2 * DFT_N2), lambda k, c, b: (0, 0))
    fused = spec is not None
    if fused:
        hspec = pl.BlockSpec((1, 1, DFT_N2, tc), lambda k, c, b: (order, k, 0, c))
        tspec = pl.BlockSpec((1, DFT_N2, LANE), lambda k, c, b: (k, 0, 0))
        in_specs = [blk, blk, hspec, hspec, wspec, wspec, tspec, tspec]
        args = [are, aim, spec[0], spec[1], tab['w_fwd'], tab['w_inv'], tab['tw_re'], tab['tw_im']]
    else:
        in_specs = [blk, blk, wspec]
        args = [are, aim, tab['w_fwd']]
    out = jax.ShapeDtypeStruct((bsz, k1p, DFT_N2, cw), F32)
    ore, oim = pl.pallas_call(
        functools.partial(_dft_mid_kernel, tab['k1h'], fused),
        grid=(k1p, cw // tc, bsz),
        in_specs=in_specs,
        out_specs=[blk, blk],
        out_shape=[out, out],
        compiler_params=_cparams("parallel", "parallel", "arbitrary"),
        name="hy_dft_mid_conv" if fused else "hy_dft_mid_filter",
    )(*args)
    return ore, oim


def _dft_last_kernel(bre_ref, bim_ref, g_ref, u_ref, x_ref, d_ref, o_ref):
    g = g_ref[...]
    d = d_ref[...]
    rows = g.shape[0]
    for s in range(DFT_S):
        bb = jnp.concatenate([bre_ref[0, :, s, :], bim_ref[0, :, s, :]], axis=0)
        y = _dft_dot(g, bb)
        u = u_ref[0, :, s, :]
        o_ref[0, 0:rows, s, :] = x_ref[0, :, s, :] * (y + u * d)
    if o_ref.shape[1] > rows:
        o_ref[0, rows:, :, :] = jnp.zeros((o_ref.shape[1] - rows,) + o_ref.shape[2:], F32)


def _dft_last(bre, bim, tab, u4, u_lead0, x4, x_lead0, dvec, out_rows):
    bsz, k1p, _, cw = bre.shape
    rows = tab['g'].shape[0]
    bspec = pl.BlockSpec((1, k1p, DFT_S, cw), lambda b, j: (b, 0, j, 0))
    return pl.pallas_call(
        _dft_last_kernel,
        grid=(bsz, DFT_N2 // DFT_S),
        in_specs=[bspec, bspec, pl.BlockSpec((rows, 2 * k1p), lambda b, j: (0, 0)),
                  pl.BlockSpec((1, rows, DFT_S, cw), lambda b, j: (u_lead0 + b, 0, j, 0)),
                  pl.BlockSpec((1, rows, DFT_S, cw), lambda b, j: (x_lead0 + b, 0, j, 0)),
                  pl.BlockSpec((1, cw), lambda b, j: (0, 0))],
        out_specs=pl.BlockSpec((1, out_rows, DFT_S, cw), lambda b, j: (b, 0, j, 0)),
        out_shape=jax.ShapeDtypeStruct((bsz, out_rows, DFT_N2, cw), F32),
        compiler_params=_cparams("parallel", "parallel"),
        name="hy_dft_last",
    )(bre, bim, tab['g'], u4, x4, dvec.reshape(1, cw).astype(F32))


def _hy_ctx_kernel(p_ref, f_ref, d_ref, y_ref, o_ref):
    del y_ref
    n = p_ref.shape[2]
    pos = lax.broadcasted_iota(jnp.int32, (n, LANE), 0)
    y = p_ref[0, 0]
    for o in range(HY_ORDER):
        def body(m, acc):
            hc = f_ref[o, pl.ds(m, 1), :]
            ha = f_ref[o, pl.ds(2 * n - m, 1), :]
            acc = acc + jnp.where(pos >= m, pltpu.roll(y, m, 0), 0.0) * hc
            return acc + jnp.where(pos < n - m, pltpu.roll(y, n - m, 0), 0.0) * ha
        conv = lax.fori_loop(1, n, body, y * f_ref[o, 0:1, :])
        y = p_ref[o + 1, 0] * (conv + y * d_ref[o:o + 1, :])
    o_ref[0] = y


def _hy_ctx(parts, filt_c, long_bias, y_all, n_lat):
    nparts, bsz, t, cw = parts.shape
    n_ctx = t - n_lat
    return pl.pallas_call(
        _hy_ctx_kernel,
        grid=(bsz, cw // LANE),
        in_specs=[pl.BlockSpec((nparts, 1, n_ctx, LANE), lambda b, c: (0, b, n_lat // n_ctx, c)),
                  pl.BlockSpec((HY_ORDER, 2 * n_ctx, LANE), lambda b, c: (0, 0, c)),
                  pl.BlockSpec((HY_ORDER, LANE), lambda b, c: (0, c)),
                  pl.BlockSpec(memory_space=pl.ANY)],
        out_specs=pl.BlockSpec((1, n_ctx, LANE), lambda b, c: (b, n_lat // n_ctx, c)),
        out_shape=jax.ShapeDtypeStruct(y_all.shape, F32),
        input_output_aliases={3: 0},
        compiler_params=_cparams("parallel", "parallel"),
        name="hy_ctx_conv",
    )(parts, filt_c, long_bias.astype(F32), y_all)


def _hyena(u_all, hy_p, n_lat):
    conv_w, conv_b, f_w1, f_b1, f_fr1, f_w2, f_b2, f_fr2, f_w3, long_bias = hy_p
    bsz, t, _ = u_all.shape
    n_ctx = t - n_lat
    assert n_ctx == DFT_N2 and n_lat % (2 * DFT_N2) == 0
    n1 = 2 * n_lat // DFT_N2
    tab = _dft_tables(n1)
    cw = HY_WIDTH
    parts = _hy_conv(u_all, conv_w, conv_b, n_lat)
    filt_l = _hy_filter(n_lat, f_w1, f_b1, f_fr1, f_w2, f_b2, f_fr2, f_w3)
    filt_c = _hy_filter(n_ctx, f_w1, f_b1, f_fr1, f_w2, f_b2, f_fr2, f_w3)
    spec = _dft_mid(*_dft_first(filt_l.reshape(HY_ORDER, n1, DFT_N2, cw), 0, HY_ORDER, tab['f1'], tab), tab)
    f1_half = tab['f1'][:, :n1 // 2]
    parts4 = parts.reshape((HY_ORDER + 1) * bsz, t // DFT_N2, DFT_N2, cw)
    y4, y_lead0 = parts4, 0
    for o in range(HY_ORDER):
        are, aim = _dft_first(y4, y_lead0, bsz, f1_half, tab)
        bre, bim = _dft_mid(are, aim, tab, spec=spec, order=o)
        last = o + 1 == HY_ORDER
        y4 = _dft_last(bre, bim, tab, y4, y_lead0, parts4, (o + 1) * bsz, long_bias[o],
                       t // DFT_N2 if last else n1 // 2)
        y_lead0 = 0
    return _hy_ctx(parts, filt_c, long_bias, y4.reshape(bsz, t, cw), n_lat)


def _row_tile(m):
    for tm in (1280, 1024, 512, 256):
        if m % tm == 0:
            return tm
    raise ValueError(m)


def _col_tile(n):
    for k in range(n // LANE, 0, -1):
        if n % (k * LANE) == 0 and k * LANE <= 1280:
            return k * LANE
    raise ValueError(n)


def _in_proj(h, w):
    bsz, t, d = h.shape
    n = w.shape[1]
    return _matmul(h.reshape(bsz * t, d), w, _row_tile(bsz * t), _col_tile(n)).reshape(bsz, t, n)


def kernel(x, c, ctx, c_ctx, router_w, router_b, mod_w, mod_b, ln_g, ln_b, exp_w_gate, exp_w_up, exp_w_down, ab_w_in, ab_w_out, hy_conv_w, hy_conv_b, hy_f_w1, hy_f_b1, hy_f_fr1, hy_f_w2, hy_f_b2, hy_f_fr2, hy_f_w3, hy_long_bias, gla_gate_w2, gla_gate_b, gla_norm_g, ssd_w_in, ssd_conv_w, ssd_conv_b, ssd_dt_bias, ssd_a_log, ssd_d, ssd_norm_g, ssd_w_out):
    bsz, n_lat, d = x.shape
    n_ctx = ctx.shape[1]
    assert bsz < 8 and n_lat % ROW_TILE == 0 and n_ctx % ROW_TILE == 0 and d == D_MODEL
    xa = jnp.concatenate([x, ctx], axis=1).astype(F32)
    c8 = jnp.zeros((8, d), F32).at[:bsz].set(c).at[bsz].set(c_ctx)
    router_w_pad = jnp.zeros((d, LANE), F32).at[:, :N_EXPERTS].set(router_w)
    mods = [_mod_vectors(c8, mod_w[i].astype(F32), mod_b[i].astype(F32)).reshape(8, 6, d) for i in range(DEPTH)]

    def rows(i, idx_l):
        lat = jnp.stack([mods[i][:bsz, k] for k in idx_l], axis=1)
        cx = jnp.broadcast_to(jnp.stack([mods[i][bsz, k] for k in idx_l], axis=0)[None], lat.shape)
        return jnp.concatenate([lat, cx], axis=1)

    def with_ln(p, g, b):
        extra = jnp.broadcast_to(jnp.stack([g, b], axis=0).astype(F32)[None], (bsz, 2, d))
        return jnp.concatenate([p, extra], axis=1)

    h = _modulate(xa, rows(0, (0, 1)), n_lat)
    for i in range(DEPTH):
        j = i // 2
        if i % 2 == 0:
            w_pad = -(-AB_IN // LANE) * LANE
            w_in = jnp.zeros((d, w_pad), BF16).at[:, :AB_IN].set(ab_w_in[j].astype(BF16))
            u_all = _in_proj(h, w_in)
            hy_p = (hy_conv_w[j], hy_conv_b[j], hy_f_w1[j], hy_f_b1[j], hy_f_fr1[j], hy_f_w2[j], hy_f_b2[j],
                    hy_f_fr2[j], hy_f_w3[j], hy_long_bias[j])
            y_hy = _hyena(u_all, hy_p, n_lat)
            y_gla = _gla(u_all, gla_gate_w2[j], gla_gate_b[j], gla_norm_g[j], n_lat)
            w_out = ab_w_out[j].astype(BF16)
            a_list, w_list = [y_hy, y_gla], [w_out[:HY_WIDTH], w_out[HY_WIDTH:]]
        else:
            u2 = _in_proj(h, ssd_w_in[j].astype(BF16))
            xbc_t = _ssd_conv(u2, ssd_conv_w[j].astype(F32), ssd_conv_b[j].astype(F32), n_lat)
            y_ssd = _ssd(xbc_t, u2, ssd_dt_bias[j].astype(F32), ssd_a_log[j], ssd_d[j].astype(F32), ssd_norm_g[j], n_lat)
            a_list, w_list = [y_ssd], [ssd_w_out[j].astype(BF16)]
        p1 = with_ln(rows(i, (2, 4, 3)), ln_g[i, 0], ln_b[i, 0])
        xa, tok, logits = _proj_update(a_list, w_list, xa, p1, router_w_pad, n_lat)
        y0, y1, gates = _moe(tok, logits, router_b, exp_w_gate.astype(F32), exp_w_up.astype(F32),
                             exp_w_down.astype(F32), i)
        gate2 = rows(i, (5,))
        if i + 1 < DEPTH:
            nxt = rows(i + 1, (1, 0))
        else:
            nxt = jnp.zeros((bsz, 4, d), F32)
        p2 = jnp.concatenate([gate2[:, 0:1], nxt[:, 0:2], gate2[:, 1:2], nxt[:, 2:4]], axis=1)
        xa, h = _combine_update(y0, y1, gates, xa, with_ln(p2, ln_g[i, 1], ln_b[i, 1]), n_lat)
    return xa[:, :n_lat].astype(x.dtype)
```

```python
import functools
import math

import numpy as np
import jax
import jax.numpy as jnp
from jax import lax
from jax.experimental import pallas as pl
from jax.experimental.pallas import tpu as pltpu

F32 = jnp.float32
BF16 = jnp.bfloat16
HIGHEST = lax.Precision.HIGHEST

D_MODEL = 2048
DEPTH = 2
GRID_W = 64
HY_WIDTH = D_MODEL // 2
HY_ORDER = 2
HY_BANDS = 16
HY_MIN_DECAY = math.log(1e-2) / 1.5
HY_MAX_DECAY = math.log(1e-2) / 0.3
GLA_HEADS = 4
GLA_DK = D_MODEL // 4
GLA_DV = D_MODEL // 2
GLA_HEAD_K = GLA_DK // GLA_HEADS
GLA_HEAD_V = GLA_DV // GLA_HEADS
GLA_RANK = 16
GLA_TAU = 16.0
AB_SIZES = ((HY_ORDER + 1) * HY_WIDTH, GLA_DK, GLA_DK, GLA_DV, GLA_DV, GLA_RANK, GLA_RANK)
AB_IN = sum(AB_SIZES)
SSD_D_INNER = 2 * D_MODEL
SSD_HEADDIM = 64
SSD_HEADS = SSD_D_INNER // SSD_HEADDIM
SSD_GROUPS = 8
SSD_HPG = SSD_HEADS // SSD_GROUPS
SSD_STATE = 128
SSD_CONV_DIM = SSD_D_INNER + 2 * SSD_GROUPS * SSD_STATE
SSD_IN = SSD_D_INNER + SSD_CONV_DIM + 2 * SSD_HEADS
N_EXPERTS = 16
N_EXPERT_GROUPS = 4
EXPERTS_PER_GROUP = N_EXPERTS // N_EXPERT_GROUPS
TOP_K = 2
D_EXPERT = D_MODEL // 2
ALPHA = (2 * DEPTH) ** 0.25
EPS = 1e-6

LANE = 128
ROW_TILE = 256
MOE_BLOCK = 256
VMEM_LIMIT = 56 * 1024 * 1024


def _cparams(*sem):
    return pltpu.CompilerParams(dimension_semantics=sem, vmem_limit_bytes=VMEM_LIMIT)


def _silu(x):
    return x * (1.0 / (1.0 + jnp.exp(-x)))


def _mod_kernel(c_ref, w_ref, b_ref, o_ref):
    o_ref[...] = jnp.dot(_silu(c_ref[...]), w_ref[...], precision=HIGHEST, preferred_element_type=F32) + b_ref[...]


def _mod_vectors(c8, w, b):
    d, n = w.shape
    tn = 1536
    return pl.pallas_call(
        _mod_kernel,
        grid=(n // tn,),
        in_specs=[pl.BlockSpec((8, d), lambda j: (0, 0)),
                  pl.BlockSpec((d, tn), lambda j: (0, j)),
                  pl.BlockSpec((1, tn), lambda j: (0, j))],
        out_specs=pl.BlockSpec((8, tn), lambda j: (0, j)),
        out_shape=jax.ShapeDtypeStruct((8, n), F32),
        compiler_params=_cparams("arbitrary"),
        name="adaln_vectors",
    )(c8, w, b.reshape(1, n))


def _modulate_kernel(n_lat_tiles, x_ref, p_ref, o_ref):
    is_ctx = pl.program_id(1) >= n_lat_tiles
    p = p_ref[0]
    shift = jnp.where(is_ctx, p[2:3], p[0:1])
    scale = jnp.where(is_ctx, p[3:4], p[1:2])
    o_ref[0] = (x_ref[0] * (1.0 + scale) + shift).astype(o_ref.dtype)


def _modulate(x, params, n_lat):
    bsz, t, d = x.shape
    return pl.pallas_call(
        functools.partial(_modulate_kernel, n_lat // ROW_TILE),
        grid=(bsz, t // ROW_TILE),
        in_specs=[pl.BlockSpec((1, ROW_TILE, d), lambda b, i: (b, i, 0)),
                  pl.BlockSpec((1, 4, d), lambda b, i: (b, 0, 0))],
        out_specs=pl.BlockSpec((1, ROW_TILE, d), lambda b, i: (b, i, 0)),
        out_shape=jax.ShapeDtypeStruct((bsz, t, d), BF16),
        compiler_params=_cparams("parallel", "parallel"),
        name="modulate",
    )(x, params)


def _matmul_kernel(a_ref, w_ref, o_ref):
    o_ref[...] = jnp.dot(a_ref[...], w_ref[...], preferred_element_type=F32).astype(o_ref.dtype)


def _matmul(a, w, tm, tn, out_dtype=F32):
    m, k = a.shape
    n = w.shape[1]
    return pl.pallas_call(
        _matmul_kernel,
        grid=(m // tm, n // tn),
        in_specs=[pl.BlockSpec((tm, k), lambda i, j: (i, 0)),
                  pl.BlockSpec((k, tn), lambda i, j: (0, j))],
        out_specs=pl.BlockSpec((tm, tn), lambda i, j: (i, j)),
        out_shape=jax.ShapeDtypeStruct((m, n), out_dtype),
        compiler_params=_cparams("parallel", "arbitrary"),
        name="in_proj",
    )(a, w)


def _layer_norm_rows(v, g, b):
    mu = jnp.mean(v, axis=-1, keepdims=True)
    vc = v - mu
    var = jnp.mean(vc * vc, axis=-1, keepdims=True)
    return vc * lax.rsqrt(var + EPS) * g + b


def _param_table(p):
    bsz, _, d = p.shape
    pad = jnp.zeros((bsz, 3, d), F32)
    lat = jnp.concatenate([p[:, 0:3], p[:, 6:8], pad], axis=1)
    cx = jnp.concatenate([p[:, 3:6], p[:, 6:8], pad], axis=1)
    return jnp.stack([lat, cx], axis=1).reshape(2 * bsz, 8, d)


def _post_update(tiles_per_batch, n_lat_tiles, tile, y, x, p_ref):
    b = tile // tiles_per_batch
    is_ctx = (tile - b * tiles_per_batch >= n_lat_tiles).astype(jnp.int32)
    p = p_ref[2 * b + is_ctx]
    xn = _layer_norm_rows(ALPHA * x + p[0:1] * y, p[3:4], p[4:5])
    return xn, xn * (1.0 + p[1:2]) + p[2:3]


PROJ_HALVES = 2


def _proj_update_kernel(tiles_per_batch, n_lat_tiles, n_a, *refs):
    a_refs = refs[:n_a]
    w_refs = refs[n_a:2 * n_a]
    x_ref, p_ref, rw_ref, xo_ref, tok_ref, lg_ref = refs[2 * n_a:]
    for hh in range(PROJ_HALVES):
        rs = slice(hh * ROW_TILE, (hh + 1) * ROW_TILE)
        y = jnp.dot(a_refs[0][rs, :].astype(BF16), w_refs[0][...], preferred_element_type=F32)
        for a_ref, w_ref in zip(a_refs[1:], w_refs[1:]):
            y = y + jnp.dot(a_ref[rs, :].astype(BF16), w_ref[...], preferred_element_type=F32)
        tile = pl.program_id(0) * PROJ_HALVES + hh
        xn, tok = _post_update(tiles_per_batch, n_lat_tiles, tile, y, x_ref[rs, :], p_ref)
        xo_ref[rs, :] = xn
        tok_ref[rs, :] = tok.astype(tok_ref.dtype)
        lg_ref[rs, :] = jnp.dot(tok, rw_ref[...], precision=HIGHEST, preferred_element_type=F32)


def _proj_update(a_list, w_list, x, params, router_w_pad, n_lat):
    bsz, t, d = x.shape
    n = bsz * t
    tm = PROJ_HALVES * ROW_TILE
    assert n % tm == 0
    n_a = len(a_list)
    flat = lambda v: v.reshape(n, v.shape[-1])
    in_specs = [pl.BlockSpec((tm, a.shape[-1]), lambda i: (i, 0)) for a in a_list]
    in_specs += [pl.BlockSpec(w.shape, lambda i: (0, 0)) for w in w_list]
    in_specs += [pl.BlockSpec((tm, d), lambda i: (i, 0)),
                 pl.BlockSpec((2 * bsz, 8, d), lambda i: (0, 0, 0)),
                 pl.BlockSpec((d, LANE), lambda i: (0, 0))]
    xo, tok, lg = pl.pallas_call(
        functools.partial(_proj_update_kernel, t // ROW_TILE, n_lat // ROW_TILE, n_a),
        grid=(n // tm,),
        in_specs=in_specs,
        out_specs=[pl.BlockSpec((tm, d), lambda i: (i, 0)),
                   pl.BlockSpec((tm, d), lambda i: (i, 0)),
                   pl.BlockSpec((tm, LANE), lambda i: (i, 0))],
        out_shape=[jax.ShapeDtypeStruct((n, d), F32),
                   jax.ShapeDtypeStruct((n, d), BF16),
                   jax.ShapeDtypeStruct((n, LANE), F32)],
        compiler_params=_cparams("parallel"),
        name="out_proj_ln",
    )(*[flat(a) for a in a_list], *w_list, flat(x), _param_table(params), router_w_pad)
    return xo.reshape(bsz, t, d), tok.reshape(bsz, t, d), lg.reshape(bsz, t, LANE)


def _route_kernel(lg_ref, rb_ref, tri_ref, oi_ref, og_ref, cnt_ref, carry_ref):
    @pl.when(pl.program_id(0) == 0)
    def _():
        carry_ref[...] = jnp.zeros_like(carry_ref)

    tm = lg_ref.shape[0]
    epg = EXPERTS_PER_GROUP
    lt = lg_ref[...].T[:N_EXPERTS]
    score = 1.0 / (1.0 + jnp.exp(-lt))
    sel = score + rb_ref[...]
    best_g = None
    for q in range(N_EXPERT_GROUPS):
        rows = [sel[q * epg + r:q * epg + r + 1] for r in range(epg)]
        gs = None
        for a in range(epg):
            for b in range(a + 1, epg):
                ps = rows[a] + rows[b]
                gs = ps if gs is None else jnp.maximum(gs, ps)
        if best_g is None:
            best_g, grp = gs, jnp.zeros_like(gs, dtype=jnp.int32)
        else:
            better = gs > best_g
            grp = jnp.where(better, q, grp)
            best_g = jnp.where(better, gs, best_g)
    in_sel, in_score = [], []
    for r in range(epg):
        v = sel[r:r + 1]
        s = score[r:r + 1]
        for q in range(1, N_EXPERT_GROUPS):
            v = jnp.where(grp == q, sel[q * epg + r:q * epg + r + 1], v)
            s = jnp.where(grp == q, score[q * epg + r:q * epg + r + 1], s)
        in_sel.append(v)
        in_score.append(s)
    i1, v1 = jnp.zeros_like(grp), in_sel[0]
    for r in range(1, epg):
        better = in_sel[r] > v1
        i1 = jnp.where(better, r, i1)
        v1 = jnp.where(better, in_sel[r], v1)
    i2, v2 = None, None
    for r in range(epg):
        cand = jnp.where(i1 == r, -jnp.inf, in_sel[r])
        if i2 is None:
            i2, v2 = jnp.zeros_like(grp), cand
        else:
            better = cand > v2
            i2 = jnp.where(better, r, i2)
            v2 = jnp.where(better, cand, v2)
    s1, s2 = in_score[0], in_score[0]
    for r in range(1, epg):
        s1 = jnp.where(i1 == r, in_score[r], s1)
        s2 = jnp.where(i2 == r, in_score[r], s2)
    e1 = grp * epg + i1
    e2 = grp * epg + i2
    tot = s1 + s2
    g1, g2 = s1 / tot, s2 / tot
    eid = lax.broadcasted_iota(jnp.int32, (N_EXPERTS, tm), 0)
    oh1 = eid == e1
    oh2 = eid == e2
    both = jnp.where(oh1 | oh2, 1.0, 0.0)
    before = jnp.dot(both.astype(BF16), tri_ref[...], preferred_element_type=F32) + carry_ref[:, 0:1]
    r1 = jnp.sum(jnp.where(oh1, before, 0.0), axis=0, keepdims=True)
    r2 = jnp.sum(jnp.where(oh2, before, 0.0), axis=0, keepdims=True)
    new_carry = carry_ref[...] + jnp.sum(both, axis=1, keepdims=True)
    carry_ref[...] = new_carry
    cnt_ref[...] = new_carry.astype(jnp.int32)
    zi = jnp.zeros((4, tm), jnp.int32)
    oi_ref[...] = jnp.concatenate([e1, e2, r1.astype(jnp.int32), r2.astype(jnp.int32), zi], axis=0)
    gt = jnp.concatenate([g1, g2, jnp.zeros((LANE - 2, tm), F32)], axis=0)
    og_ref[...] = gt.T


def _route(logits, router_b):
    n = logits.shape[0]
    tm = ROW_TILE
    tri = jnp.asarray(np.triu(np.ones((tm, tm), np.float32), 1)).astype(BF16)
    oi, og, cnt = pl.pallas_call(
        _route_kernel,
        grid=(n // tm,),
        in_specs=[pl.BlockSpec((tm, LANE), lambda i: (i, 0)),
                  pl.BlockSpec((N_EXPERTS, 1), lambda i: (0, 0)),
                  pl.BlockSpec((tm, tm), lambda i: (0, 0))],
        out_specs=[pl.BlockSpec((8, tm), lambda i: (0, i)),
                   pl.BlockSpec((tm, LANE), lambda i: (i, 0)),
                   pl.BlockSpec((N_EXPERTS, LANE), lambda i: (0, 0))],
        out_shape=[jax.ShapeDtypeStruct((8, n), jnp.int32),
                   jax.ShapeDtypeStruct((n, LANE), F32),
                   jax.ShapeDtypeStruct((N_EXPERTS, LANE), jnp.int32)],
        scratch_shapes=[pltpu.VMEM((N_EXPERTS, LANE), F32)],
        compiler_params=_cparams("arbitrary"),
        name="moe_route",
    )(logits, router_b.reshape(N_EXPERTS, 1).astype(F32), tri)
    return oi, og, cnt[:, 0]


W_CONV_ROWS = 256


def _expert_kernel(layer, be_ref, nb_ref, first_ref, nxt_ref, x_ref, wg_hbm, wu_hbm, wd_hbm, o_ref,
                   sg_ref, su_ref, sd_ref, wg_ref, wu_ref, wd_ref, sem):
    i = pl.program_id(0)

    def copies(e):
        return (pltpu.make_async_copy(wg_hbm.at[layer, e], sg_ref, sem.at[0]),
                pltpu.make_async_copy(wu_hbm.at[layer, e], su_ref, sem.at[1]),
                pltpu.make_async_copy(wd_hbm.at[layer, e], sd_ref, sem.at[2]))

    def convert(src, dst):
        def body(r, carry):
            r0 = pl.multiple_of(r * W_CONV_ROWS, W_CONV_ROWS)
            dst[pl.ds(r0, W_CONV_ROWS), :] = src[pl.ds(r0, W_CONV_ROWS), :].astype(BF16)
            return carry
        lax.fori_loop(0, src.shape[0] // W_CONV_ROWS, body, 0)

    active = i < nb_ref[0]

    @pl.when(active & (i == 0))
    def _():
        for cp in copies(be_ref[0]):
            cp.start()

    @pl.when(active & (first_ref[i] == 1))
    def _():
        for cp in copies(be_ref[i]):
            cp.wait()
        convert(sg_ref, wg_ref)
        convert(su_ref, wu_ref)
        convert(sd_ref, wd_ref)

        @pl.when(nxt_ref[i] >= 0)
        def _():
            for cp in copies(nxt_ref[i]):
                cp.start()

    @pl.when(active)
    def _():
        x = x_ref[...]
        hg = jnp.dot(x, wg_ref[...], preferred_element_type=F32)
        hu = jnp.dot(x, wu_ref[...], preferred_element_type=F32)
        h = (_silu(hg) * hu).astype(BF16)
        o_ref[...] = jnp.dot(h, wd_ref[...], preferred_element_type=F32).astype(o_ref.dtype)

    @pl.when(jnp.logical_not(active))
    def _():
        o_ref[...] = jnp.zeros_like(o_ref)


def _expert_blocks(buf, block_expert, n_used, first, nxt, wg, wu, wd, layer):
    rows, d = buf.shape
    n_blocks = rows // MOE_BLOCK
    de = wg.shape[-1]
    grid_spec = pltpu.PrefetchScalarGridSpec(
        num_scalar_prefetch=4,
        grid=(n_blocks,),
        in_specs=[pl.BlockSpec((MOE_BLOCK, d), lambda i, *_: (i, 0)),
                  pl.BlockSpec(memory_space=pl.ANY),
                  pl.BlockSpec(memory_space=pl.ANY),
                  pl.BlockSpec(memory_space=pl.ANY)],
        out_specs=pl.BlockSpec((MOE_BLOCK, d), lambda i, *_: (i, 0)),
        scratch_shapes=[pltpu.VMEM((d, de), F32), pltpu.VMEM((d, de), F32), pltpu.VMEM((de, d), F32),
                        pltpu.VMEM((d, de), BF16), pltpu.VMEM((d, de), BF16), pltpu.VMEM((de, d), BF16),
                        pltpu.SemaphoreType.DMA((3,))],
    )
    return pl.pallas_call(
        functools.partial(_expert_kernel, layer),
        grid_spec=grid_spec,
        out_shape=jax.ShapeDtypeStruct((rows, d), BF16),
        compiler_params=_cparams("arbitrary"),
        name="moe_experts",
    )(block_expert, n_used, first, nxt, buf, wg, wu, wd)


def _combine_update_kernel(tiles_per_batch, n_lat_tiles, y0_ref, y1_ref, g_ref, x_ref, p_ref, xo_ref, h_ref):
    g = g_ref[...]
    y = y0_ref[...].astype(F32) * g[:, 0:1] + y1_ref[...].astype(F32) * g[:, 1:2]
    xn, h = _post_update(tiles_per_batch, n_lat_tiles, pl.program_id(0), y, x_ref[...], p_ref)
    xo_ref[...] = xn
    h_ref[...] = h.astype(h_ref.dtype)


def _combine_update(y0, y1, gates, x, params, n_lat):
    bsz, t, d = x.shape
    n = bsz * t
    row = pl.BlockSpec((ROW_TILE, d), lambda i: (i, 0))
    xo, h = pl.pallas_call(
        functools.partial(_combine_update_kernel, t // ROW_TILE, n_lat // ROW_TILE),
        grid=(n // ROW_TILE,),
        in_specs=[row, row, pl.BlockSpec((ROW_TILE, LANE), lambda i: (i, 0)), row,
                  pl.BlockSpec((2 * bsz, 8, d), lambda i: (0, 0, 0))],
        out_specs=[row, row],
        out_shape=[jax.ShapeDtypeStruct((n, d), F32), jax.ShapeDtypeStruct((n, d), BF16)],
        compiler_params=_cparams("parallel"),
        name="moe_combine_ln",
    )(y0, y1, gates, x.reshape(n, d), _param_table(params))
    return xo.reshape(bsz, t, d), h.reshape(bsz, t, d)


def _moe(tok, logits, router_b, wg, wu, wd, layer):
    bsz, t, d = tok.shape
    n = bsz * t
    tk = n * TOP_K
    oi, gates, counts = _route(logits.reshape(n, LANE), router_b)
    experts, ranks = oi[0:2], oi[2:4]
    padded = (counts + MOE_BLOCK - 1) // MOE_BLOCK * MOE_BLOCK
    pends = jnp.cumsum(padded)
    pstarts = pends - padded
    dest = pstarts[experts] + ranks
    n_blocks = -(-tk // MOE_BLOCK) + N_EXPERTS
    blk_start = jnp.arange(n_blocks, dtype=jnp.int32) * MOE_BLOCK
    block_expert = jnp.minimum(jnp.sum(pends[None, :] <= blk_start[:, None], axis=1), N_EXPERTS - 1).astype(jnp.int32)
    n_used = (pends[-1] // MOE_BLOCK).astype(jnp.int32).reshape(1)
    prev_e = jnp.concatenate([jnp.full((1,), -1, jnp.int32), block_expert[:-1]])
    first = (block_expert != prev_e).astype(jnp.int32)
    later = jnp.where((counts[None, :] > 0) & (jnp.arange(N_EXPERTS)[None, :] > jnp.arange(N_EXPERTS)[:, None]),
                      jnp.arange(N_EXPERTS)[None, :], N_EXPERTS)
    nxt_e = jnp.min(later, axis=1)
    nxt_e = jnp.where(nxt_e >= N_EXPERTS, -1, nxt_e).astype(jnp.int32)
    nxt = nxt_e[block_expert]
    tok_id = jnp.broadcast_to(jnp.arange(n, dtype=jnp.int32)[None], (TOP_K, n))
    src = jnp.zeros((n_blocks * MOE_BLOCK,), jnp.int32).at[dest.reshape(-1)].set(tok_id.reshape(-1))
    buf = tok.reshape(n, d)[src]
    out = _expert_blocks(buf, block_expert, n_used, first, nxt, wg, wu, wd, layer)
    return out[dest[0]], out[dest[1]], gates


GLA_CS = 64
EXP_CLAMP = 80.0


def _gla_kernel(reverse, has_prev, nsub, *refs):
    if has_prev:
        (q_ref, k_ref, v_ref, g1_ref, w2_ref, gb_ref, prev_ref, r_ref, ng_ref, o_ref, s_ref) = refs
    else:
        (q_ref, k_ref, v_ref, g1_ref, w2_ref, gb_ref, o_ref, s_ref) = refs

    @pl.when(pl.program_id(1) == 0)
    def _():
        s_ref[...] = jnp.zeros_like(s_ref)

    cs = GLA_CS
    row = lax.broadcasted_iota(jnp.int32, (cs, cs), 0)
    col = lax.broadcasted_iota(jnp.int32, (cs, cs), 1)
    keep = (col >= row) if reverse else (col <= row)
    cum_m = keep.astype(F32)
    z = jnp.dot(g1_ref[0], w2_ref[0], precision=HIGHEST, preferred_element_type=F32) + gb_ref[0]
    log_gate = (jnp.minimum(z, 0.0) - jnp.log(1.0 + jnp.exp(-jnp.abs(z)))) * (1.0 / GLA_TAU)
    q_all = q_ref[0] * (GLA_HEAD_K ** -0.5)
    k_all = k_ref[0]
    v_all = v_ref[0]
    last = 0 if reverse else cs - 1
    for h in range(GLA_HEADS):
        ks = slice(h * GLA_HEAD_K, (h + 1) * GLA_HEAD_K)
        vs = slice(h * GLA_HEAD_V, (h + 1) * GLA_HEAD_V)
        state = s_ref[h]
        outs = [None] * nsub
        for c in (range(nsub - 1, -1, -1) if reverse else range(nsub)):
            rs = slice(c * cs, (c + 1) * cs)
            b = jnp.dot(cum_m, log_gate[rs, ks], precision=HIGHEST, preferred_element_type=F32)
            b_last = b[last:last + 1]
            ref = 0.5 * b_last
            qc, kc, vc = q_all[rs, ks], k_all[rs, ks], v_all[rs, vs].astype(BF16)
            q_in = (qc * jnp.exp(jnp.minimum(b - ref, EXP_CLAMP))).astype(BF16)
            k_in = (kc * jnp.exp(jnp.minimum(ref - b, EXP_CLAMP))).astype(BF16)
            att = lax.dot_general(q_in, k_in, (((1,), (1,)), ((), ())), preferred_element_type=F32)
            att = jnp.where(keep, att, 0.0).astype(BF16)
            o = jnp.dot(att, vc, preferred_element_type=F32)
            o = o + jnp.dot((qc * jnp.exp(b)).astype(BF16), state.astype(BF16), preferred_element_type=F32)
            k_tail = (kc * jnp.exp(b_last - b)).astype(BF16)
            upd = lax.dot_general(k_tail, vc, (((0,), (0,)), ((), ())), preferred_element_type=F32)
            state = jnp.exp(b_last).reshape(GLA_HEAD_K, 1) * state + upd
            outs[c] = o
        s_ref[h] = state
        o_h = jnp.concatenate(outs, axis=0)
        if has_prev:
            o_h = o_h + prev_ref[0][:, vs]
            o_h = o_h * lax.rsqrt(jnp.mean(o_h * o_h, axis=-1, keepdims=True) + EPS) * ng_ref[:, vs]
            o_h = o_h * _silu(r_ref[0][:, vs])
        o_ref[0, :, vs] = o_h.astype(o_ref.dtype)


def _gla_direction(u_all, w2pad, gbias, n_lat, reverse, prev=None, norm_g=None):
    bsz, t, _ = u_all.shape
    n = t // ROW_TILE
    nl = n_lat // ROW_TILE
    nsub = ROW_TILE // GLA_CS
    if reverse:
        blk = lambda s: n - 1 - s
    else:
        blk = lambda s: (s + nl) % n
    q0, k0, v0, r0, g0 = (AB_SIZES[0] // GLA_DK, (AB_SIZES[0] + GLA_DK) // GLA_DK, (AB_SIZES[0] + 2 * GLA_DK) // GLA_DV,
                          (AB_SIZES[0] + 2 * GLA_DK + GLA_DV) // GLA_DV, (AB_IN - 2 * GLA_RANK) // LANE)
    d = 1 if reverse else 0
    in_specs = [pl.BlockSpec((1, ROW_TILE, GLA_DK), lambda b, s: (b, blk(s), q0)),
                pl.BlockSpec((1, ROW_TILE, GLA_DK), lambda b, s: (b, blk(s), k0)),
                pl.BlockSpec((1, ROW_TILE, GLA_DV), lambda b, s: (b, blk(s), v0)),
                pl.BlockSpec((1, ROW_TILE, LANE), lambda b, s: (b, blk(s), g0)),
                pl.BlockSpec((1, LANE, GLA_DK), lambda b, s: (d, 0, 0)),
                pl.BlockSpec((1, 1, GLA_DK), lambda b, s: (d, 0, 0))]
    args = [u_all, u_all, u_all, u_all, w2pad, gbias]
    has_prev = prev is not None
    if has_prev:
        in_specs += [pl.BlockSpec((1, ROW_TILE, GLA_DV), lambda b, s: (b, blk(s), 0)),
                     pl.BlockSpec((1, ROW_TILE, GLA_DV), lambda b, s: (b, blk(s), r0)),
                     pl.BlockSpec((1, GLA_DV), lambda b, s: (0, 0))]
        args += [prev, u_all, norm_g.reshape(1, GLA_DV)]
    return pl.pallas_call(
        functools.partial(_gla_kernel, reverse, has_prev, nsub),
        grid=(bsz, n),
        in_specs=in_specs,
        out_specs=pl.BlockSpec((1, ROW_TILE, GLA_DV), lambda b, s: (b, blk(s), 0)),
        out_shape=jax.ShapeDtypeStruct((bsz, t, GLA_DV), BF16 if has_prev else F32),
        scratch_shapes=[pltpu.VMEM((GLA_HEADS, GLA_HEAD_K, GLA_HEAD_V), F32)],
        compiler_params=_cparams("parallel", "arbitrary"),
        name="gla_bwd_norm" if reverse else "gla_fwd",
    )(*args)


def _gla(u_all, gate_w2, gate_b, norm_g, n_lat):
    w2pad = jnp.zeros((2, LANE, GLA_DK), F32)
    w2pad = w2pad.at[0, :GLA_RANK].set(gate_w2[0]).at[1, GLA_RANK:2 * GLA_RANK].set(gate_w2[1])
    gbias = gate_b.reshape(2, 1, GLA_DK).astype(F32)
    o_f = _gla_direction(u_all, w2pad, gbias, n_lat, False)
    return _gla_direction(u_all, w2pad, gbias, n_lat, True, prev=o_f, norm_g=norm_g.astype(F32))


CONV_CHUNK = 512


def _shift_rows(v, dh):
    return v if dh == 0 else pltpu.roll(v, (-dh) % v.shape[0], 0)


def _ssd_conv_kernel(n_lat, x_ref, w_ref, b_ref, o_ref, pad_ref):
    t = x_ref.shape[1]
    n_ctx = t - n_lat
    pad_ref[0:GRID_W, :] = jnp.zeros((GRID_W, LANE), F32)
    pad_ref[GRID_W + n_lat:2 * GRID_W + n_lat, :] = jnp.zeros((GRID_W, LANE), F32)
    pad_ref[GRID_W:GRID_W + n_lat, :] = x_ref[0, 0:n_lat, :]
    w = w_ref[...]
    bias = b_ref[...]
    col = lax.broadcasted_iota(jnp.int32, (CONV_CHUNK, LANE), 0) % GRID_W
    masks = {-1: col >= 1, 0: None, 1: col <= GRID_W - 2}

    def body(ci, carry):
        t0 = pl.multiple_of(ci * CONV_CHUNK, CONV_CHUNK)
        acc = jnp.zeros((CONV_CHUNK, LANE), F32) + bias
        for i in range(3):
            slab = pad_ref[pl.ds(t0 + i * GRID_W, CONV_CHUNK), :]
            for j in range(3):
                sh = _shift_rows(slab, j - 1)
                if masks[j - 1] is not None:
                    sh = jnp.where(masks[j - 1], sh, 0.0)
                acc = acc + sh * w[i * 3 + j:i * 3 + j + 1]
        o_ref[0, 0, pl.ds(t0, CONV_CHUNK), :] = _silu(acc)
        return carry

    lax.fori_loop(0, n_lat // CONV_CHUNK, body, 0)
    xc = x_ref[0, n_lat:t, :]
    pos = lax.broadcasted_iota(jnp.int32, (n_ctx, LANE), 0)
    acc = xc * w[4:5] + bias
    acc = acc + jnp.where(pos >= 1, _shift_rows(xc, -1), 0.0) * w[3:4]
    acc = acc + jnp.where(pos <= n_ctx - 2, _shift_rows(xc, 1), 0.0) * w[5:6]
    o_ref[0, 0, n_lat:t, :] = _silu(acc)


def _ssd_conv(u2, conv_w, conv_b, n_lat):
    bsz, t, _ = u2.shape
    n_tiles = SSD_CONV_DIM // LANE
    c0 = SSD_D_INNER // LANE
    return pl.pallas_call(
        functools.partial(_ssd_conv_kernel, n_lat),
        grid=(bsz, n_tiles),
        in_specs=[pl.BlockSpec((1, t, LANE), lambda b, j: (b, 0, c0 + j)),
                  pl.BlockSpec((9, LANE), lambda b, j: (0, j)),
                  pl.BlockSpec((1, LANE), lambda b, j: (0, j))],
        out_specs=pl.BlockSpec((1, 1, t, LANE), lambda b, j: (b, j, 0, 0)),
        out_shape=jax.ShapeDtypeStruct((bsz, n_tiles, t, LANE), F32),
        scratch_shapes=[pltpu.VMEM((n_lat + 2 * GRID_W, LANE), F32)],
        compiler_params=_cparams("parallel", "parallel"),
        name="ssd_conv",
    )(u2, conv_w.reshape(9, SSD_CONV_DIM), conv_b.reshape(1, SSD_CONV_DIM))


SSD_CS = 128


def _softplus(x):
    return jnp.maximum(x, 0.0) + jnp.log(1.0 + jnp.exp(-jnp.abs(x)))


def _ssd_kernel(reverse, has_prev, nsub, *refs):
    if has_prev:
        (x_ref, b_ref, c_ref, dt_ref, p_ref, sel_ref, prev_ref, z_ref, ng_ref, o_ref, s_ref) = refs
    else:
        (x_ref, b_ref, c_ref, dt_ref, p_ref, sel_ref, o_ref, s_ref) = refs
    g = pl.program_id(2)

    @pl.when(pl.program_id(1) == 0)
    def _():
        s_ref[g] = jnp.zeros(s_ref.shape[1:], F32)

    cs = SSD_CS
    hw = SSD_HEADDIM
    row = lax.broadcasted_iota(jnp.int32, (cs, cs), 0)
    col = lax.broadcasted_iota(jnp.int32, (cs, cs), 1)
    keep = (col >= row) if reverse else (col <= row)
    cum_m = keep.astype(F32)
    last = 0 if reverse else cs - 1
    sel = sel_ref[0]
    pg = jnp.dot(p_ref[...], sel, precision=HIGHEST, preferred_element_type=F32)
    dtv = _softplus(jnp.dot(dt_ref[0], sel, precision=HIGHEST, preferred_element_type=F32) + pg[0:1])
    a = dtv * pg[1:2]
    for c in (range(nsub - 1, -1, -1) if reverse else range(nsub)):
        rs = slice(c * cs, (c + 1) * cs)
        acs = jnp.dot(cum_m, a[rs], precision=HIGHEST, preferred_element_type=F32)
        acs_t = acs.T
        dt_t = dtv[rs].T
        a_last = acs[last:last + 1]
        tail = jnp.exp(a_last - acs) * dtv[rs]
        e_acs = jnp.exp(acs)
        e_last = jnp.exp(a_last)
        bg = b_ref[0, 0, rs, :].astype(BF16)
        cg = c_ref[0, 0, rs, :].astype(BF16)
        cb = lax.dot_general(cg, bg, (((1,), (1,)), ((), ())), preferred_element_type=F32)
        ys = []
        for r in range(SSD_HPG):
            xh = x_ref[0, r // 2, rs, (r % 2) * hw:(r % 2 + 1) * hw]
            state = s_ref[g, r]
            decay = jnp.where(keep, jnp.exp(jnp.minimum(acs[:, r:r + 1] - acs_t[r:r + 1, :], 0.0)), 0.0)
            m = (cb * decay * dt_t[r:r + 1, :]).astype(BF16)
            y = jnp.dot(m, xh.astype(BF16), preferred_element_type=F32)
            y = y + e_acs[:, r:r + 1] * lax.dot_general(cg, state.astype(BF16), (((1,), (1,)), ((), ())),
                                                        preferred_element_type=F32)
            y = y + pg[2:3, r:r + 1] * xh
            xw = (xh * tail[:, r:r + 1]).astype(BF16)
            upd = lax.dot_general(xw, bg, (((0,), (0,)), ((), ())), preferred_element_type=F32)
            s_ref[g, r] = e_last[:, r:r + 1] * state + upd
            ys.append(y)
        if has_prev:
            yt = [jnp.concatenate(ys[2 * k:2 * k + 2], axis=1) + prev_ref[0, k, rs, :] for k in range(SSD_HPG // 2)]
            z = z_ref[0, rs, :]
            yt = [yt[k] * _silu(z[:, k * LANE:(k + 1) * LANE]) for k in range(SSD_HPG // 2)]
            ss = sum(jnp.sum(v * v, axis=-1, keepdims=True) for v in yt)
            inv = lax.rsqrt(ss * (1.0 / (SSD_HPG * hw)) + EPS)
            ng = ng_ref[...]
            for k in range(SSD_HPG // 2):
                o_ref[0, rs, k * LANE:(k + 1) * LANE] = (yt[k] * inv * ng[:, k * LANE:(k + 1) * LANE]).astype(o_ref.dtype)
        else:
            for k in range(SSD_HPG // 2):
                o_ref[0, k, rs, :] = jnp.concatenate(ys[2 * k:2 * k + 2], axis=1)


def _ssd_direction(xbc_t, u2, pvec, sel, n_lat, reverse, prev=None, norm_g=None):
    bsz, _, t, _ = xbc_t.shape
    n = t // ROW_TILE
    nl = n_lat // ROW_TILE
    nsub = ROW_TILE // SSD_CS
    tpg = SSD_HPG * SSD_HEADDIM // LANE
    gw = SSD_HPG * SSD_HEADDIM
    if reverse:
        blk = lambda s: n - 1 - s
    else:
        blk = lambda s: (s + nl) % n
    b0 = SSD_D_INNER // LANE
    c0 = b0 + SSD_GROUPS
    d = 1 if reverse else 0
    in_specs = [pl.BlockSpec((1, tpg, ROW_TILE, LANE), lambda b, s, g: (b, g, blk(s), 0)),
                pl.BlockSpec((1, 1, ROW_TILE, LANE), lambda b, s, g: (b, b0 + g, blk(s), 0)),
                pl.BlockSpec((1, 1, ROW_TILE, LANE), lambda b, s, g: (b, c0 + g, blk(s), 0)),
                pl.BlockSpec((1, ROW_TILE, LANE), lambda b, s, g: (b, blk(s), (SSD_IN - 2 * SSD_HEADS) // LANE)),
                pl.BlockSpec((8, LANE), lambda b, s, g: (0, 0)),
                pl.BlockSpec((1, LANE, LANE), lambda b, s, g: (d * SSD_GROUPS + g, 0, 0))]
    args = [xbc_t, xbc_t, xbc_t, u2, pvec, sel]
    has_prev = prev is not None
    if has_prev:
        in_specs += [pl.BlockSpec((1, tpg, ROW_TILE, LANE), lambda b, s, g: (b, g, blk(s), 0)),
                     pl.BlockSpec((1, ROW_TILE, gw), lambda b, s, g: (b, blk(s), g)),
                     pl.BlockSpec((1, gw), lambda b, s, g: (0, g))]
        args += [prev, u2, norm_g.reshape(1, SSD_D_INNER)]
        out_spec = pl.BlockSpec((1, ROW_TILE, gw), lambda b, s, g: (b, blk(s), g))
        out_shape = jax.ShapeDtypeStruct((bsz, t, SSD_D_INNER), BF16)
    else:
        out_spec = pl.BlockSpec((1, tpg, ROW_TILE, LANE), lambda b, s, g: (b, g, blk(s), 0))
        out_shape = jax.ShapeDtypeStruct((bsz, SSD_D_INNER // LANE, t, LANE), F32)
    return pl.pallas_call(
        functools.partial(_ssd_kernel, reverse, has_prev, nsub),
        grid=(bsz, n, SSD_GROUPS),
        in_specs=in_specs,
        out_specs=out_spec,
        out_shape=out_shape,
        scratch_shapes=[pltpu.VMEM((SSD_GROUPS, SSD_HPG, SSD_HEADDIM, SSD_STATE), F32)],
        compiler_params=_cparams("parallel", "arbitrary", "arbitrary"),
        name="ssd_bwd_norm" if reverse else "ssd_fwd",
    )(*args)


def _ssd(xbc_t, u2, dt_bias, a_log, d_skip, norm_g, n_lat):
    pvec = jnp.zeros((8, LANE), F32)
    pvec = pvec.at[0].set(dt_bias.reshape(-1)).at[1].set(-jnp.exp(a_log.astype(F32)).reshape(-1))
    pvec = pvec.at[2].set(d_skip.reshape(-1))
    lane = np.arange(LANE)
    sel_np = np.zeros((2 * SSD_GROUPS, LANE, LANE), np.float32)
    for dd in range(2):
        for gg in range(SSD_GROUPS):
            for r in range(SSD_HPG):
                sel_np[dd * SSD_GROUPS + gg, dd * SSD_HEADS + gg * SSD_HPG + r, r] = 1.0
    sel = jnp.asarray(sel_np)
    y_f = _ssd_direction(xbc_t, u2, pvec, sel, n_lat, False)
    return _ssd_direction(xbc_t, u2, pvec, sel, n_lat, True, prev=y_f, norm_g=norm_g.astype(F32))


HY_PAD = 8


def _hy_conv_kernel(n_lat, x_ref, w_ref, b_ref, o_ref, pad_ref):
    t = x_ref.shape[1]
    w = w_ref[...]
    bias = b_ref[...]
    ch = CONV_CHUNK
    pad_ref[0:HY_PAD, :] = jnp.zeros((HY_PAD, LANE), F32)
    pad_ref[HY_PAD + n_lat:2 * HY_PAD + n_lat, :] = jnp.zeros((HY_PAD, LANE), F32)
    pad_ref[HY_PAD:HY_PAD + n_lat, :] = x_ref[0, 0:n_lat, :]
    rowi = lax.broadcasted_iota(jnp.int32, (ch, LANE), 0)

    def body(ci, carry):
        t0 = pl.multiple_of(ci * ch, ch)
        cur = pad_ref[pl.ds(t0 + HY_PAD, ch), :]
        before = pad_ref[pl.ds(t0, HY_PAD), :][HY_PAD - 1:HY_PAD]
        after = pad_ref[pl.ds(t0 + HY_PAD + ch, HY_PAD), :][0:1]
        down = jnp.where(rowi == 0, before, _shift_rows(cur, -1))
        up = jnp.where(rowi == ch - 1, after, _shift_rows(cur, 1))
        o_ref[0, 0, pl.ds(t0, ch), :] = down * w[0:1] + cur * w[1:2] + up * w[2:3] + bias
        return carry

    lax.fori_loop(0, n_lat // ch, body, 0)
    n_ctx = t - n_lat
    xc = x_ref[0, n_lat:t, :]
    pos = lax.broadcasted_iota(jnp.int32, (n_ctx, LANE), 0)
    acc = xc * w[1:2] + bias
    acc = acc + jnp.where(pos >= 1, _shift_rows(xc, -1), 0.0) * w[0:1]
    acc = acc + jnp.where(pos <= n_ctx - 2, _shift_rows(xc, 1), 0.0) * w[2:3]
    o_ref[0, 0, n_lat:t, :] = acc


def _hy_conv(u_all, conv_w, conv_b, n_lat):
    bsz, t, _ = u_all.shape
    nch = (HY_ORDER + 1) * HY_WIDTH
    tpp = HY_WIDTH // LANE
    return pl.pallas_call(
        functools.partial(_hy_conv_kernel, n_lat),
        grid=(bsz, nch // LANE),
        in_specs=[pl.BlockSpec((1, t, LANE), lambda b, j: (b, 0, j)),
                  pl.BlockSpec((3, LANE), lambda b, j: (0, j)),
                  pl.BlockSpec((1, LANE), lambda b, j: (0, j))],
        out_specs=pl.BlockSpec((1, 1, t, LANE), lambda b, j: (j // tpp, b, 0, j % tpp)),
        out_shape=jax.ShapeDtypeStruct((HY_ORDER + 1, bsz, t, HY_WIDTH), F32),
        scratch_shapes=[pltpu.VMEM((n_lat + 2 * HY_PAD, LANE), F32)],
        compiler_params=_cparams("parallel", "parallel"),
        name="hy_short_conv",
    )(u_all, conv_w, conv_b.reshape(1, nch))


FILT_ROWS = 256


def _hy_filter_kernel(seq_len, bands_ref, w1t_ref, b1_ref, fr1_ref, w2t_ref, b2_ref, fr2_ref, w3a_ref, w3b_ref,
                      dl_ref, o_ref):
    i = pl.program_id(0)

    def lag_of(shape, axis):
        n = i * FILT_ROWS + lax.broadcasted_iota(jnp.int32, shape, axis)
        return n, jnp.where(n < seq_len, n, 2 * seq_len - n).astype(F32)

    _, pos_r = lag_of((1, FILT_ROWS), 1)
    tt_r = pos_r / max(seq_len - 1, 1)
    ang = 2.0 * math.pi * bands_ref[...] * pos_r / seq_len
    w1t = w1t_ref[...]
    nb = HY_BANDS
    hp = functools.partial(jnp.dot, precision=HIGHEST, preferred_element_type=F32)
    pre = w1t[:, 0:1] * tt_r + hp(w1t[:, 1:1 + nb], jnp.cos(ang)) - hp(w1t[:, 1 + nb:1 + 2 * nb], jnp.sin(ang))
    hid = jnp.sin(fr1_ref[...] * (pre + b1_ref[...]))
    hid = jnp.sin(fr2_ref[...] * (hp(w2t_ref[...], hid) + b2_ref[...]))
    n_c, pos_c = lag_of((FILT_ROWS, 1), 0)
    decay = jnp.exp(-(pos_c / max(seq_len - 1, 1)) * dl_ref[...])
    for o, w3_ref in enumerate((w3a_ref, w3b_ref)):
        h = lax.dot_general(hid, w3_ref[...], (((0,), (0,)), ((), ())), precision=HIGHEST,
                            preferred_element_type=F32)
        o_ref[o] = jnp.where(n_c == seq_len, 0.0, h * decay)


def _hy_filter(seq_len, f_w1, f_b1, f_fr1, f_w2, f_b2, f_fr2, f_w3):
    assert HY_ORDER == 2
    fh = f_w1.shape[1]
    emb = f_w1.shape[0]
    half = seq_len // FILT_ROWS
    bands = jnp.asarray(np.linspace(1e-4, HY_BANDS - 1, HY_BANDS, dtype=np.float32)).reshape(HY_BANDS, 1)
    deltas = jnp.asarray(np.abs(np.linspace(HY_MIN_DECAY, HY_MAX_DECAY, HY_WIDTH, dtype=np.float32))).reshape(1, HY_WIDTH)
    col = lambda v: v.reshape(fh, 1).astype(F32)
    full = lambda shape: pl.BlockSpec(shape, lambda i: (0,) * len(shape))
    w3 = f_w3.astype(F32)
    return pl.pallas_call(
        functools.partial(_hy_filter_kernel, seq_len),
        grid=(2 * half,),
        in_specs=[full((HY_BANDS, 1)), full((fh, emb)), full((fh, 1)), full((fh, 1)), full((fh, fh)), full((fh, 1)),
                  full((fh, 1)),
                  pl.BlockSpec((fh, HY_WIDTH), lambda i: (0, i // half)),
                  pl.BlockSpec((fh, HY_WIDTH), lambda i: (0, 2 + i // half)),
                  full((1, HY_WIDTH))],
        out_specs=pl.BlockSpec((HY_ORDER, FILT_ROWS, HY_WIDTH), lambda i: (0, i, 0)),
        out_shape=jax.ShapeDtypeStruct((HY_ORDER, 2 * seq_len, HY_WIDTH), F32),
        compiler_params=_cparams("parallel"),
        name="hy_filter",
    )(bands, f_w1.astype(F32).T, col(f_b1), col(f_fr1), f_w2.astype(F32).T, col(f_b2), col(f_fr2), w3, w3, deltas)


DFT_N2 = 256
DFT_S = 8


def _dft_tables(n1):
    n = n1 * DFT_N2
    k1h = n1 // 2 + 1
    k1p = -(-k1h // 8) * 8
    k1 = np.arange(k1p)[:, None].astype(np.float64)
    valid = (np.arange(k1p) < k1h)[:, None]
    th1 = 2.0 * np.pi * k1 * np.arange(n1)[None, :] / n1
    f1 = np.concatenate([np.where(valid, np.cos(th1), 0.0), np.where(valid, -np.sin(th1), 0.0)], axis=0)
    tw = 2.0 * np.pi * k1 * np.arange(DFT_N2)[None, :] / n
    tw_re = np.repeat(np.where(valid, np.cos(tw), 0.0)[:, :, None], LANE, axis=2)
    tw_im = np.repeat(np.where(valid, -np.sin(tw), 0.0)[:, :, None], LANE, axis=2)
    ph = 2.0 * np.pi * np.outer(np.arange(DFT_N2), np.arange(DFT_N2)) / DFT_N2
    f2re, f2im = np.cos(ph), -np.sin(ph)
    w_fwd = np.block([[f2re, -f2im], [f2im, f2re]])
    w_inv = np.block([[f2re, f2im], [-f2im, f2re]])
    wgt = np.where((np.arange(k1p) == 0) | (np.arange(k1p) == n1 // 2), 1.0, 2.0) * (np.arange(k1p) < k1h)
    th_i = 2.0 * np.pi * np.arange(n1 // 2)[:, None] * np.arange(k1p)[None, :] / n1
    g = np.concatenate([wgt * np.cos(th_i), -wgt * np.sin(th_i)], axis=1) / n
    f = lambda a: jnp.asarray(a.astype(np.float32))
    return dict(k1h=k1h, k1p=k1p, f1=f(f1), tw_re=f(tw_re), tw_im=f(tw_im), w_fwd=f(w_fwd), w_inv=f(w_inv), g=f(g))


def _split_bf16(v):
    hi = v.astype(BF16)
    return hi, (v - hi.astype(F32)).astype(BF16)


def _dft_dot_bf16(a, b):
    return jnp.dot(a.astype(BF16), b.astype(BF16), preferred_element_type=F32)


def _dft_dot(a, b):
    ah, al = _split_bf16(a)
    bh, bl = _split_bf16(b)
    out = jnp.dot(ah, bh, preferred_element_type=F32)
    out = out + jnp.dot(ah, bl, preferred_element_type=F32)
    return out + jnp.dot(al, bh, preferred_element_type=F32)


def _dft_first_kernel(k1p, x_ref, f1_ref, twr_ref, twi_ref, ore_ref, oim_ref):
    reps = x_ref.shape[3] // LANE
    f1 = f1_ref[...]
    for s in range(DFT_S):
        a = _dft_dot(f1, x_ref[0, :, s, :])
        tr = jnp.concatenate([twr_ref[:, s, :]] * reps, axis=1)
        ti = jnp.concatenate([twi_ref[:, s, :]] * reps, axis=1)
        are, aim = a[:k1p], a[k1p:]
        ore_ref[0, :, s, :] = are * tr - aim * ti
        oim_ref[0, :, s, :] = are * ti + aim * tr


def _dft_first(x4, lead0, nlead, f1, tab):
    rows = f1.shape[1]
    cw = x4.shape[3]
    k1p = tab['k1p']
    out = jax.ShapeDtypeStruct((nlead, k1p, DFT_N2, cw), F32)
    ospec = pl.BlockSpec((1, k1p, DFT_S, cw), lambda b, j: (b, 0, j, 0))
    tspec = pl.BlockSpec((k1p, DFT_S, LANE), lambda b, j: (0, j, 0))
    return pl.pallas_call(
        functools.partial(_dft_first_kernel, k1p),
        grid=(nlead, DFT_N2 // DFT_S),
        in_specs=[pl.BlockSpec((1, rows, DFT_S, cw), lambda b, j: (lead0 + b, 0, j, 0)),
                  pl.BlockSpec((2 * k1p, rows), lambda b, j: (0, 0)), tspec, tspec],
        out_specs=[ospec, ospec],
        out_shape=[out, out],
        compiler_params=_cparams("parallel", "parallel"),
        name="hy_dft_first",
    )(x4, f1, tab['tw_re'], tab['tw_im'])


DFT_TC = 512


def _dft_mid_kernel(k1h, fused, *refs):
    if fused:
        are_ref, aim_ref, hre_ref, him_ref, wf_ref, wi_ref, twr_ref, twi_ref, ore_ref, oim_ref = refs
    else:
        are_ref, aim_ref, wf_ref, ore_ref, oim_ref = refs
    n2 = DFT_N2

    @pl.when(pl.program_id(0) < k1h)
    def _():
        a = jnp.concatenate([are_ref[0, 0], aim_ref[0, 0]], axis=0)
        dot = _dft_dot_bf16 if fused else _dft_dot
        x = dot(wf_ref[...], a)
        xre, xim = x[:n2], x[n2:]
        if not fused:
            ore_ref[0, 0] = xre
            oim_ref[0, 0] = xim
        else:
            hre, him = hre_ref[0, 0], him_ref[0, 0]
            y = jnp.concatenate([xre * hre - xim * him, xre * him + xim * hre], axis=0)
            bb = dot(wi_ref[...], y)
            bre, bim = bb[:n2], bb[n2:]
            twr, twi = twr_ref[0], twi_ref[0]
            for c in range(bre.shape[1] // LANE):
                sl = slice(c * LANE, (c + 1) * LANE)
                ore_ref[0, 0, :, sl] = bre[:, sl] * twr + bim[:, sl] * twi
                oim_ref[0, 0, :, sl] = bim[:, sl] * twr - bre[:, sl] * twi

    @pl.when(pl.program_id(0) >= k1h)
    def _():
        ore_ref[...] = jnp.zeros_like(ore_ref)
        oim_ref[...] = jnp.zeros_like(oim_ref)


def _dft_mid(are, aim, tab, spec=None, order=0):
    bsz, k1p, _, cw = are.shape
    tc = min(DFT_TC, cw)
    blk = pl.BlockSpec((1, 1, DFT_N2, tc), lambda k, c, b: (b, k, 0, c))
    wspec = pl.BlockSpec((2 * DFT_N2, 2 * DFT_N2), lambda k, c, b: (0, 0))
    fused = spec is not None
    if fused:
        hspec = pl.BlockSpec((1, 1, DFT_N2, tc), lambda k, c, b: (order, k, 0, c))
        tspec = pl.BlockSpec((1, DFT_N2, LANE), lambda k, c, b: (k, 0, 0))
        in_specs = [blk, blk, hspec, hspec, wspec, wspec, tspec, tspec]
        args = [are, aim, spec[0], spec[1], tab['w_fwd'].astype(BF16), tab['w_inv'].astype(BF16),
                tab['tw_re'], tab['tw_im']]
    else:
        in_specs = [blk, blk, wspec]
        args = [are, aim, tab['w_fwd']]
    out = jax.ShapeDtypeStruct((bsz, k1p, DFT_N2, cw), F32)
    ore, oim = pl.pallas_call(
        functools.partial(_dft_mid_kernel, tab['k1h'], fused),
        grid=(k1p, cw // tc, bsz),
        in_specs=in_specs,
        out_specs=[blk, blk],
        out_shape=[out, out],
        compiler_params=_cparams("parallel", "parallel", "arbitrary"),
        name="hy_dft_mid_conv" if fused else "hy_dft_mid_filter",
    )(*args)
    return ore, oim


def _dft_last_kernel(bre_ref, bim_ref, g_ref, u_ref, x_ref, d_ref, o_ref):
    g = g_ref[...]
    d = d_ref[...]
    rows = g.shape[0]
    for s in range(DFT_S):
        bb = jnp.concatenate([bre_ref[0, :, s, :], bim_ref[0, :, s, :]], axis=0)
        y = _dft_dot(g, bb)
        u = u_ref[0, :, s, :]
        o_ref[0, 0:rows, s, :] = x_ref[0, :, s, :] * (y + u * d)
    if o_ref.shape[1] > rows:
        o_ref[0, rows:, :, :] = jnp.zeros((o_ref.shape[1] - rows,) + o_ref.shape[2:], F32)


def _dft_last(bre, bim, tab, u4, u_lead0, x4, x_lead0, dvec, out_rows):
    bsz, k1p, _, cw = bre.shape
    rows = tab['g'].shape[0]
    bspec = pl.BlockSpec((1, k1p, DFT_S, cw), lambda b, j: (b, 0, j, 0))
    return pl.pallas_call(
        _dft_last_kernel,
        grid=(bsz, DFT_N2 // DFT_S),
        in_specs=[bspec, bspec, pl.BlockSpec((rows, 2 * k1p), lambda b, j: (0, 0)),
                  pl.BlockSpec((1, rows, DFT_S, cw), lambda b, j: (u_lead0 + b, 0, j, 0)),
                  pl.BlockSpec((1, rows, DFT_S, cw), lambda b, j: (x_lead0 + b, 0, j, 0)),
                  pl.BlockSpec((1, cw), lambda b, j: (0, 0))],
        out_specs=pl.BlockSpec((1, out_rows, DFT_S, cw), lambda b, j: (b, 0, j, 0)),
        out_shape=jax.ShapeDtypeStruct((bsz, out_rows, DFT_N2, cw), F32),
        compiler_params=_cparams("parallel", "parallel"),
        name="hy_dft_last",
    )(bre, bim, tab['g'], u4, x4, dvec.reshape(1, cw).astype(F32))


def _ctx_dft_tables(n):
    size = 2 * n
    kp = -(-(n + 1) // 8) * 8
    k = np.arange(kp)[:, None].astype(np.float64)
    valid = (np.arange(kp) <= n)[:, None]
    th = 2.0 * np.pi * k * np.arange(size)[None, :] / size
    fw = np.concatenate([np.where(valid, np.cos(th), 0.0), np.where(valid, -np.sin(th), 0.0)], axis=0)
    wgt = np.where((np.arange(kp) == 0) | (np.arange(kp) == n), 1.0, 2.0) * (np.arange(kp) <= n)
    thi = 2.0 * np.pi * np.arange(n)[:, None] * np.arange(kp)[None, :] / size
    inv = np.concatenate([wgt * np.cos(thi), -wgt * np.sin(thi)], axis=1) / size
    return kp, jnp.asarray(fw.astype(np.float32)), jnp.asarray(inv.astype(np.float32))


def _hy_ctx_kernel(kp, p_ref, f_ref, d_ref, fw_ref, inv_ref, y_ref, o_ref):
    del y_ref
    n = p_ref.shape[2]
    fw = fw_ref[...]
    y = p_ref[0, 0]
    for o in range(HY_ORDER):
        h = _dft_dot(fw, f_ref[o])
        x = _dft_dot(fw[:, :n], y)
        hre, him, xre, xim = h[:kp], h[kp:], x[:kp], x[kp:]
        prod = jnp.concatenate([xre * hre - xim * him, xre * him + xim * hre], axis=0)
        conv = _dft_dot(inv_ref[...], prod)
        y = p_ref[o + 1, 0] * (conv + y * d_ref[o:o + 1, :])
    o_ref[0] = y


CTX_TC = 256


def _hy_ctx(parts, filt_c, long_bias, y_all, n_lat):
    nparts, bsz, t, cw = parts.shape
    n_ctx = t - n_lat
    kp, fw, inv = _ctx_dft_tables(n_ctx)
    return pl.pallas_call(
        functools.partial(_hy_ctx_kernel, kp),
        grid=(bsz, cw // CTX_TC),
        in_specs=[pl.BlockSpec((nparts, 1, n_ctx, CTX_TC), lambda b, c: (0, b, n_lat // n_ctx, c)),
                  pl.BlockSpec((HY_ORDER, 2 * n_ctx, CTX_TC), lambda b, c: (0, 0, c)),
                  pl.BlockSpec((HY_ORDER, CTX_TC), lambda b, c: (0, c)),
                  pl.BlockSpec(fw.shape, lambda b, c: (0, 0)),
                  pl.BlockSpec(inv.shape, lambda b, c: (0, 0)),
                  pl.BlockSpec(memory_space=pl.ANY)],
        out_specs=pl.BlockSpec((1, n_ctx, CTX_TC), lambda b, c: (b, n_lat // n_ctx, c)),
        out_shape=jax.ShapeDtypeStruct(y_all.shape, F32),
        input_output_aliases={5: 0},
        compiler_params=_cparams("parallel", "parallel"),
        name="hy_ctx_conv",
    )(parts, filt_c, long_bias.astype(F32), fw, inv, y_all)


def _hyena(u_all, hy_p, n_lat):
    conv_w, conv_b, f_w1, f_b1, f_fr1, f_w2, f_b2, f_fr2, f_w3, long_bias = hy_p
    bsz, t, _ = u_all.shape
    n_ctx = t - n_lat
    assert n_ctx == DFT_N2 and n_lat % (2 * DFT_N2) == 0
    n1 = 2 * n_lat // DFT_N2
    tab = _dft_tables(n1)
    cw = HY_WIDTH
    parts = _hy_conv(u_all, conv_w, conv_b, n_lat)
    filt_l = _hy_filter(n_lat, f_w1, f_b1, f_fr1, f_w2, f_b2, f_fr2, f_w3)
    filt_c = _hy_filter(n_ctx, f_w1, f_b1, f_fr1, f_w2, f_b2, f_fr2, f_w3)
    spec = _dft_mid(*_dft_first(filt_l.reshape(HY_ORDER, n1, DFT_N2, cw), 0, HY_ORDER, tab['f1'], tab), tab)
    f1_half = tab['f1'][:, :n1 // 2]
    parts4 = parts.reshape((HY_ORDER + 1) * bsz, t // DFT_N2, DFT_N2, cw)
    y4, y_lead0 = parts4, 0
    for o in range(HY_ORDER):
        are, aim = _dft_first(y4, y_lead0, bsz, f1_half, tab)
        bre, bim = _dft_mid(are, aim, tab, spec=spec, order=o)
        last = o + 1 == HY_ORDER
        y4 = _dft_last(bre, bim, tab, y4, y_lead0, parts4, (o + 1) * bsz, long_bias[o],
                       t // DFT_N2 if last else n1 // 2)
        y_lead0 = 0
    return _hy_ctx(parts, filt_c, long_bias, y4.reshape(bsz, t, cw), n_lat)


def _row_tile(m):
    for tm in (1280, 1024, 512, 256):
        if m % tm == 0:
            return tm
    raise ValueError(m)


def _col_tile(n):
    for k in range(n // LANE, 0, -1):
        if n % (k * LANE) == 0 and k * LANE <= 1280:
            return k * LANE
    raise ValueError(n)


def _in_proj(h, w):
    bsz, t, d = h.shape
    n = w.shape[1]
    return _matmul(h.reshape(bsz * t, d), w, _row_tile(bsz * t), _col_tile(n)).reshape(bsz, t, n)


def kernel(x, c, ctx, c_ctx, router_w, router_b, mod_w, mod_b, ln_g, ln_b, exp_w_gate, exp_w_up, exp_w_down, ab_w_in, ab_w_out, hy_conv_w, hy_conv_b, hy_f_w1, hy_f_b1, hy_f_fr1, hy_f_w2, hy_f_b2, hy_f_fr2, hy_f_w3, hy_long_bias, gla_gate_w2, gla_gate_b, gla_norm_g, ssd_w_in, ssd_conv_w, ssd_conv_b, ssd_dt_bias, ssd_a_log, ssd_d, ssd_norm_g, ssd_w_out):
    bsz, n_lat, d = x.shape
    n_ctx = ctx.shape[1]
    assert bsz < 8 and n_lat % ROW_TILE == 0 and n_ctx % ROW_TILE == 0 and d == D_MODEL
    xa = jnp.concatenate([x, ctx], axis=1).astype(F32)
    c8 = jnp.zeros((8, d), F32).at[:bsz].set(c).at[bsz].set(c_ctx)
    router_w_pad = jnp.zeros((d, LANE), F32).at[:, :N_EXPERTS].set(router_w)
    mods = [_mod_vectors(c8, mod_w[i].astype(F32), mod_b[i].astype(F32)).reshape(8, 6, d) for i in range(DEPTH)]

    def rows(i, idx_l):
        lat = jnp.stack([mods[i][:bsz, k] for k in idx_l], axis=1)
        cx = jnp.broadcast_to(jnp.stack([mods[i][bsz, k] for k in idx_l], axis=0)[None], lat.shape)
        return jnp.concatenate([lat, cx], axis=1)

    def with_ln(p, g, b):
        extra = jnp.broadcast_to(jnp.stack([g, b], axis=0).astype(F32)[None], (bsz, 2, d))
        return jnp.concatenate([p, extra], axis=1)

    h = _modulate(xa, rows(0, (0, 1)), n_lat)
    for i in range(DEPTH):
        j = i // 2
        if i % 2 == 0:
            w_pad = -(-AB_IN // LANE) * LANE
            w_in = jnp.zeros((d, w_pad), BF16).at[:, :AB_IN].set(ab_w_in[j].astype(BF16))
            u_all = _in_proj(h, w_in)
            hy_p = (hy_conv_w[j], hy_conv_b[j], hy_f_w1[j], hy_f_b1[j], hy_f_fr1[j], hy_f_w2[j], hy_f_b2[j],
                    hy_f_fr2[j], hy_f_w3[j], hy_long_bias[j])
            y_hy = _hyena(u_all, hy_p, n_lat)
            y_gla = _gla(u_all, gla_gate_w2[j], gla_gate_b[j], gla_norm_g[j], n_lat)
            w_out = ab_w_out[j].astype(BF16)
            a_list, w_list = [y_hy, y_gla], [w_out[:HY_WIDTH], w_out[HY_WIDTH:]]
        else:
            u2 = _in_proj(h, ssd_w_in[j].astype(BF16))
            xbc_t = _ssd_conv(u2, ssd_conv_w[j].astype(F32), ssd_conv_b[j].astype(F32), n_lat)
            y_ssd = _ssd(xbc_t, u2, ssd_dt_bias[j].astype(F32), ssd_a_log[j], ssd_d[j].astype(F32), ssd_norm_g[j], n_lat)
            a_list, w_list = [y_ssd], [ssd_w_out[j].astype(BF16)]
        p1 = with_ln(rows(i, (2, 4, 3)), ln_g[i, 0], ln_b[i, 0])
        xa, tok, logits = _proj_update(a_list, w_list, xa, p1, router_w_pad, n_lat)
        y0, y1, gates = _moe(tok, logits, router_b, exp_w_gate.astype(F32), exp_w_up.astype(F32),
                             exp_w_down.astype(F32), i)
        gate2 = rows(i, (5,))
        if i + 1 < DEPTH:
            nxt = rows(i + 1, (1, 0))
        else:
            nxt = jnp.zeros((bsz, 4, d), F32)
        p2 = jnp.concatenate([gate2[:, 0:1], nxt[:, 0:2], gate2[:, 1:2], nxt[:, 2:4]], axis=1)
        xa, h = _combine_update(y0, y1, gates, xa, with_ln(p2, ln_g[i, 1], ln_b[i, 1]), n_lat)
    return xa[:, :n_lat].astype(x.dtype)
```

```python
import functools
import math

import numpy as np
import jax
import jax.numpy as jnp
from jax import lax
from jax.experimental import pallas as pl
from jax.experimental.pallas import tpu as pltpu

F32 = jnp.float32
BF16 = jnp.bfloat16
HIGHEST = lax.Precision.HIGHEST

D_MODEL = 2048
DEPTH = 2
GRID_W = 64
HY_WIDTH = D_MODEL // 2
HY_ORDER = 2
HY_BANDS = 16
HY_MIN_DECAY = math.log(1e-2) / 1.5
HY_MAX_DECAY = math.log(1e-2) / 0.3
GLA_HEADS = 4
GLA_DK = D_MODEL // 4
GLA_DV = D_MODEL // 2
GLA_HEAD_K = GLA_DK // GLA_HEADS
GLA_HEAD_V = GLA_DV // GLA_HEADS
GLA_RANK = 16
GLA_TAU = 16.0
AB_SIZES = ((HY_ORDER + 1) * HY_WIDTH, GLA_DK, GLA_DK, GLA_DV, GLA_DV, GLA_RANK, GLA_RANK)
AB_IN = sum(AB_SIZES)
SSD_D_INNER = 2 * D_MODEL
SSD_HEADDIM = 64
SSD_HEADS = SSD_D_INNER // SSD_HEADDIM
SSD_GROUPS = 8
SSD_HPG = SSD_HEADS // SSD_GROUPS
SSD_STATE = 128
SSD_CONV_DIM = SSD_D_INNER + 2 * SSD_GROUPS * SSD_STATE
SSD_IN = SSD_D_INNER + SSD_CONV_DIM + 2 * SSD_HEADS
N_EXPERTS = 16
N_EXPERT_GROUPS = 4
EXPERTS_PER_GROUP = N_EXPERTS // N_EXPERT_GROUPS
TOP_K = 2
D_EXPERT = D_MODEL // 2
ALPHA = (2 * DEPTH) ** 0.25
EPS = 1e-6

LANE = 128
ROW_TILE = 256
MOE_BLOCK = 256
VMEM_LIMIT = 56 * 1024 * 1024


def _cparams(*sem):
    return pltpu.CompilerParams(dimension_semantics=sem, vmem_limit_bytes=VMEM_LIMIT)


def _silu(x):
    return x * (1.0 / (1.0 + jnp.exp(-x)))


def _mod_kernel(c_ref, w_ref, b_ref, o_ref):
    o_ref[...] = jnp.dot(_silu(c_ref[...]), w_ref[...], precision=HIGHEST, preferred_element_type=F32) + b_ref[...]


def _mod_vectors(c8, w, b):
    d, n = w.shape
    tn = 1536
    return pl.pallas_call(
        _mod_kernel,
        grid=(n // tn,),
        in_specs=[pl.BlockSpec((8, d), lambda j: (0, 0)),
                  pl.BlockSpec((d, tn), lambda j: (0, j)),
                  pl.BlockSpec((1, tn), lambda j: (0, j))],
        out_specs=pl.BlockSpec((8, tn), lambda j: (0, j)),
        out_shape=jax.ShapeDtypeStruct((8, n), F32),
        compiler_params=_cparams("arbitrary"),
        name="adaln_vectors",
    )(c8, w, b.reshape(1, n))


def _modulate_kernel(n_lat_tiles, x_ref, p_ref, o_ref):
    is_ctx = pl.program_id(1) >= n_lat_tiles
    p = p_ref[0]
    shift = jnp.where(is_ctx, p[2:3], p[0:1])
    scale = jnp.where(is_ctx, p[3:4], p[1:2])
    o_ref[0] = (x_ref[0] * (1.0 + scale) + shift).astype(o_ref.dtype)


def _modulate(x, params, n_lat):
    bsz, t, d = x.shape
    return pl.pallas_call(
        functools.partial(_modulate_kernel, n_lat // ROW_TILE),
        grid=(bsz, t // ROW_TILE),
        in_specs=[pl.BlockSpec((1, ROW_TILE, d), lambda b, i: (b, i, 0)),
                  pl.BlockSpec((1, 4, d), lambda b, i: (b, 0, 0))],
        out_specs=pl.BlockSpec((1, ROW_TILE, d), lambda b, i: (b, i, 0)),
        out_shape=jax.ShapeDtypeStruct((bsz, t, d), BF16),
        compiler_params=_cparams("parallel", "parallel"),
        name="modulate",
    )(x, params)


def _matmul_kernel(a_ref, w_ref, o_ref):
    o_ref[...] = jnp.dot(a_ref[...], w_ref[...], preferred_element_type=F32).astype(o_ref.dtype)


def _matmul(a, w, tm, tn, out_dtype=F32):
    m, k = a.shape
    n = w.shape[1]
    return pl.pallas_call(
        _matmul_kernel,
        grid=(m // tm, n // tn),
        in_specs=[pl.BlockSpec((tm, k), lambda i, j: (i, 0)),
                  pl.BlockSpec((k, tn), lambda i, j: (0, j))],
        out_specs=pl.BlockSpec((tm, tn), lambda i, j: (i, j)),
        out_shape=jax.ShapeDtypeStruct((m, n), out_dtype),
        compiler_params=_cparams("parallel", "arbitrary"),
        name="in_proj",
    )(a, w)


def _layer_norm_rows(v, g, b):
    mu = jnp.mean(v, axis=-1, keepdims=True)
    vc = v - mu
    var = jnp.mean(vc * vc, axis=-1, keepdims=True)
    return vc * lax.rsqrt(var + EPS) * g + b


def _param_table(p):
    bsz, _, d = p.shape
    pad = jnp.zeros((bsz, 3, d), F32)
    lat = jnp.concatenate([p[:, 0:3], p[:, 6:8], pad], axis=1)
    cx = jnp.concatenate([p[:, 3:6], p[:, 6:8], pad], axis=1)
    return jnp.stack([lat, cx], axis=1).reshape(2 * bsz, 8, d)


def _post_update(tiles_per_batch, n_lat_tiles, tile, y, x, p_ref):
    b = tile // tiles_per_batch
    is_ctx = (tile - b * tiles_per_batch >= n_lat_tiles).astype(jnp.int32)
    p = p_ref[2 * b + is_ctx]
    xn = _layer_norm_rows(ALPHA * x + p[0:1] * y, p[3:4], p[4:5])
    return xn, xn * (1.0 + p[1:2]) + p[2:3]


PROJ_HALVES = 2


def _proj_update_kernel(tiles_per_batch, n_lat_tiles, n_a, *refs):
    a_refs = refs[:n_a]
    w_refs = refs[n_a:2 * n_a]
    x_ref, p_ref, rw_ref, xo_ref, tok_ref, lg_ref = refs[2 * n_a:]
    for hh in range(PROJ_HALVES):
        rs = slice(hh * ROW_TILE, (hh + 1) * ROW_TILE)
        y = jnp.dot(a_refs[0][rs, :].astype(BF16), w_refs[0][...], preferred_element_type=F32)
        for a_ref, w_ref in zip(a_refs[1:], w_refs[1:]):
            y = y + jnp.dot(a_ref[rs, :].astype(BF16), w_ref[...], preferred_element_type=F32)
        tile = pl.program_id(0) * PROJ_HALVES + hh
        xn, tok = _post_update(tiles_per_batch, n_lat_tiles, tile, y, x_ref[rs, :], p_ref)
        xo_ref[rs, :] = xn
        tok_ref[rs, :] = tok.astype(tok_ref.dtype)
        lg_ref[rs, :] = jnp.dot(tok, rw_ref[...], precision=HIGHEST, preferred_element_type=F32)


def _proj_update(a_list, w_list, x, params, router_w_pad, n_lat):
    bsz, t, d = x.shape
    n = bsz * t
    tm = PROJ_HALVES * ROW_TILE
    assert n % tm == 0
    n_a = len(a_list)
    flat = lambda v: v.reshape(n, v.shape[-1])
    in_specs = [pl.BlockSpec((tm, a.shape[-1]), lambda i: (i, 0)) for a in a_list]
    in_specs += [pl.BlockSpec(w.shape, lambda i: (0, 0)) for w in w_list]
    in_specs += [pl.BlockSpec((tm, d), lambda i: (i, 0)),
                 pl.BlockSpec((2 * bsz, 8, d), lambda i: (0, 0, 0)),
                 pl.BlockSpec((d, LANE), lambda i: (0, 0))]
    xo, tok, lg = pl.pallas_call(
        functools.partial(_proj_update_kernel, t // ROW_TILE, n_lat // ROW_TILE, n_a),
        grid=(n // tm,),
        in_specs=in_specs,
        out_specs=[pl.BlockSpec((tm, d), lambda i: (i, 0)),
                   pl.BlockSpec((tm, d), lambda i: (i, 0)),
                   pl.BlockSpec((tm, LANE), lambda i: (i, 0))],
        out_shape=[jax.ShapeDtypeStruct((n, d), F32),
                   jax.ShapeDtypeStruct((n, d), BF16),
                   jax.ShapeDtypeStruct((n, LANE), F32)],
        compiler_params=_cparams("parallel"),
        name="out_proj_ln",
    )(*[flat(a) for a in a_list], *w_list, flat(x), _param_table(params), router_w_pad)
    return xo.reshape(bsz, t, d), tok.reshape(bsz, t, d), lg.reshape(bsz, t, LANE)


def _route_kernel(lg_ref, rb_ref, tri_ref, oi_ref, og_ref, cnt_ref, carry_ref):
    @pl.when(pl.program_id(0) == 0)
    def _():
        carry_ref[...] = jnp.zeros_like(carry_ref)

    tm = lg_ref.shape[0]
    epg = EXPERTS_PER_GROUP
    lt = lg_ref[...].T[:N_EXPERTS]
    score = 1.0 / (1.0 + jnp.exp(-lt))
    sel = score + rb_ref[...]
    best_g = None
    for q in range(N_EXPERT_GROUPS):
        rows = [sel[q * epg + r:q * epg + r + 1] for r in range(epg)]
        gs = None
        for a in range(epg):
            for b in range(a + 1, epg):
                ps = rows[a] + rows[b]
                gs = ps if gs is None else jnp.maximum(gs, ps)
        if best_g is None:
            best_g, grp = gs, jnp.zeros_like(gs, dtype=jnp.int32)
        else:
            better = gs > best_g
            grp = jnp.where(better, q, grp)
            best_g = jnp.where(better, gs, best_g)
    in_sel, in_score = [], []
    for r in range(epg):
        v = sel[r:r + 1]
        s = score[r:r + 1]
        for q in range(1, N_EXPERT_GROUPS):
            v = jnp.where(grp == q, sel[q * epg + r:q * epg + r + 1], v)
            s = jnp.where(grp == q, score[q * epg + r:q * epg + r + 1], s)
        in_sel.append(v)
        in_score.append(s)
    i1, v1 = jnp.zeros_like(grp), in_sel[0]
    for r in range(1, epg):
        better = in_sel[r] > v1
        i1 = jnp.where(better, r, i1)
        v1 = jnp.where(better, in_sel[r], v1)
    i2, v2 = None, None
    for r in range(epg):
        cand = jnp.where(i1 == r, -jnp.inf, in_sel[r])
        if i2 is None:
            i2, v2 = jnp.zeros_like(grp), cand
        else:
            better = cand > v2
            i2 = jnp.where(better, r, i2)
            v2 = jnp.where(better, cand, v2)
    s1, s2 = in_score[0], in_score[0]
    for r in range(1, epg):
        s1 = jnp.where(i1 == r, in_score[r], s1)
        s2 = jnp.where(i2 == r, in_score[r], s2)
    e1 = grp * epg + i1
    e2 = grp * epg + i2
    tot = s1 + s2
    g1, g2 = s1 / tot, s2 / tot
    eid = lax.broadcasted_iota(jnp.int32, (N_EXPERTS, tm), 0)
    oh1 = eid == e1
    oh2 = eid == e2
    both = jnp.where(oh1 | oh2, 1.0, 0.0)
    before = jnp.dot(both.astype(BF16), tri_ref[...], preferred_element_type=F32) + carry_ref[:, 0:1]
    r1 = jnp.sum(jnp.where(oh1, before, 0.0), axis=0, keepdims=True)
    r2 = jnp.sum(jnp.where(oh2, before, 0.0), axis=0, keepdims=True)
    new_carry = carry_ref[...] + jnp.sum(both, axis=1, keepdims=True)
    carry_ref[...] = new_carry
    cnt_ref[...] = new_carry.astype(jnp.int32)
    zi = jnp.zeros((4, tm), jnp.int32)
    oi_ref[...] = jnp.concatenate([e1, e2, r1.astype(jnp.int32), r2.astype(jnp.int32), zi], axis=0)
    gt = jnp.concatenate([g1, g2, jnp.zeros((LANE - 2, tm), F32)], axis=0)
    og_ref[...] = gt.T


def _route(logits, router_b):
    n = logits.shape[0]
    tm = ROW_TILE
    tri = jnp.asarray(np.triu(np.ones((tm, tm), np.float32), 1)).astype(BF16)
    oi, og, cnt = pl.pallas_call(
        _route_kernel,
        grid=(n // tm,),
        in_specs=[pl.BlockSpec((tm, LANE), lambda i: (i, 0)),
                  pl.BlockSpec((N_EXPERTS, 1), lambda i: (0, 0)),
                  pl.BlockSpec((tm, tm), lambda i: (0, 0))],
        out_specs=[pl.BlockSpec((8, tm), lambda i: (0, i)),
                   pl.BlockSpec((tm, LANE), lambda i: (i, 0)),
                   pl.BlockSpec((N_EXPERTS, LANE), lambda i: (0, 0))],
        out_shape=[jax.ShapeDtypeStruct((8, n), jnp.int32),
                   jax.ShapeDtypeStruct((n, LANE), F32),
                   jax.ShapeDtypeStruct((N_EXPERTS, LANE), jnp.int32)],
        scratch_shapes=[pltpu.VMEM((N_EXPERTS, LANE), F32)],
        compiler_params=_cparams("arbitrary"),
        name="moe_route",
    )(logits, router_b.reshape(N_EXPERTS, 1).astype(F32), tri)
    return oi, og, cnt[:, 0]


W_CONV_ROWS = 256


def _expert_kernel(layer, be_ref, nb_ref, first_ref, nxt_ref, x_ref, wg_hbm, wu_hbm, wd_hbm, o_ref,
                   sg_ref, su_ref, sd_ref, wg_ref, wu_ref, wd_ref, sem):
    i = pl.program_id(0)

    def copies(e):
        return (pltpu.make_async_copy(wg_hbm.at[layer, e], sg_ref, sem.at[0]),
                pltpu.make_async_copy(wu_hbm.at[layer, e], su_ref, sem.at[1]),
                pltpu.make_async_copy(wd_hbm.at[layer, e], sd_ref, sem.at[2]))

    def convert(src, dst):
        def body(r, carry):
            r0 = pl.multiple_of(r * W_CONV_ROWS, W_CONV_ROWS)
            dst[pl.ds(r0, W_CONV_ROWS), :] = src[pl.ds(r0, W_CONV_ROWS), :].astype(BF16)
            return carry
        lax.fori_loop(0, src.shape[0] // W_CONV_ROWS, body, 0)

    active = i < nb_ref[0]

    @pl.when(active & (i == 0))
    def _():
        for cp in copies(be_ref[0]):
            cp.start()

    @pl.when(active & (first_ref[i] == 1))
    def _():
        for cp in copies(be_ref[i]):
            cp.wait()
        convert(sg_ref, wg_ref)
        convert(su_ref, wu_ref)
        convert(sd_ref, wd_ref)

        @pl.when(nxt_ref[i] >= 0)
        def _():
            for cp in copies(nxt_ref[i]):
                cp.start()

    @pl.when(active)
    def _():
        x = x_ref[...]
        hg = jnp.dot(x, wg_ref[...], preferred_element_type=F32)
        hu = jnp.dot(x, wu_ref[...], preferred_element_type=F32)
        h = (_silu(hg) * hu).astype(BF16)
        o_ref[...] = jnp.dot(h, wd_ref[...], preferred_element_type=F32).astype(o_ref.dtype)

    @pl.when(jnp.logical_not(active))
    def _():
        o_ref[...] = jnp.zeros_like(o_ref)


def _expert_blocks(buf, block_expert, n_used, first, nxt, wg, wu, wd, layer):
    rows, d = buf.shape
    n_blocks = rows // MOE_BLOCK
    de = wg.shape[-1]
    grid_spec = pltpu.PrefetchScalarGridSpec(
        num_scalar_prefetch=4,
        grid=(n_blocks,),
        in_specs=[pl.BlockSpec((MOE_BLOCK, d), lambda i, *_: (i, 0)),
                  pl.BlockSpec(memory_space=pl.ANY),
                  pl.BlockSpec(memory_space=pl.ANY),
                  pl.BlockSpec(memory_space=pl.ANY)],
        out_specs=pl.BlockSpec((MOE_BLOCK, d), lambda i, *_: (i, 0)),
        scratch_shapes=[pltpu.VMEM((d, de), F32), pltpu.VMEM((d, de), F32), pltpu.VMEM((de, d), F32),
                        pltpu.VMEM((d, de), BF16), pltpu.VMEM((d, de), BF16), pltpu.VMEM((de, d), BF16),
                        pltpu.SemaphoreType.DMA((3,))],
    )
    return pl.pallas_call(
        functools.partial(_expert_kernel, layer),
        grid_spec=grid_spec,
        out_shape=jax.ShapeDtypeStruct((rows, d), BF16),
        compiler_params=_cparams("arbitrary"),
        name="moe_experts",
    )(block_expert, n_used, first, nxt, buf, wg, wu, wd)


def _combine_update_kernel(tiles_per_batch, n_lat_tiles, y0_ref, y1_ref, g_ref, x_ref, p_ref, xo_ref, h_ref):
    g = g_ref[...]
    y = y0_ref[...].astype(F32) * g[:, 0:1] + y1_ref[...].astype(F32) * g[:, 1:2]
    xn, h = _post_update(tiles_per_batch, n_lat_tiles, pl.program_id(0), y, x_ref[...], p_ref)
    xo_ref[...] = xn
    h_ref[...] = h.astype(h_ref.dtype)


def _combine_update(y0, y1, gates, x, params, n_lat):
    bsz, t, d = x.shape
    n = bsz * t
    row = pl.BlockSpec((ROW_TILE, d), lambda i: (i, 0))
    xo, h = pl.pallas_call(
        functools.partial(_combine_update_kernel, t // ROW_TILE, n_lat // ROW_TILE),
        grid=(n // ROW_TILE,),
        in_specs=[row, row, pl.BlockSpec((ROW_TILE, LANE), lambda i: (i, 0)), row,
                  pl.BlockSpec((2 * bsz, 8, d), lambda i: (0, 0, 0))],
        out_specs=[row, row],
        out_shape=[jax.ShapeDtypeStruct((n, d), F32), jax.ShapeDtypeStruct((n, d), BF16)],
        compiler_params=_cparams("parallel"),
        name="moe_combine_ln",
    )(y0, y1, gates, x.reshape(n, d), _param_table(params))
    return xo.reshape(bsz, t, d), h.reshape(bsz, t, d)


def _moe(tok, logits, router_b, wg, wu, wd, layer):
    bsz, t, d = tok.shape
    n = bsz * t
    tk = n * TOP_K
    oi, gates, counts = _route(logits.reshape(n, LANE), router_b)
    experts, ranks = oi[0:2], oi[2:4]
    padded = (counts + MOE_BLOCK - 1) // MOE_BLOCK * MOE_BLOCK
    pends = jnp.cumsum(padded)
    pstarts = pends - padded
    dest = pstarts[experts] + ranks
    n_blocks = -(-tk // MOE_BLOCK) + N_EXPERTS
    blk_start = jnp.arange(n_blocks, dtype=jnp.int32) * MOE_BLOCK
    block_expert = jnp.minimum(jnp.sum(pends[None, :] <= blk_start[:, None], axis=1), N_EXPERTS - 1).astype(jnp.int32)
    n_used = (pends[-1] // MOE_BLOCK).astype(jnp.int32).reshape(1)
    prev_e = jnp.concatenate([jnp.full((1,), -1, jnp.int32), block_expert[:-1]])
    first = (block_expert != prev_e).astype(jnp.int32)
    later = jnp.where((counts[None, :] > 0) & (jnp.arange(N_EXPERTS)[None, :] > jnp.arange(N_EXPERTS)[:, None]),
                      jnp.arange(N_EXPERTS)[None, :], N_EXPERTS)
    nxt_e = jnp.min(later, axis=1)
    nxt_e = jnp.where(nxt_e >= N_EXPERTS, -1, nxt_e).astype(jnp.int32)
    nxt = nxt_e[block_expert]
    tok_id = jnp.broadcast_to(jnp.arange(n, dtype=jnp.int32)[None], (TOP_K, n))
    src = jnp.zeros((n_blocks * MOE_BLOCK,), jnp.int32).at[dest.reshape(-1)].set(tok_id.reshape(-1))
    buf = tok.reshape(n, d)[src]
    out = _expert_blocks(buf, block_expert, n_used, first, nxt, wg, wu, wd, layer)
    return out[dest[0]], out[dest[1]], gates


GLA_CS = 64
EXP_CLAMP = 80.0


def _gla_kernel(reverse, has_prev, nsub, *refs):
    if has_prev:
        (q_ref, k_ref, v_ref, g1_ref, w2_ref, gb_ref, prev_ref, r_ref, ng_ref, o_ref, s_ref) = refs
    else:
        (q_ref, k_ref, v_ref, g1_ref, w2_ref, gb_ref, o_ref, s_ref) = refs

    @pl.when(pl.program_id(1) == 0)
    def _():
        s_ref[...] = jnp.zeros_like(s_ref)

    cs = GLA_CS
    row = lax.broadcasted_iota(jnp.int32, (cs, cs), 0)
    col = lax.broadcasted_iota(jnp.int32, (cs, cs), 1)
    keep = (col >= row) if reverse else (col <= row)
    cum_m = keep.astype(F32)
    z = jnp.dot(g1_ref[0], w2_ref[0], precision=HIGHEST, preferred_element_type=F32) + gb_ref[0]
    log_gate = (jnp.minimum(z, 0.0) - jnp.log(1.0 + jnp.exp(-jnp.abs(z)))) * (1.0 / GLA_TAU)
    q_all = q_ref[0] * (GLA_HEAD_K ** -0.5)
    k_all = k_ref[0]
    v_all = v_ref[0]
    last = 0 if reverse else cs - 1
    for h in range(GLA_HEADS):
        ks = slice(h * GLA_HEAD_K, (h + 1) * GLA_HEAD_K)
        vs = slice(h * GLA_HEAD_V, (h + 1) * GLA_HEAD_V)
        state = s_ref[h]
        outs = [None] * nsub
        for c in (range(nsub - 1, -1, -1) if reverse else range(nsub)):
            rs = slice(c * cs, (c + 1) * cs)
            b = jnp.dot(cum_m, log_gate[rs, ks], precision=HIGHEST, preferred_element_type=F32)
            b_last = b[last:last + 1]
            ref = 0.5 * b_last
            qc, kc, vc = q_all[rs, ks], k_all[rs, ks], v_all[rs, vs].astype(BF16)
            q_in = (qc * jnp.exp(jnp.minimum(b - ref, EXP_CLAMP))).astype(BF16)
            k_in = (kc * jnp.exp(jnp.minimum(ref - b, EXP_CLAMP))).astype(BF16)
            att = lax.dot_general(q_in, k_in, (((1,), (1,)), ((), ())), preferred_element_type=F32)
            att = jnp.where(keep, att, 0.0).astype(BF16)
            o = jnp.dot(att, vc, preferred_element_type=F32)
            o = o + jnp.dot((qc * jnp.exp(b)).astype(BF16), state.astype(BF16), preferred_element_type=F32)
            k_tail = (kc * jnp.exp(b_last - b)).astype(BF16)
            upd = lax.dot_general(k_tail, vc, (((0,), (0,)), ((), ())), preferred_element_type=F32)
            state = jnp.exp(b_last).reshape(GLA_HEAD_K, 1) * state + upd
            outs[c] = o
        s_ref[h] = state
        o_h = jnp.concatenate(outs, axis=0)
        if has_prev:
            o_h = o_h + prev_ref[0][:, vs]
            o_h = o_h * lax.rsqrt(jnp.mean(o_h * o_h, axis=-1, keepdims=True) + EPS) * ng_ref[:, vs]
            o_h = o_h * _silu(r_ref[0][:, vs])
        o_ref[0, :, vs] = o_h.astype(o_ref.dtype)


def _gla_direction(u_all, w2pad, gbias, n_lat, reverse, prev=None, norm_g=None):
    bsz, t, _ = u_all.shape
    n = t // ROW_TILE
    nl = n_lat // ROW_TILE
    nsub = ROW_TILE // GLA_CS
    if reverse:
        blk = lambda s: n - 1 - s
    else:
        blk = lambda s: (s + nl) % n
    q0, k0, v0, r0, g0 = (AB_SIZES[0] // GLA_DK, (AB_SIZES[0] + GLA_DK) // GLA_DK, (AB_SIZES[0] + 2 * GLA_DK) // GLA_DV,
                          (AB_SIZES[0] + 2 * GLA_DK + GLA_DV) // GLA_DV, (AB_IN - 2 * GLA_RANK) // LANE)
    d = 1 if reverse else 0
    in_specs = [pl.BlockSpec((1, ROW_TILE, GLA_DK), lambda b, s: (b, blk(s), q0)),
                pl.BlockSpec((1, ROW_TILE, GLA_DK), lambda b, s: (b, blk(s), k0)),
                pl.BlockSpec((1, ROW_TILE, GLA_DV), lambda b, s: (b, blk(s), v0)),
                pl.BlockSpec((1, ROW_TILE, LANE), lambda b, s: (b, blk(s), g0)),
                pl.BlockSpec((1, LANE, GLA_DK), lambda b, s: (d, 0, 0)),
                pl.BlockSpec((1, 1, GLA_DK), lambda b, s: (d, 0, 0))]
    args = [u_all, u_all, u_all, u_all, w2pad, gbias]
    has_prev = prev is not None
    if has_prev:
        in_specs += [pl.BlockSpec((1, ROW_TILE, GLA_DV), lambda b, s: (b, blk(s), 0)),
                     pl.BlockSpec((1, ROW_TILE, GLA_DV), lambda b, s: (b, blk(s), r0)),
                     pl.BlockSpec((1, GLA_DV), lambda b, s: (0, 0))]
        args += [prev, u_all, norm_g.reshape(1, GLA_DV)]
    return pl.pallas_call(
        functools.partial(_gla_kernel, reverse, has_prev, nsub),
        grid=(bsz, n),
        in_specs=in_specs,
        out_specs=pl.BlockSpec((1, ROW_TILE, GLA_DV), lambda b, s: (b, blk(s), 0)),
        out_shape=jax.ShapeDtypeStruct((bsz, t, GLA_DV), BF16 if has_prev else F32),
        scratch_shapes=[pltpu.VMEM((GLA_HEADS, GLA_HEAD_K, GLA_HEAD_V), F32)],
        compiler_params=_cparams("parallel", "arbitrary"),
        name="gla_bwd_norm" if reverse else "gla_fwd",
    )(*args)


def _gla(u_all, gate_w2, gate_b, norm_g, n_lat):
    w2pad = jnp.zeros((2, LANE, GLA_DK), F32)
    w2pad = w2pad.at[0, :GLA_RANK].set(gate_w2[0]).at[1, GLA_RANK:2 * GLA_RANK].set(gate_w2[1])
    gbias = gate_b.reshape(2, 1, GLA_DK).astype(F32)
    o_f = _gla_direction(u_all, w2pad, gbias, n_lat, False)
    return _gla_direction(u_all, w2pad, gbias, n_lat, True, prev=o_f, norm_g=norm_g.astype(F32))


CONV_CHUNK = 512


def _shift_rows(v, dh):
    return v if dh == 0 else pltpu.roll(v, (-dh) % v.shape[0], 0)


def _ssd_conv_kernel(n_lat, x_ref, w_ref, b_ref, o_ref, pad_ref):
    t = x_ref.shape[1]
    n_ctx = t - n_lat
    pad_ref[0:GRID_W, :] = jnp.zeros((GRID_W, LANE), F32)
    pad_ref[GRID_W + n_lat:2 * GRID_W + n_lat, :] = jnp.zeros((GRID_W, LANE), F32)
    pad_ref[GRID_W:GRID_W + n_lat, :] = x_ref[0, 0:n_lat, :]
    w = w_ref[...]
    bias = b_ref[...]
    col = lax.broadcasted_iota(jnp.int32, (CONV_CHUNK, LANE), 0) % GRID_W
    masks = {-1: col >= 1, 0: None, 1: col <= GRID_W - 2}

    def body(ci, carry):
        t0 = pl.multiple_of(ci * CONV_CHUNK, CONV_CHUNK)
        acc = jnp.zeros((CONV_CHUNK, LANE), F32) + bias
        for i in range(3):
            slab = pad_ref[pl.ds(t0 + i * GRID_W, CONV_CHUNK), :]
            for j in range(3):
                sh = _shift_rows(slab, j - 1)
                if masks[j - 1] is not None:
                    sh = jnp.where(masks[j - 1], sh, 0.0)
                acc = acc + sh * w[i * 3 + j:i * 3 + j + 1]
        o_ref[0, 0, pl.ds(t0, CONV_CHUNK), :] = _silu(acc)
        return carry

    lax.fori_loop(0, n_lat // CONV_CHUNK, body, 0)
    xc = x_ref[0, n_lat:t, :]
    pos = lax.broadcasted_iota(jnp.int32, (n_ctx, LANE), 0)
    acc = xc * w[4:5] + bias
    acc = acc + jnp.where(pos >= 1, _shift_rows(xc, -1), 0.0) * w[3:4]
    acc = acc + jnp.where(pos <= n_ctx - 2, _shift_rows(xc, 1), 0.0) * w[5:6]
    o_ref[0, 0, n_lat:t, :] = _silu(acc)


def _ssd_conv(u2, conv_w, conv_b, n_lat):
    bsz, t, _ = u2.shape
    n_tiles = SSD_CONV_DIM // LANE
    c0 = SSD_D_INNER // LANE
    return pl.pallas_call(
        functools.partial(_ssd_conv_kernel, n_lat),
        grid=(bsz, n_tiles),
        in_specs=[pl.BlockSpec((1, t, LANE), lambda b, j: (b, 0, c0 + j)),
                  pl.BlockSpec((9, LANE), lambda b, j: (0, j)),
                  pl.BlockSpec((1, LANE), lambda b, j: (0, j))],
        out_specs=pl.BlockSpec((1, 1, t, LANE), lambda b, j: (b, j, 0, 0)),
        out_shape=jax.ShapeDtypeStruct((bsz, n_tiles, t, LANE), F32),
        scratch_shapes=[pltpu.VMEM((n_lat + 2 * GRID_W, LANE), F32)],
        compiler_params=_cparams("parallel", "parallel"),
        name="ssd_conv",
    )(u2, conv_w.reshape(9, SSD_CONV_DIM), conv_b.reshape(1, SSD_CONV_DIM))


SSD_CS = 128


def _softplus(x):
    return jnp.maximum(x, 0.0) + jnp.log(1.0 + jnp.exp(-jnp.abs(x)))


def _spread_dot(v, onehot):
    hi = v.astype(BF16)
    lo = (v - hi.astype(F32)).astype(BF16)
    return jnp.dot(hi, onehot, preferred_element_type=F32) + jnp.dot(lo, onehot, preferred_element_type=F32)


def _ssd_kernel(reverse, has_prev, nsub, *refs):
    if has_prev:
        (x_ref, b_ref, c_ref, dt_ref, p_ref, expq_ref, dsk_ref, prev_ref, z_ref, ng_ref, o_ref,
         s_ref, at_ref, dtt_ref, acs_ref, eag_ref, twg_ref) = refs
    else:
        (x_ref, b_ref, c_ref, dt_ref, p_ref, expq_ref, dsk_ref, o_ref,
         s_ref, at_ref, dtt_ref, acs_ref, eag_ref, twg_ref) = refs

    @pl.when(pl.program_id(1) == 0)
    def _():
        s_ref[...] = jnp.zeros_like(s_ref)

    cs = SSD_CS
    nh = SSD_HEADS
    tpg = SSD_HPG * SSD_HEADDIM // LANE
    gw = SSD_HPG * SSD_HEADDIM
    row = lax.broadcasted_iota(jnp.int32, (cs, cs), 0)
    col = lax.broadcasted_iota(jnp.int32, (cs, cs), 1)
    keep = (col >= row) if reverse else (col <= row)
    cum_m = keep.astype(BF16)
    last = 0 if reverse else cs - 1
    lane = lax.broadcasted_iota(jnp.int32, (cs, LANE), 1)
    d0 = nh if reverse else 0
    p = p_ref[...]
    dtv = _softplus(dt_ref[0][:, d0:d0 + nh] + p[0:1, d0:d0 + nh])
    a_all = dtv * p[1:2, d0:d0 + nh]
    for c in (range(nsub - 1, -1, -1) if reverse else range(nsub)):
        rs = slice(c * cs, (c + 1) * cs)
        a = a_all[rs]
        hi = a.astype(BF16)
        r1 = a - hi.astype(F32)
        mid = r1.astype(BF16)
        lo = (r1 - mid.astype(F32)).astype(BF16)
        acs = (jnp.dot(cum_m, hi, preferred_element_type=F32) + jnp.dot(cum_m, mid, preferred_element_type=F32)
               + jnp.dot(cum_m, lo, preferred_element_type=F32))
        a_last = acs[last:last + 1]
        acs_ref[...] = acs
        e_acs = jnp.exp(acs)
        tail_w = jnp.exp(a_last - acs) * dtv[rs]
        for gg in range(SSD_GROUPS):
            eag_ref[gg] = e_acs[:, gg * SSD_HPG:(gg + 1) * SSD_HPG]
            twg_ref[gg] = tail_w[:, gg * SSD_HPG:(gg + 1) * SSD_HPG]
        at_ref[...] = acs.T
        dtt_ref[...] = dtv[rs].T

        def group(g, carry):
            h0 = pl.multiple_of(g * SSD_HPG, SSD_HPG)
            bg = b_ref[0, g, rs, :].astype(BF16)
            cg = c_ref[0, g, rs, :].astype(BF16)
            cb = lax.dot_general(cg, bg, (((1,), (1,)), ((), ())), preferred_element_type=F32)
            colb = _spread_dot(acs_ref[...], expq_ref[g])
            rows8 = at_ref[pl.ds(h0, SSD_HPG), :]
            dt8 = dtt_ref[pl.ds(h0, SSD_HPG), :]
            ms = []
            for r in range(SSD_HPG):
                diff = colb[:, r * LANE:(r + 1) * LANE] - rows8[r:r + 1, :]
                decay = jnp.where(keep, jnp.exp(jnp.minimum(diff, 0.0)), 0.0)
                ms.append((cb * decay * dt8[r:r + 1, :]).astype(BF16))
            ea_g = eag_ref[g]
            tw_g = twg_ref[g]
            state = s_ref[g]
            inter = jnp.dot(cg, state.astype(BF16), preferred_element_type=F32)
            xs = [x_ref[0, g * tpg + k, rs, :] for k in range(tpg)]
            dsk = dsk_ref[g]
            first_head = lane < SSD_HEADDIM
            ys, xws, e_last = [], [], []
            for k in range(tpg):
                bd = jnp.concatenate([jnp.where(first_head, xs[k], 0.0),
                                      jnp.where(first_head, 0.0, xs[k])], axis=0).astype(BF16)
                m2 = jnp.concatenate(ms[2 * k:2 * k + 2], axis=1)
                sl = slice(k * LANE, (k + 1) * LANE)
                e_t = jnp.where(first_head, ea_g[:, 2 * k:2 * k + 1], ea_g[:, 2 * k + 1:2 * k + 2])
                w_t = jnp.where(first_head, tw_g[:, 2 * k:2 * k + 1], tw_g[:, 2 * k + 1:2 * k + 2])
                y = jnp.dot(m2, bd, preferred_element_type=F32)
                ys.append(y + e_t * inter[:, sl] + dsk[:, sl] * xs[k])
                xws.append((xs[k] * w_t).astype(BF16))
                e_last.append(e_t[last:last + 1])
            upd = lax.dot_general(bg, jnp.concatenate(xws, axis=1), (((0,), (0,)), ((), ())),
                                  preferred_element_type=F32)
            s_ref[g] = jnp.concatenate(e_last, axis=1) * state + upd
            if has_prev:
                off = pl.multiple_of(g * gw, gw)
                z = z_ref[0, rs, pl.ds(off, gw)]
                yt = [(ys[k] + prev_ref[0, g * tpg + k, rs, :]) * _silu(z[:, k * LANE:(k + 1) * LANE])
                      for k in range(tpg)]
                ss = sum(jnp.sum(v * v, axis=-1, keepdims=True) for v in yt)
                inv = lax.rsqrt(ss * (1.0 / gw) + EPS)
                ng = ng_ref[:, pl.ds(off, gw)]
                o_ref[0, rs, pl.ds(off, gw)] = (jnp.concatenate(yt, axis=1) * inv * ng).astype(o_ref.dtype)
            else:
                for k in range(tpg):
                    o_ref[0, g * tpg + k, rs, :] = ys[k]
            return carry

        lax.fori_loop(0, SSD_GROUPS, group, 0)


def _ssd_direction(xbc_t, u2, pvec, exp_q, dsk, n_lat, reverse, prev=None, norm_g=None):
    bsz, _, t, _ = xbc_t.shape
    n = t // ROW_TILE
    nl = n_lat // ROW_TILE
    nsub = ROW_TILE // SSD_CS
    nx = SSD_D_INNER // LANE
    gw = SSD_HPG * SSD_HEADDIM
    if reverse:
        blk = lambda s: n - 1 - s
    else:
        blk = lambda s: (s + nl) % n
    d = 1 if reverse else 0
    full = lambda a: pl.BlockSpec(a.shape, lambda b, s: (0,) * a.ndim)
    in_specs = [pl.BlockSpec((1, nx, ROW_TILE, LANE), lambda b, s: (b, 0, blk(s), 0)),
                pl.BlockSpec((1, SSD_GROUPS, ROW_TILE, LANE), lambda b, s: (b, nx // SSD_GROUPS, blk(s), 0)),
                pl.BlockSpec((1, SSD_GROUPS, ROW_TILE, LANE), lambda b, s: (b, nx // SSD_GROUPS + 1, blk(s), 0)),
                pl.BlockSpec((1, ROW_TILE, LANE), lambda b, s: (b, blk(s), (SSD_IN - 2 * SSD_HEADS) // LANE)),
                full(pvec), full(exp_q),
                pl.BlockSpec((SSD_GROUPS, 1, gw), lambda b, s: (d, 0, 0))]
    args = [xbc_t, xbc_t, xbc_t, u2, pvec, exp_q, dsk.reshape(2 * SSD_GROUPS, 1, gw)]
    has_prev = prev is not None
    if has_prev:
        in_specs += [pl.BlockSpec((1, nx, ROW_TILE, LANE), lambda b, s: (b, 0, blk(s), 0)),
                     pl.BlockSpec((1, ROW_TILE, SSD_D_INNER), lambda b, s: (b, blk(s), 0)),
                     pl.BlockSpec((1, SSD_D_INNER), lambda b, s: (0, 0))]
        args += [prev, u2, norm_g.reshape(1, SSD_D_INNER)]
        out_spec = pl.BlockSpec((1, ROW_TILE, SSD_D_INNER), lambda b, s: (b, blk(s), 0))
        out_shape = jax.ShapeDtypeStruct((bsz, t, SSD_D_INNER), BF16)
    else:
        out_spec = pl.BlockSpec((1, nx, ROW_TILE, LANE), lambda b, s: (b, 0, blk(s), 0))
        out_shape = jax.ShapeDtypeStruct((bsz, nx, t, LANE), F32)
    return pl.pallas_call(
        functools.partial(_ssd_kernel, reverse, has_prev, nsub),
        grid=(bsz, n),
        in_specs=in_specs,
        out_specs=out_spec,
        out_shape=out_shape,
        scratch_shapes=[pltpu.VMEM((SSD_GROUPS, SSD_STATE, gw), F32),
                        pltpu.VMEM((SSD_HEADS, SSD_CS), F32), pltpu.VMEM((SSD_HEADS, SSD_CS), F32),
                        pltpu.VMEM((SSD_CS, SSD_HEADS), F32),
                        pltpu.VMEM((SSD_GROUPS, SSD_CS, SSD_HPG), F32),
                        pltpu.VMEM((SSD_GROUPS, SSD_CS, SSD_HPG), F32)],
        compiler_params=_cparams("parallel", "arbitrary"),
        name="ssd_bwd_norm" if reverse else "ssd_fwd",
    )(*args)


def _ssd(xbc_t, u2, dt_bias, a_log, d_skip, norm_g, n_lat):
    gw = SSD_HPG * SSD_HEADDIM
    pvec = jnp.zeros((8, LANE), F32)
    pvec = pvec.at[0].set(dt_bias.reshape(-1)).at[1].set(-jnp.exp(a_log.astype(F32)).reshape(-1))
    exp_q = np.zeros((SSD_GROUPS, SSD_HEADS, SSD_HPG * LANE), np.float32)
    for gg in range(SSD_GROUPS):
        for r in range(SSD_HPG):
            exp_q[gg, gg * SSD_HPG + r, r * LANE:(r + 1) * LANE] = 1.0
    exp_q = jnp.asarray(exp_q).astype(BF16)
    dsk = jnp.repeat(d_skip.astype(F32).reshape(2, SSD_GROUPS, SSD_HPG), SSD_HEADDIM, axis=-1)
    y_f = _ssd_direction(xbc_t, u2, pvec, exp_q, dsk, n_lat, False)
    return _ssd_direction(xbc_t, u2, pvec, exp_q, dsk, n_lat, True, prev=y_f, norm_g=norm_g.astype(F32))


HY_PAD = 8


def _hy_conv_kernel(n_lat, x_ref, w_ref, b_ref, o_ref, pad_ref):
    t = x_ref.shape[1]
    w = w_ref[...]
    bias = b_ref[...]
    ch = CONV_CHUNK
    pad_ref[0:HY_PAD, :] = jnp.zeros((HY_PAD, LANE), F32)
    pad_ref[HY_PAD + n_lat:2 * HY_PAD + n_lat, :] = jnp.zeros((HY_PAD, LANE), F32)
    pad_ref[HY_PAD:HY_PAD + n_lat, :] = x_ref[0, 0:n_lat, :]
    rowi = lax.broadcasted_iota(jnp.int32, (ch, LANE), 0)

    def body(ci, carry):
        t0 = pl.multiple_of(ci * ch, ch)
        cur = pad_ref[pl.ds(t0 + HY_PAD, ch), :]
        before = pad_ref[pl.ds(t0, HY_PAD), :][HY_PAD - 1:HY_PAD]
        after = pad_ref[pl.ds(t0 + HY_PAD + ch, HY_PAD), :][0:1]
        down = jnp.where(rowi == 0, before, _shift_rows(cur, -1))
        up = jnp.where(rowi == ch - 1, after, _shift_rows(cur, 1))
        o_ref[0, 0, pl.ds(t0, ch), :] = down * w[0:1] + cur * w[1:2] + up * w[2:3] + bias
        return carry

    lax.fori_loop(0, n_lat // ch, body, 0)
    n_ctx = t - n_lat
    xc = x_ref[0, n_lat:t, :]
    pos = lax.broadcasted_iota(jnp.int32, (n_ctx, LANE), 0)
    acc = xc * w[1:2] + bias
    acc = acc + jnp.where(pos >= 1, _shift_rows(xc, -1), 0.0) * w[0:1]
    acc = acc + jnp.where(pos <= n_ctx - 2, _shift_rows(xc, 1), 0.0) * w[2:3]
    o_ref[0, 0, n_lat:t, :] = acc


def _hy_conv(u_all, conv_w, conv_b, n_lat):
    bsz, t, _ = u_all.shape
    nch = (HY_ORDER + 1) * HY_WIDTH
    tpp = HY_WIDTH // LANE
    return pl.pallas_call(
        functools.partial(_hy_conv_kernel, n_lat),
        grid=(bsz, nch // LANE),
        in_specs=[pl.BlockSpec((1, t, LANE), lambda b, j: (b, 0, j)),
                  pl.BlockSpec((3, LANE), lambda b, j: (0, j)),
                  pl.BlockSpec((1, LANE), lambda b, j: (0, j))],
        out_specs=pl.BlockSpec((1, 1, t, LANE), lambda b, j: (j // tpp, b, 0, j % tpp)),
        out_shape=jax.ShapeDtypeStruct((HY_ORDER + 1, bsz, t, HY_WIDTH), F32),
        scratch_shapes=[pltpu.VMEM((n_lat + 2 * HY_PAD, LANE), F32)],
        compiler_params=_cparams("parallel", "parallel"),
        name="hy_short_conv",
    )(u_all, conv_w, conv_b.reshape(1, nch))


FILT_ROWS = 256


def _hy_filter_kernel(seq_len, bands_ref, w1t_ref, b1_ref, fr1_ref, w2t_ref, b2_ref, fr2_ref, w3a_ref, w3b_ref,
                      dl_ref, o_ref):
    i = pl.program_id(0)

    def lag_of(shape, axis):
        n = i * FILT_ROWS + lax.broadcasted_iota(jnp.int32, shape, axis)
        return n, jnp.where(n < seq_len, n, 2 * seq_len - n).astype(F32)

    _, pos_r = lag_of((1, FILT_ROWS), 1)
    tt_r = pos_r / max(seq_len - 1, 1)
    ang = 2.0 * math.pi * bands_ref[...] * pos_r / seq_len
    w1t = w1t_ref[...]
    nb = HY_BANDS
    hp = functools.partial(jnp.dot, precision=HIGHEST, preferred_element_type=F32)
    pre = w1t[:, 0:1] * tt_r + hp(w1t[:, 1:1 + nb], jnp.cos(ang)) - hp(w1t[:, 1 + nb:1 + 2 * nb], jnp.sin(ang))
    hid = jnp.sin(fr1_ref[...] * (pre + b1_ref[...]))
    hid = jnp.sin(fr2_ref[...] * (hp(w2t_ref[...], hid) + b2_ref[...]))
    n_c, pos_c = lag_of((FILT_ROWS, 1), 0)
    decay = jnp.exp(-(pos_c / max(seq_len - 1, 1)) * dl_ref[...])
    for o, w3_ref in enumerate((w3a_ref, w3b_ref)):
        h = lax.dot_general(hid, w3_ref[...], (((0,), (0,)), ((), ())), precision=HIGHEST,
                            preferred_element_type=F32)
        o_ref[o] = jnp.where(n_c == seq_len, 0.0, h * decay)


def _hy_filter(seq_len, f_w1, f_b1, f_fr1, f_w2, f_b2, f_fr2, f_w3):
    assert HY_ORDER == 2
    fh = f_w1.shape[1]
    emb = f_w1.shape[0]
    half = seq_len // FILT_ROWS
    bands = jnp.asarray(np.linspace(1e-4, HY_BANDS - 1, HY_BANDS, dtype=np.float32)).reshape(HY_BANDS, 1)
    deltas = jnp.asarray(np.abs(np.linspace(HY_MIN_DECAY, HY_MAX_DECAY, HY_WIDTH, dtype=np.float32))).reshape(1, HY_WIDTH)
    col = lambda v: v.reshape(fh, 1).astype(F32)
    full = lambda shape: pl.BlockSpec(shape, lambda i: (0,) * len(shape))
    w3 = f_w3.astype(F32)
    return pl.pallas_call(
        functools.partial(_hy_filter_kernel, seq_len),
        grid=(2 * half,),
        in_specs=[full((HY_BANDS, 1)), full((fh, emb)), full((fh, 1)), full((fh, 1)), full((fh, fh)), full((fh, 1)),
                  full((fh, 1)),
                  pl.BlockSpec((fh, HY_WIDTH), lambda i: (0, i // half)),
                  pl.BlockSpec((fh, HY_WIDTH), lambda i: (0, 2 + i // half)),
                  full((1, HY_WIDTH))],
        out_specs=pl.BlockSpec((HY_ORDER, FILT_ROWS, HY_WIDTH), lambda i: (0, i, 0)),
        out_shape=jax.ShapeDtypeStruct((HY_ORDER, 2 * seq_len, HY_WIDTH), F32),
        compiler_params=_cparams("parallel"),
        name="hy_filter",
    )(bands, f_w1.astype(F32).T, col(f_b1), col(f_fr1), f_w2.astype(F32).T, col(f_b2), col(f_fr2), w3, w3, deltas)


DFT_N2 = 256
DFT_S = 8


def _dft_tables(n1):
    n = n1 * DFT_N2
    k1h = n1 // 2 + 1
    k1p = -(-k1h // 8) * 8
    k1 = np.arange(k1p)[:, None].astype(np.float64)
    valid = (np.arange(k1p) < k1h)[:, None]
    th1 = 2.0 * np.pi * k1 * np.arange(n1)[None, :] / n1
    f1 = np.concatenate([np.where(valid, np.cos(th1), 0.0), np.where(valid, -np.sin(th1), 0.0)], axis=0)
    tw = 2.0 * np.pi * k1 * np.arange(DFT_N2)[None, :] / n
    tw_re = np.repeat(np.where(valid, np.cos(tw), 0.0)[:, :, None], LANE, axis=2)
    tw_im = np.repeat(np.where(valid, -np.sin(tw), 0.0)[:, :, None], LANE, axis=2)
    ph = 2.0 * np.pi * np.outer(np.arange(DFT_N2), np.arange(DFT_N2)) / DFT_N2
    f2re, f2im = np.cos(ph), -np.sin(ph)
    w_fwd = np.block([[f2re, -f2im], [f2im, f2re]])
    w_inv = np.block([[f2re, f2im], [-f2im, f2re]])
    wgt = np.where((np.arange(k1p) == 0) | (np.arange(k1p) == n1 // 2), 1.0, 2.0) * (np.arange(k1p) < k1h)
    th_i = 2.0 * np.pi * np.arange(n1 // 2)[:, None] * np.arange(k1p)[None, :] / n1
    g = np.concatenate([wgt * np.cos(th_i), -wgt * np.sin(th_i)], axis=1) / n
    f = lambda a: jnp.asarray(a.astype(np.float32))
    return dict(k1h=k1h, k1p=k1p, f1=f(f1), tw_re=f(tw_re), tw_im=f(tw_im), w_fwd=f(w_fwd), w_inv=f(w_inv), g=f(g))


def _split_bf16(v):
    hi = v.astype(BF16)
    return hi, (v - hi.astype(F32)).astype(BF16)


def _dft_dot_bf16(a, b):
    return jnp.dot(a.astype(BF16), b.astype(BF16), preferred_element_type=F32)


def _dft_dot(a, b):
    ah, al = _split_bf16(a)
    bh, bl = _split_bf16(b)
    out = jnp.dot(ah, bh, preferred_element_type=F32)
    out = out + jnp.dot(ah, bl, preferred_element_type=F32)
    return out + jnp.dot(al, bh, preferred_element_type=F32)


def _dft_first_kernel(k1p, x_ref, f1_ref, twr_ref, twi_ref, ore_ref, oim_ref):
    reps = x_ref.shape[3] // LANE
    f1 = f1_ref[...]
    for s in range(DFT_S):
        a = _dft_dot_bf16(f1, x_ref[0, :, s, :])
        tr = jnp.concatenate([twr_ref[:, s, :]] * reps, axis=1)
        ti = jnp.concatenate([twi_ref[:, s, :]] * reps, axis=1)
        are, aim = a[:k1p], a[k1p:]
        ore_ref[0, :, s, :] = are * tr - aim * ti
        oim_ref[0, :, s, :] = are * ti + aim * tr


def _dft_first(x4, lead0, nlead, f1, tab):
    rows = f1.shape[1]
    cw = x4.shape[3]
    k1p = tab['k1p']
    out = jax.ShapeDtypeStruct((nlead, k1p, DFT_N2, cw), F32)
    ospec = pl.BlockSpec((1, k1p, DFT_S, cw), lambda b, j: (b, 0, j, 0))
    tspec = pl.BlockSpec((k1p, DFT_S, LANE), lambda b, j: (0, j, 0))
    return pl.pallas_call(
        functools.partial(_dft_first_kernel, k1p),
        grid=(nlead, DFT_N2 // DFT_S),
        in_specs=[pl.BlockSpec((1, rows, DFT_S, cw), lambda b, j: (lead0 + b, 0, j, 0)),
                  pl.BlockSpec((2 * k1p, rows), lambda b, j: (0, 0)), tspec, tspec],
        out_specs=[ospec, ospec],
        out_shape=[out, out],
        compiler_params=_cparams("parallel", "parallel"),
        name="hy_dft_first",
    )(x4, f1, tab['tw_re'], tab['tw_im'])


DFT_TC = 512


def _dft_mid_kernel(k1h, fused, *refs):
    if fused:
        are_ref, aim_ref, hre_ref, him_ref, wf_ref, wi_ref, twr_ref, twi_ref, ore_ref, oim_ref = refs
    else:
        are_ref, aim_ref, wf_ref, ore_ref, oim_ref = refs
    n2 = DFT_N2

    @pl.when(pl.program_id(0) < k1h)
    def _():
        a = jnp.concatenate([are_ref[0, 0], aim_ref[0, 0]], axis=0)
        dot = _dft_dot_bf16
        x = dot(wf_ref[...], a)
        xre, xim = x[:n2], x[n2:]
        if not fused:
            ore_ref[0, 0] = xre
            oim_ref[0, 0] = xim
        else:
            hre, him = hre_ref[0, 0], him_ref[0, 0]
            y = jnp.concatenate([xre * hre - xim * him, xre * him + xim * hre], axis=0)
            bb = dot(wi_ref[...], y)
            bre, bim = bb[:n2], bb[n2:]
            twr, twi = twr_ref[0], twi_ref[0]
            for c in range(bre.shape[1] // LANE):
                sl = slice(c * LANE, (c + 1) * LANE)
                ore_ref[0, 0, :, sl] = bre[:, sl] * twr + bim[:, sl] * twi
                oim_ref[0, 0, :, sl] = bim[:, sl] * twr - bre[:, sl] * twi

    @pl.when(pl.program_id(0) >= k1h)
    def _():
        ore_ref[...] = jnp.zeros_like(ore_ref)
        oim_ref[...] = jnp.zeros_like(oim_ref)


def _dft_mid(are, aim, tab, spec=None, order=0):
    bsz, k1p, _, cw = are.shape
    tc = min(DFT_TC, cw)
    blk = pl.BlockSpec((1, 1, DFT_N2, tc), lambda k, c, b: (b, k, 0, c))
    wspec = pl.BlockSpec((2 * DFT_N2, 2 * DFT_N2), lambda k, c, b: (0, 0))
    fused = spec is not None
    if fused:
        hspec = pl.BlockSpec((1, 1, DFT_N2, tc), lambda k, c, b: (order, k, 0, c))
        tspec = pl.BlockSpec((1, DFT_N2, LANE), lambda k, c, b: (k, 0, 0))
        in_specs = [blk, blk, hspec, hspec, wspec, wspec, tspec, tspec]
        args = [are, aim, spec[0], spec[1], tab['w_fwd'].astype(BF16), tab['w_inv'].astype(BF16),
                tab['tw_re'], tab['tw_im']]
    else:
        in_specs = [blk, blk, wspec]
        args = [are, aim, tab['w_fwd']]
    out = jax.ShapeDtypeStruct((bsz, k1p, DFT_N2, cw), F32)
    ore, oim = pl.pallas_call(
        functools.partial(_dft_mid_kernel, tab['k1h'], fused),
        grid=(k1p, cw // tc, bsz),
        in_specs=in_specs,
        out_specs=[blk, blk],
        out_shape=[out, out],
        compiler_params=_cparams("parallel", "parallel", "arbitrary"),
        name="hy_dft_mid_conv" if fused else "hy_dft_mid_filter",
    )(*args)
    return ore, oim


def _dft_last_kernel(bre_ref, bim_ref, g_ref, u_ref, x_ref, d_ref, o_ref):
    g = g_ref[...]
    d = d_ref[...]
    rows = g.shape[0]
    for s in range(DFT_S):
        bb = jnp.concatenate([bre_ref[0, :, s, :], bim_ref[0, :, s, :]], axis=0)
        y = _dft_dot_bf16(g, bb)
        u = u_ref[0, :, s, :]
        o_ref[0, 0:rows, s, :] = x_ref[0, :, s, :] * (y + u * d)
    if o_ref.shape[1] > rows:
        o_ref[0, rows:, :, :] = jnp.zeros((o_ref.shape[1] - rows,) + o_ref.shape[2:], F32)


def _dft_last(bre, bim, tab, u4, u_lead0, x4, x_lead0, dvec, out_rows):
    bsz, k1p, _, cw = bre.shape
    rows = tab['g'].shape[0]
    bspec = pl.BlockSpec((1, k1p, DFT_S, cw), lambda b, j: (b, 0, j, 0))
    return pl.pallas_call(
        _dft_last_kernel,
        grid=(bsz, DFT_N2 // DFT_S),
        in_specs=[bspec, bspec, pl.BlockSpec((rows, 2 * k1p), lambda b, j: (0, 0)),
                  pl.BlockSpec((1, rows, DFT_S, cw), lambda b, j: (u_lead0 + b, 0, j, 0)),
                  pl.BlockSpec((1, rows, DFT_S, cw), lambda b, j: (x_lead0 + b, 0, j, 0)),
                  pl.BlockSpec((1, cw), lambda b, j: (0, 0))],
        out_specs=pl.BlockSpec((1, out_rows, DFT_S, cw), lambda b, j: (b, 0, j, 0)),
        out_shape=jax.ShapeDtypeStruct((bsz, out_rows, DFT_N2, cw), F32),
        compiler_params=_cparams("parallel", "parallel"),
        name="hy_dft_last",
    )(bre, bim, tab['g'], u4, x4, dvec.reshape(1, cw).astype(F32))


def _ctx_dft_tables(n):
    size = 2 * n
    kp = -(-(n + 1) // 8) * 8
    k = np.arange(kp)[:, None].astype(np.float64)
    valid = (np.arange(kp) <= n)[:, None]
    th = 2.0 * np.pi * k * np.arange(size)[None, :] / size
    fw = np.concatenate([np.where(valid, np.cos(th), 0.0), np.where(valid, -np.sin(th), 0.0)], axis=0)
    wgt = np.where((np.arange(kp) == 0) | (np.arange(kp) == n), 1.0, 2.0) * (np.arange(kp) <= n)
    thi = 2.0 * np.pi * np.arange(n)[:, None] * np.arange(kp)[None, :] / size
    inv = np.concatenate([wgt * np.cos(thi), -wgt * np.sin(thi)], axis=1) / size
    return kp, jnp.asarray(fw.astype(np.float32)), jnp.asarray(inv.astype(np.float32))


def _hy_ctx_kernel(kp, p_ref, f_ref, d_ref, fw_ref, inv_ref, y_ref, o_ref):
    del y_ref
    n = p_ref.shape[2]
    fw = fw_ref[...]
    y = p_ref[0, 0]
    for o in range(HY_ORDER):
        h = _dft_dot(fw, f_ref[o])
        x = _dft_dot(fw[:, :n], y)
        hre, him, xre, xim = h[:kp], h[kp:], x[:kp], x[kp:]
        prod = jnp.concatenate([xre * hre - xim * him, xre * him + xim * hre], axis=0)
        conv = _dft_dot(inv_ref[...], prod)
        y = p_ref[o + 1, 0] * (conv + y * d_ref[o:o + 1, :])
    o_ref[0] = y


CTX_TC = 256


def _hy_ctx(parts, filt_c, long_bias, y_all, n_lat):
    nparts, bsz, t, cw = parts.shape
    n_ctx = t - n_lat
    kp, fw, inv = _ctx_dft_tables(n_ctx)
    return pl.pallas_call(
        functools.partial(_hy_ctx_kernel, kp),
        grid=(bsz, cw // CTX_TC),
        in_specs=[pl.BlockSpec((nparts, 1, n_ctx, CTX_TC), lambda b, c: (0, b, n_lat // n_ctx, c)),
                  pl.BlockSpec((HY_ORDER, 2 * n_ctx, CTX_TC), lambda b, c: (0, 0, c)),
                  pl.BlockSpec((HY_ORDER, CTX_TC), lambda b, c: (0, c)),
                  pl.BlockSpec(fw.shape, lambda b, c: (0, 0)),
                  pl.BlockSpec(inv.shape, lambda b, c: (0, 0)),
                  pl.BlockSpec(memory_space=pl.ANY)],
        out_specs=pl.BlockSpec((1, n_ctx, CTX_TC), lambda b, c: (b, n_lat // n_ctx, c)),
        out_shape=jax.ShapeDtypeStruct(y_all.shape, F32),
        input_output_aliases={5: 0},
        compiler_params=_cparams("parallel", "parallel"),
        name="hy_ctx_conv",
    )(parts, filt_c, long_bias.astype(F32), fw, inv, y_all)


def _hyena(u_all, hy_p, n_lat):
    conv_w, conv_b, f_w1, f_b1, f_fr1, f_w2, f_b2, f_fr2, f_w3, long_bias = hy_p
    bsz, t, _ = u_all.shape
    n_ctx = t - n_lat
    assert n_ctx == DFT_N2 and n_lat % (2 * DFT_N2) == 0
    n1 = 2 * n_lat // DFT_N2
    tab = _dft_tables(n1)
    cw = HY_WIDTH
    parts = _hy_conv(u_all, conv_w, conv_b, n_lat)
    filt_l = _hy_filter(n_lat, f_w1, f_b1, f_fr1, f_w2, f_b2, f_fr2, f_w3)
    filt_c = _hy_filter(n_ctx, f_w1, f_b1, f_fr1, f_w2, f_b2, f_fr2, f_w3)
    spec = _dft_mid(*_dft_first(filt_l.reshape(HY_ORDER, n1, DFT_N2, cw), 0, HY_ORDER, tab['f1'], tab), tab)
    f1_half = tab['f1'][:, :n1 // 2]
    parts4 = parts.reshape((HY_ORDER + 1) * bsz, t // DFT_N2, DFT_N2, cw)
    y4, y_lead0 = parts4, 0
    for o in range(HY_ORDER):
        are, aim = _dft_first(y4, y_lead0, bsz, f1_half, tab)
        bre, bim = _dft_mid(are, aim, tab, spec=spec, order=o)
        last = o + 1 == HY_ORDER
        y4 = _dft_last(bre, bim, tab, y4, y_lead0, parts4, (o + 1) * bsz, long_bias[o],
                       t // DFT_N2 if last else n1 // 2)
        y_lead0 = 0
    return _hy_ctx(parts, filt_c, long_bias, y4.reshape(bsz, t, cw), n_lat)


def _row_tile(m):
    for tm in (1280, 1024, 512, 256):
        if m % tm == 0:
            return tm
    raise ValueError(m)


def _col_tile(n):
    for k in range(n // LANE, 0, -1):
        if n % (k * LANE) == 0 and k * LANE <= 1280:
            return k * LANE
    raise ValueError(n)


def _in_proj(h, w):
    bsz, t, d = h.shape
    n = w.shape[1]
    return _matmul(h.reshape(bsz * t, d), w, _row_tile(bsz * t), _col_tile(n)).reshape(bsz, t, n)


def kernel(x, c, ctx, c_ctx, router_w, router_b, mod_w, mod_b, ln_g, ln_b, exp_w_gate, exp_w_up, exp_w_down, ab_w_in, ab_w_out, hy_conv_w, hy_conv_b, hy_f_w1, hy_f_b1, hy_f_fr1, hy_f_w2, hy_f_b2, hy_f_fr2, hy_f_w3, hy_long_bias, gla_gate_w2, gla_gate_b, gla_norm_g, ssd_w_in, ssd_conv_w, ssd_conv_b, ssd_dt_bias, ssd_a_log, ssd_d, ssd_norm_g, ssd_w_out):
    bsz, n_lat, d = x.shape
    n_ctx = ctx.shape[1]
    assert bsz < 8 and n_lat % ROW_TILE == 0 and n_ctx % ROW_TILE == 0 and d == D_MODEL
    xa = jnp.concatenate([x, ctx], axis=1).astype(F32)
    c8 = jnp.zeros((8, d), F32).at[:bsz].set(c).at[bsz].set(c_ctx)
    router_w_pad = jnp.zeros((d, LANE), F32).at[:, :N_EXPERTS].set(router_w)
    mods = [_mod_vectors(c8, mod_w[i].astype(F32), mod_b[i].astype(F32)).reshape(8, 6, d) for i in range(DEPTH)]

    def rows(i, idx_l):
        lat = jnp.stack([mods[i][:bsz, k] for k in idx_l], axis=1)
        cx = jnp.broadcast_to(jnp.stack([mods[i][bsz, k] for k in idx_l], axis=0)[None], lat.shape)
        return jnp.concatenate([lat, cx], axis=1)

    def with_ln(p, g, b):
        extra = jnp.broadcast_to(jnp.stack([g, b], axis=0).astype(F32)[None], (bsz, 2, d))
        return jnp.concatenate([p, extra], axis=1)

    h = _modulate(xa, rows(0, (0, 1)), n_lat)
    for i in range(DEPTH):
        j = i // 2
        if i % 2 == 0:
            w_pad = -(-AB_IN // LANE) * LANE
            w_in = jnp.zeros((d, w_pad), BF16).at[:, :AB_IN].set(ab_w_in[j].astype(BF16))
            u_all = _in_proj(h, w_in)
            hy_p = (hy_conv_w[j], hy_conv_b[j], hy_f_w1[j], hy_f_b1[j], hy_f_fr1[j], hy_f_w2[j], hy_f_b2[j],
                    hy_f_fr2[j], hy_f_w3[j], hy_long_bias[j])
            y_hy = _hyena(u_all, hy_p, n_lat)
            y_gla = _gla(u_all, gla_gate_w2[j], gla_gate_b[j], gla_norm_g[j], n_lat)
            w_out = ab_w_out[j].astype(BF16)
            a_list, w_list = [y_hy, y_gla], [w_out[:HY_WIDTH], w_out[HY_WIDTH:]]
        else:
            u2 = _in_proj(h, ssd_w_in[j].astype(BF16))
            xbc_t = _ssd_conv(u2, ssd_conv_w[j].astype(F32), ssd_conv_b[j].astype(F32), n_lat)
            y_ssd = _ssd(xbc_t, u2, ssd_dt_bias[j].astype(F32), ssd_a_log[j], ssd_d[j].astype(F32), ssd_norm_g[j], n_lat)
            a_list, w_list = [y_ssd], [ssd_w_out[j].astype(BF16)]
        p1 = with_ln(rows(i, (2, 4, 3)), ln_g[i, 0], ln_b[i, 0])
        xa, tok, logits = _proj_update(a_list, w_list, xa, p1, router_w_pad, n_lat)
        y0, y1, gates = _moe(tok, logits, router_b, exp_w_gate.astype(F32), exp_w_up.astype(F32),
                             exp_w_down.astype(F32), i)
        gate2 = rows(i, (5,))
        if i + 1 < DEPTH:
            nxt = rows(i + 1, (1, 0))
        else:
            nxt = jnp.zeros((bsz, 4, d), F32)
        p2 = jnp.concatenate([gate2[:, 0:1], nxt[:, 0:2], gate2[:, 1:2], nxt[:, 2:4]], axis=1)
        xa, h = _combine_update(y0, y1, gates, xa, with_ln(p2, ln_g[i, 1], ln_b[i, 1]), n_lat)
    return xa[:, :n_lat].astype(x.dtype)
```

```python
import functools
import math

import numpy as np
import jax
import jax.numpy as jnp
from jax import lax
from jax.experimental import pallas as pl
from jax.experimental.pallas import tpu as pltpu

F32 = jnp.float32
BF16 = jnp.bfloat16
HIGHEST = lax.Precision.HIGHEST

D_MODEL = 2048
DEPTH = 2
GRID_W = 64
HY_WIDTH = D_MODEL // 2
HY_ORDER = 2
HY_BANDS = 16
HY_MIN_DECAY = math.log(1e-2) / 1.5
HY_MAX_DECAY = math.log(1e-2) / 0.3
GLA_HEADS = 4
GLA_DK = D_MODEL // 4
GLA_DV = D_MODEL // 2
GLA_HEAD_K = GLA_DK // GLA_HEADS
GLA_HEAD_V = GLA_DV // GLA_HEADS
GLA_RANK = 16
GLA_TAU = 16.0
AB_SIZES = ((HY_ORDER + 1) * HY_WIDTH, GLA_DK, GLA_DK, GLA_DV, GLA_DV, GLA_RANK, GLA_RANK)
AB_IN = sum(AB_SIZES)
SSD_D_INNER = 2 * D_MODEL
SSD_HEADDIM = 64
SSD_HEADS = SSD_D_INNER // SSD_HEADDIM
SSD_GROUPS = 8
SSD_HPG = SSD_HEADS // SSD_GROUPS
SSD_STATE = 128
SSD_CONV_DIM = SSD_D_INNER + 2 * SSD_GROUPS * SSD_STATE
SSD_IN = SSD_D_INNER + SSD_CONV_DIM + 2 * SSD_HEADS
N_EXPERTS = 16
N_EXPERT_GROUPS = 4
EXPERTS_PER_GROUP = N_EXPERTS // N_EXPERT_GROUPS
TOP_K = 2
D_EXPERT = D_MODEL // 2
ALPHA = (2 * DEPTH) ** 0.25
EPS = 1e-6

LANE = 128
ROW_TILE = 256
MOE_BLOCK = 256
VMEM_LIMIT = 56 * 1024 * 1024


def _cparams(*sem):
    return pltpu.CompilerParams(dimension_semantics=sem, vmem_limit_bytes=VMEM_LIMIT)


def _silu(x):
    return x * (1.0 / (1.0 + jnp.exp(-x)))


def _mod_kernel(c_ref, w_ref, b_ref, o_ref):
    o_ref[...] = jnp.dot(_silu(c_ref[...]), w_ref[...], precision=HIGHEST, preferred_element_type=F32) + b_ref[...]


def _mod_vectors(c8, w, b):
    d, n = w.shape
    tn = 1536
    return pl.pallas_call(
        _mod_kernel,
        grid=(n // tn,),
        in_specs=[pl.BlockSpec((8, d), lambda j: (0, 0)),
                  pl.BlockSpec((d, tn), lambda j: (0, j)),
                  pl.BlockSpec((1, tn), lambda j: (0, j))],
        out_specs=pl.BlockSpec((8, tn), lambda j: (0, j)),
        out_shape=jax.ShapeDtypeStruct((8, n), F32),
        compiler_params=_cparams("arbitrary"),
        name="adaln_vectors",
    )(c8, w, b.reshape(1, n))


def _modulate_kernel(n_lat_tiles, x_ref, p_ref, o_ref):
    is_ctx = pl.program_id(1) >= n_lat_tiles
    p = p_ref[0]
    shift = jnp.where(is_ctx, p[2:3], p[0:1])
    scale = jnp.where(is_ctx, p[3:4], p[1:2])
    o_ref[0] = (x_ref[0] * (1.0 + scale) + shift).astype(o_ref.dtype)


def _modulate(x, params, n_lat):
    bsz, t, d = x.shape
    return pl.pallas_call(
        functools.partial(_modulate_kernel, n_lat // ROW_TILE),
        grid=(bsz, t // ROW_TILE),
        in_specs=[pl.BlockSpec((1, ROW_TILE, d), lambda b, i: (b, i, 0)),
                  pl.BlockSpec((1, 4, d), lambda b, i: (b, 0, 0))],
        out_specs=pl.BlockSpec((1, ROW_TILE, d), lambda b, i: (b, i, 0)),
        out_shape=jax.ShapeDtypeStruct((bsz, t, d), BF16),
        compiler_params=_cparams("parallel", "parallel"),
        name="modulate",
    )(x, params)


def _matmul_kernel(a_ref, w_ref, o_ref):
    o_ref[...] = jnp.dot(a_ref[...], w_ref[...], preferred_element_type=F32).astype(o_ref.dtype)


def _matmul(a, w, tm, tn, out_dtype=F32):
    m, k = a.shape
    n = w.shape[1]
    return pl.pallas_call(
        _matmul_kernel,
        grid=(m // tm, n // tn),
        in_specs=[pl.BlockSpec((tm, k), lambda i, j: (i, 0)),
                  pl.BlockSpec((k, tn), lambda i, j: (0, j))],
        out_specs=pl.BlockSpec((tm, tn), lambda i, j: (i, j)),
        out_shape=jax.ShapeDtypeStruct((m, n), out_dtype),
        compiler_params=_cparams("parallel", "arbitrary"),
        name="in_proj",
    )(a, w)


def _layer_norm_rows(v, g, b):
    mu = jnp.mean(v, axis=-1, keepdims=True)
    vc = v - mu
    var = jnp.mean(vc * vc, axis=-1, keepdims=True)
    return vc * lax.rsqrt(var + EPS) * g + b


def _param_table(p):
    bsz, _, d = p.shape
    pad = jnp.zeros((bsz, 3, d), F32)
    lat = jnp.concatenate([p[:, 0:3], p[:, 6:8], pad], axis=1)
    cx = jnp.concatenate([p[:, 3:6], p[:, 6:8], pad], axis=1)
    return jnp.stack([lat, cx], axis=1).reshape(2 * bsz, 8, d)


def _post_update(tiles_per_batch, n_lat_tiles, tile, y, x, p_ref):
    b = tile // tiles_per_batch
    is_ctx = (tile - b * tiles_per_batch >= n_lat_tiles).astype(jnp.int32)
    p = p_ref[2 * b + is_ctx]
    xn = _layer_norm_rows(ALPHA * x + p[0:1] * y, p[3:4], p[4:5])
    return xn, xn * (1.0 + p[1:2]) + p[2:3]


PROJ_HALVES = 2


def _proj_update_kernel(tiles_per_batch, n_lat_tiles, n_a, *refs):
    a_refs = refs[:n_a]
    w_refs = refs[n_a:2 * n_a]
    x_ref, p_ref, rw_ref, xo_ref, tok_ref, lg_ref = refs[2 * n_a:]
    for hh in range(PROJ_HALVES):
        rs = slice(hh * ROW_TILE, (hh + 1) * ROW_TILE)
        y = jnp.dot(a_refs[0][rs, :].astype(BF16), w_refs[0][...], preferred_element_type=F32)
        for a_ref, w_ref in zip(a_refs[1:], w_refs[1:]):
            y = y + jnp.dot(a_ref[rs, :].astype(BF16), w_ref[...], preferred_element_type=F32)
        tile = pl.program_id(0) * PROJ_HALVES + hh
        xn, tok = _post_update(tiles_per_batch, n_lat_tiles, tile, y, x_ref[rs, :], p_ref)
        xo_ref[rs, :] = xn
        tok_ref[rs, :] = tok.astype(tok_ref.dtype)
        lg_ref[rs, :] = jnp.dot(tok, rw_ref[...], precision=HIGHEST, preferred_element_type=F32)


def _proj_update(a_list, w_list, x, params, router_w_pad, n_lat):
    bsz, t, d = x.shape
    n = bsz * t
    tm = PROJ_HALVES * ROW_TILE
    assert n % tm == 0
    n_a = len(a_list)
    flat = lambda v: v.reshape(n, v.shape[-1])
    in_specs = [pl.BlockSpec((tm, a.shape[-1]), lambda i: (i, 0)) for a in a_list]
    in_specs += [pl.BlockSpec(w.shape, lambda i: (0, 0)) for w in w_list]
    in_specs += [pl.BlockSpec((tm, d), lambda i: (i, 0)),
                 pl.BlockSpec((2 * bsz, 8, d), lambda i: (0, 0, 0)),
                 pl.BlockSpec((d, LANE), lambda i: (0, 0))]
    xo, tok, lg = pl.pallas_call(
        functools.partial(_proj_update_kernel, t // ROW_TILE, n_lat // ROW_TILE, n_a),
        grid=(n // tm,),
        in_specs=in_specs,
        out_specs=[pl.BlockSpec((tm, d), lambda i: (i, 0)),
                   pl.BlockSpec((tm, d), lambda i: (i, 0)),
                   pl.BlockSpec((tm, LANE), lambda i: (i, 0))],
        out_shape=[jax.ShapeDtypeStruct((n, d), F32),
                   jax.ShapeDtypeStruct((n, d), BF16),
                   jax.ShapeDtypeStruct((n, LANE), F32)],
        compiler_params=_cparams("parallel"),
        name="out_proj_ln",
    )(*[flat(a) for a in a_list], *w_list, flat(x), _param_table(params), router_w_pad)
    return xo.reshape(bsz, t, d), tok.reshape(bsz, t, d), lg.reshape(bsz, t, LANE)


def _route_kernel(lg_ref, rb_ref, tri_ref, oi_ref, og_ref, cnt_ref, carry_ref):
    @pl.when(pl.program_id(0) == 0)
    def _():
        carry_ref[...] = jnp.zeros_like(carry_ref)

    tm = lg_ref.shape[0]
    epg = EXPERTS_PER_GROUP
    lt = lg_ref[...].T[:N_EXPERTS]
    score = 1.0 / (1.0 + jnp.exp(-lt))
    sel = score + rb_ref[...]
    best_g = None
    for q in range(N_EXPERT_GROUPS):
        rows = [sel[q * epg + r:q * epg + r + 1] for r in range(epg)]
        gs = None
        for a in range(epg):
            for b in range(a + 1, epg):
                ps = rows[a] + rows[b]
                gs = ps if gs is None else jnp.maximum(gs, ps)
        if best_g is None:
            best_g, grp = gs, jnp.zeros_like(gs, dtype=jnp.int32)
        else:
            better = gs > best_g
            grp = jnp.where(better, q, grp)
            best_g = jnp.where(better, gs, best_g)
    in_sel, in_score = [], []
    for r in range(epg):
        v = sel[r:r + 1]
        s = score[r:r + 1]
        for q in range(1, N_EXPERT_GROUPS):
            v = jnp.where(grp == q, sel[q * epg + r:q * epg + r + 1], v)
            s = jnp.where(grp == q, score[q * epg + r:q * epg + r + 1], s)
        in_sel.append(v)
        in_score.append(s)
    i1, v1 = jnp.zeros_like(grp), in_sel[0]
    for r in range(1, epg):
        better = in_sel[r] > v1
        i1 = jnp.where(better, r, i1)
        v1 = jnp.where(better, in_sel[r], v1)
    i2, v2 = None, None
    for r in range(epg):
        cand = jnp.where(i1 == r, -jnp.inf, in_sel[r])
        if i2 is None:
            i2, v2 = jnp.zeros_like(grp), cand
        else:
            better = cand > v2
            i2 = jnp.where(better, r, i2)
            v2 = jnp.where(better, cand, v2)
    s1, s2 = in_score[0], in_score[0]
    for r in range(1, epg):
        s1 = jnp.where(i1 == r, in_score[r], s1)
        s2 = jnp.where(i2 == r, in_score[r], s2)
    e1 = grp * epg + i1
    e2 = grp * epg + i2
    tot = s1 + s2
    g1, g2 = s1 / tot, s2 / tot
    eid = lax.broadcasted_iota(jnp.int32, (N_EXPERTS, tm), 0)
    oh1 = eid == e1
    oh2 = eid == e2
    both = jnp.where(oh1 | oh2, 1.0, 0.0)
    before = jnp.dot(both.astype(BF16), tri_ref[...], preferred_element_type=F32) + carry_ref[:, 0:1]
    r1 = jnp.sum(jnp.where(oh1, before, 0.0), axis=0, keepdims=True)
    r2 = jnp.sum(jnp.where(oh2, before, 0.0), axis=0, keepdims=True)
    new_carry = carry_ref[...] + jnp.sum(both, axis=1, keepdims=True)
    carry_ref[...] = new_carry
    cnt_ref[...] = new_carry.astype(jnp.int32)
    zi = jnp.zeros((4, tm), jnp.int32)
    oi_ref[...] = jnp.concatenate([e1, e2, r1.astype(jnp.int32), r2.astype(jnp.int32), zi], axis=0)
    gt = jnp.concatenate([g1, g2, jnp.zeros((LANE - 2, tm), F32)], axis=0)
    og_ref[...] = gt.T


def _route(logits, router_b):
    n = logits.shape[0]
    tm = ROW_TILE
    tri = jnp.asarray(np.triu(np.ones((tm, tm), np.float32), 1)).astype(BF16)
    oi, og, cnt = pl.pallas_call(
        _route_kernel,
        grid=(n // tm,),
        in_specs=[pl.BlockSpec((tm, LANE), lambda i: (i, 0)),
                  pl.BlockSpec((N_EXPERTS, 1), lambda i: (0, 0)),
                  pl.BlockSpec((tm, tm), lambda i: (0, 0))],
        out_specs=[pl.BlockSpec((8, tm), lambda i: (0, i)),
                   pl.BlockSpec((tm, LANE), lambda i: (i, 0)),
                   pl.BlockSpec((N_EXPERTS, LANE), lambda i: (0, 0))],
        out_shape=[jax.ShapeDtypeStruct((8, n), jnp.int32),
                   jax.ShapeDtypeStruct((n, LANE), F32),
                   jax.ShapeDtypeStruct((N_EXPERTS, LANE), jnp.int32)],
        scratch_shapes=[pltpu.VMEM((N_EXPERTS, LANE), F32)],
        compiler_params=_cparams("arbitrary"),
        name="moe_route",
    )(logits, router_b.reshape(N_EXPERTS, 1).astype(F32), tri)
    return oi, og, cnt[:, 0]


W_CONV_ROWS = 256


def _expert_kernel(layer, be_ref, nb_ref, first_ref, nxt_ref, x_ref, wg_hbm, wu_hbm, wd_hbm, o_ref,
                   sg_ref, su_ref, sd_ref, wg_ref, wu_ref, wd_ref, sem):
    i = pl.program_id(0)

    def copies(e):
        return (pltpu.make_async_copy(wg_hbm.at[layer, e], sg_ref, sem.at[0]),
                pltpu.make_async_copy(wu_hbm.at[layer, e], su_ref, sem.at[1]),
                pltpu.make_async_copy(wd_hbm.at[layer, e], sd_ref, sem.at[2]))

    def convert(src, dst):
        def body(r, carry):
            r0 = pl.multiple_of(r * W_CONV_ROWS, W_CONV_ROWS)
            dst[pl.ds(r0, W_CONV_ROWS), :] = src[pl.ds(r0, W_CONV_ROWS), :].astype(BF16)
            return carry
        lax.fori_loop(0, src.shape[0] // W_CONV_ROWS, body, 0)

    active = i < nb_ref[0]

    @pl.when(active & (i == 0))
    def _():
        for cp in copies(be_ref[0]):
            cp.start()

    @pl.when(active & (first_ref[i] == 1))
    def _():
        for cp in copies(be_ref[i]):
            cp.wait()
        convert(sg_ref, wg_ref)
        convert(su_ref, wu_ref)
        convert(sd_ref, wd_ref)

        @pl.when(nxt_ref[i] >= 0)
        def _():
            for cp in copies(nxt_ref[i]):
                cp.start()

    @pl.when(active)
    def _():
        x = x_ref[...]
        hg = jnp.dot(x, wg_ref[...], preferred_element_type=F32)
        hu = jnp.dot(x, wu_ref[...], preferred_element_type=F32)
        h = (_silu(hg) * hu).astype(BF16)
        o_ref[...] = jnp.dot(h, wd_ref[...], preferred_element_type=F32).astype(o_ref.dtype)

    @pl.when(jnp.logical_not(active))
    def _():
        o_ref[...] = jnp.zeros_like(o_ref)


def _expert_blocks(buf, block_expert, n_used, first, nxt, wg, wu, wd, layer):
    rows, d = buf.shape
    n_blocks = rows // MOE_BLOCK
    de = wg.shape[-1]
    grid_spec = pltpu.PrefetchScalarGridSpec(
        num_scalar_prefetch=4,
        grid=(n_blocks,),
        in_specs=[pl.BlockSpec((MOE_BLOCK, d), lambda i, *_: (i, 0)),
                  pl.BlockSpec(memory_space=pl.ANY),
                  pl.BlockSpec(memory_space=pl.ANY),
                  pl.BlockSpec(memory_space=pl.ANY)],
        out_specs=pl.BlockSpec((MOE_BLOCK, d), lambda i, *_: (i, 0)),
        scratch_shapes=[pltpu.VMEM((d, de), F32), pltpu.VMEM((d, de), F32), pltpu.VMEM((de, d), F32),
                        pltpu.VMEM((d, de), BF16), pltpu.VMEM((d, de), BF16), pltpu.VMEM((de, d), BF16),
                        pltpu.SemaphoreType.DMA((3,))],
    )
    return pl.pallas_call(
        functools.partial(_expert_kernel, layer),
        grid_spec=grid_spec,
        out_shape=jax.ShapeDtypeStruct((rows, d), BF16),
        compiler_params=_cparams("arbitrary"),
        name="moe_experts",
    )(block_expert, n_used, first, nxt, buf, wg, wu, wd)


def _combine_update_kernel(tiles_per_batch, n_lat_tiles, last, y0_ref, y1_ref, g_ref, x_ref, p_ref, xo_ref, *h_ref):
    i = pl.program_id(0)
    tile = (i // n_lat_tiles) * tiles_per_batch + i % n_lat_tiles if last else i
    g = g_ref[...]
    y = y0_ref[...].astype(F32) * g[:, 0:1] + y1_ref[...].astype(F32) * g[:, 1:2]
    xn, h = _post_update(tiles_per_batch, n_lat_tiles, tile, y, x_ref[...], p_ref)
    xo_ref[...] = xn
    if not last:
        h_ref[0][...] = h.astype(h_ref[0].dtype)


def _combine_update(y0, y1, gates, x, params, n_lat, last):
    bsz, t, d = x.shape
    n = bsz * t
    tpb, nl = t // ROW_TILE, n_lat // ROW_TILE
    if last:
        src = lambda i: ((i // nl) * tpb + i % nl, 0)
        steps, out_rows = bsz * nl, bsz * n_lat
    else:
        src = lambda i: (i, 0)
        steps, out_rows = n // ROW_TILE, n
    row_in = pl.BlockSpec((ROW_TILE, d), src)
    row_out = pl.BlockSpec((ROW_TILE, d), lambda i: (i, 0))
    out_shape = [jax.ShapeDtypeStruct((out_rows, d), F32)] + ([] if last else [jax.ShapeDtypeStruct((n, d), BF16)])
    outs = pl.pallas_call(
        functools.partial(_combine_update_kernel, tpb, nl, last),
        grid=(steps,),
        in_specs=[row_in, row_in, pl.BlockSpec((ROW_TILE, LANE), src), row_in,
                  pl.BlockSpec((2 * bsz, 8, d), lambda i: (0, 0, 0))],
        out_specs=[row_out] * len(out_shape),
        out_shape=out_shape,
        compiler_params=_cparams("parallel"),
        name="moe_combine_ln",
    )(y0, y1, gates, x.reshape(n, d), _param_table(params))
    if last:
        return outs[0].reshape(bsz, n_lat, d), None
    return outs[0].reshape(bsz, t, d), outs[1].reshape(bsz, t, d)


def _moe(tok, logits, router_b, wg, wu, wd, layer):
    bsz, t, d = tok.shape
    n = bsz * t
    tk = n * TOP_K
    oi, gates, counts = _route(logits.reshape(n, LANE), router_b)
    experts, ranks = oi[0:2], oi[2:4]
    padded = (counts + MOE_BLOCK - 1) // MOE_BLOCK * MOE_BLOCK
    pends = jnp.cumsum(padded)
    pstarts = pends - padded
    eids = jnp.arange(N_EXPERTS, dtype=jnp.int32)
    table = lambda idx, tab: jnp.sum(jnp.where(idx[..., None] == eids, tab.astype(jnp.int32), 0), axis=-1)
    dest = table(experts, pstarts) + ranks
    n_blocks = -(-tk // MOE_BLOCK) + N_EXPERTS
    blk_start = jnp.arange(n_blocks, dtype=jnp.int32) * MOE_BLOCK
    block_expert = jnp.minimum(jnp.sum(pends[None, :] <= blk_start[:, None], axis=1), N_EXPERTS - 1).astype(jnp.int32)
    n_used = (pends[-1] // MOE_BLOCK).astype(jnp.int32).reshape(1)
    prev_e = jnp.concatenate([jnp.full((1,), -1, jnp.int32), block_expert[:-1]])
    first = (block_expert != prev_e).astype(jnp.int32)
    later = jnp.where((counts[None, :] > 0) & (jnp.arange(N_EXPERTS)[None, :] > jnp.arange(N_EXPERTS)[:, None]),
                      jnp.arange(N_EXPERTS)[None, :], N_EXPERTS)
    nxt_e = jnp.min(later, axis=1)
    nxt_e = jnp.where(nxt_e >= N_EXPERTS, -1, nxt_e).astype(jnp.int32)
    nxt = table(block_expert, nxt_e)
    tok_id = jnp.broadcast_to(jnp.arange(n, dtype=jnp.int32)[None], (TOP_K, n))
    src = jnp.zeros((n_blocks * MOE_BLOCK,), jnp.int32).at[dest.reshape(-1)].set(tok_id.reshape(-1))
    buf = tok.reshape(n, d)[src]
    out = _expert_blocks(buf, block_expert, n_used, first, nxt, wg, wu, wd, layer)
    return out[dest[0]], out[dest[1]], gates


GLA_CS = 64
EXP_CLAMP = 80.0


def _gla_kernel(reverse, has_prev, nsub, *refs):
    if has_prev:
        (q_ref, k_ref, v_ref, g1_ref, w2_ref, gb_ref, prev_ref, r_ref, ng_ref, o_ref, s_ref) = refs
    else:
        (q_ref, k_ref, v_ref, g1_ref, w2_ref, gb_ref, o_ref, s_ref) = refs

    @pl.when(pl.program_id(1) == 0)
    def _():
        s_ref[...] = jnp.zeros_like(s_ref)

    cs = GLA_CS
    row = lax.broadcasted_iota(jnp.int32, (cs, cs), 0)
    col = lax.broadcasted_iota(jnp.int32, (cs, cs), 1)
    keep = (col >= row) if reverse else (col <= row)
    cum_m = keep.astype(BF16)
    z = _dft_dot(g1_ref[0], w2_ref[0]) + gb_ref[0]
    log_gate = (jnp.minimum(z, 0.0) - jnp.log(1.0 + jnp.exp(-jnp.abs(z)))) * (1.0 / GLA_TAU)
    q_all = q_ref[0] * (GLA_HEAD_K ** -0.5)
    k_all = k_ref[0]
    v_all = v_ref[0]
    last = 0 if reverse else cs - 1
    for h in range(GLA_HEADS):
        ks = slice(h * GLA_HEAD_K, (h + 1) * GLA_HEAD_K)
        vs = slice(h * GLA_HEAD_V, (h + 1) * GLA_HEAD_V)
        state = s_ref[h]
        outs = [None] * nsub
        for c in (range(nsub - 1, -1, -1) if reverse else range(nsub)):
            rs = slice(c * cs, (c + 1) * cs)
            lg_hi, lg_lo = _split_bf16(log_gate[rs, ks])
            b = jnp.dot(cum_m, lg_hi, preferred_element_type=F32) + jnp.dot(cum_m, lg_lo, preferred_element_type=F32)
            b_last = b[last:last + 1]
            ref = 0.5 * b_last
            qc, kc, vc = q_all[rs, ks], k_all[rs, ks], v_all[rs, vs].astype(BF16)
            q_in = (qc * jnp.exp(jnp.minimum(b - ref, EXP_CLAMP))).astype(BF16)
            k_in = (kc * jnp.exp(jnp.minimum(ref - b, EXP_CLAMP))).astype(BF16)
            att = lax.dot_general(q_in, k_in, (((1,), (1,)), ((), ())), preferred_element_type=F32)
            att = jnp.where(keep, att, 0.0).astype(BF16)
            o = jnp.dot(att, vc, preferred_element_type=F32)
            o = o + jnp.dot((qc * jnp.exp(b)).astype(BF16), state.astype(BF16), preferred_element_type=F32)
            k_tail = (kc * jnp.exp(b_last - b)).astype(BF16)
            upd = lax.dot_general(k_tail, vc, (((0,), (0,)), ((), ())), preferred_element_type=F32)
            state = jnp.exp(b_last).reshape(GLA_HEAD_K, 1) * state + upd
            outs[c] = o
        s_ref[h] = state
        o_h = jnp.concatenate(outs, axis=0)
        if has_prev:
            o_h = o_h + prev_ref[0][:, vs]
            o_h = o_h * lax.rsqrt(jnp.mean(o_h * o_h, axis=-1, keepdims=True) + EPS) * ng_ref[:, vs]
            o_h = o_h * _silu(r_ref[0][:, vs])
        o_ref[0, :, vs] = o_h.astype(o_ref.dtype)


def _gla_direction(u_all, w2pad, gbias, n_lat, reverse, prev=None, norm_g=None):
    bsz, t, _ = u_all.shape
    n = t // ROW_TILE
    nl = n_lat // ROW_TILE
    nsub = ROW_TILE // GLA_CS
    if reverse:
        blk = lambda s: n - 1 - s
    else:
        blk = lambda s: (s + nl) % n
    q0, k0, v0, r0, g0 = (AB_SIZES[0] // GLA_DK, (AB_SIZES[0] + GLA_DK) // GLA_DK, (AB_SIZES[0] + 2 * GLA_DK) // GLA_DV,
                          (AB_SIZES[0] + 2 * GLA_DK + GLA_DV) // GLA_DV, (AB_IN - 2 * GLA_RANK) // LANE)
    d = 1 if reverse else 0
    in_specs = [pl.BlockSpec((1, ROW_TILE, GLA_DK), lambda b, s: (b, blk(s), q0)),
                pl.BlockSpec((1, ROW_TILE, GLA_DK), lambda b, s: (b, blk(s), k0)),
                pl.BlockSpec((1, ROW_TILE, GLA_DV), lambda b, s: (b, blk(s), v0)),
                pl.BlockSpec((1, ROW_TILE, LANE), lambda b, s: (b, blk(s), g0)),
                pl.BlockSpec((1, LANE, GLA_DK), lambda b, s: (d, 0, 0)),
                pl.BlockSpec((1, 1, GLA_DK), lambda b, s: (d, 0, 0))]
    args = [u_all, u_all, u_all, u_all, w2pad, gbias]
    has_prev = prev is not None
    if has_prev:
        in_specs += [pl.BlockSpec((1, ROW_TILE, GLA_DV), lambda b, s: (b, blk(s), 0)),
                     pl.BlockSpec((1, ROW_TILE, GLA_DV), lambda b, s: (b, blk(s), r0)),
                     pl.BlockSpec((1, GLA_DV), lambda b, s: (0, 0))]
        args += [prev, u_all, norm_g.reshape(1, GLA_DV)]
    return pl.pallas_call(
        functools.partial(_gla_kernel, reverse, has_prev, nsub),
        grid=(bsz, n),
        in_specs=in_specs,
        out_specs=pl.BlockSpec((1, ROW_TILE, GLA_DV), lambda b, s: (b, blk(s), 0)),
        out_shape=jax.ShapeDtypeStruct((bsz, t, GLA_DV), BF16 if has_prev else F32),
        scratch_shapes=[pltpu.VMEM((GLA_HEADS, GLA_HEAD_K, GLA_HEAD_V), F32)],
        compiler_params=_cparams("parallel", "arbitrary"),
        name="gla_bwd_norm" if reverse else "gla_fwd",
    )(*args)


def _gla(u_all, gate_w2, gate_b, norm_g, n_lat):
    w2pad = jnp.zeros((2, LANE, GLA_DK), F32)
    w2pad = w2pad.at[0, :GLA_RANK].set(gate_w2[0]).at[1, GLA_RANK:2 * GLA_RANK].set(gate_w2[1])
    gbias = gate_b.reshape(2, 1, GLA_DK).astype(F32)
    o_f = _gla_direction(u_all, w2pad, gbias, n_lat, False)
    return _gla_direction(u_all, w2pad, gbias, n_lat, True, prev=o_f, norm_g=norm_g.astype(F32))


CONV_CHUNK = 512


def _shift_rows(v, dh):
    return v if dh == 0 else pltpu.roll(v, (-dh) % v.shape[0], 0)


def _ssd_conv_kernel(n_lat, x_ref, w_ref, b_ref, o_ref, pad_ref):
    t = x_ref.shape[1]
    n_ctx = t - n_lat
    pad_ref[0:GRID_W, :] = jnp.zeros((GRID_W, LANE), F32)
    pad_ref[GRID_W + n_lat:2 * GRID_W + n_lat, :] = jnp.zeros((GRID_W, LANE), F32)
    pad_ref[GRID_W:GRID_W + n_lat, :] = x_ref[0, 0:n_lat, :]
    w = w_ref[...]
    bias = b_ref[...]
    col = lax.broadcasted_iota(jnp.int32, (CONV_CHUNK, LANE), 0) % GRID_W
    masks = {-1: col >= 1, 0: None, 1: col <= GRID_W - 2}

    def body(ci, carry):
        t0 = pl.multiple_of(ci * CONV_CHUNK, CONV_CHUNK)
        slabs = [pad_ref[pl.ds(t0 + i * GRID_W, CONV_CHUNK), :] for i in range(3)]
        part = [sum(slabs[i] * w[i * 3 + j:i * 3 + j + 1] for i in range(3)) for j in range(3)]
        acc = part[1] + bias
        acc = acc + jnp.where(masks[-1], _shift_rows(part[0], -1), 0.0)
        acc = acc + jnp.where(masks[1], _shift_rows(part[2], 1), 0.0)
        o_ref[0, 0, pl.ds(t0, CONV_CHUNK), :] = _silu(acc)
        return carry

    lax.fori_loop(0, n_lat // CONV_CHUNK, body, 0)
    xc = x_ref[0, n_lat:t, :]
    pos = lax.broadcasted_iota(jnp.int32, (n_ctx, LANE), 0)
    acc = xc * w[4:5] + bias
    acc = acc + jnp.where(pos >= 1, _shift_rows(xc, -1), 0.0) * w[3:4]
    acc = acc + jnp.where(pos <= n_ctx - 2, _shift_rows(xc, 1), 0.0) * w[5:6]
    o_ref[0, 0, n_lat:t, :] = _silu(acc)


def _ssd_conv(u2, conv_w, conv_b, n_lat):
    bsz, t, _ = u2.shape
    n_tiles = SSD_CONV_DIM // LANE
    c0 = SSD_D_INNER // LANE
    return pl.pallas_call(
        functools.partial(_ssd_conv_kernel, n_lat),
        grid=(bsz, n_tiles),
        in_specs=[pl.BlockSpec((1, t, LANE), lambda b, j: (b, 0, c0 + j)),
                  pl.BlockSpec((9, LANE), lambda b, j: (0, j)),
                  pl.BlockSpec((1, LANE), lambda b, j: (0, j))],
        out_specs=pl.BlockSpec((1, 1, t, LANE), lambda b, j: (b, j, 0, 0)),
        out_shape=jax.ShapeDtypeStruct((bsz, n_tiles, t, LANE), F32),
        scratch_shapes=[pltpu.VMEM((n_lat + 2 * GRID_W, LANE), F32)],
        compiler_params=_cparams("parallel", "parallel"),
        name="ssd_conv",
    )(u2, conv_w.reshape(9, SSD_CONV_DIM), conv_b.reshape(1, SSD_CONV_DIM))


SSD_CS = 128


def _softplus(x):
    return jnp.maximum(x, 0.0) + jnp.log(1.0 + jnp.exp(-jnp.abs(x)))


def _spread_dot(v, onehot):
    hi = v.astype(BF16)
    lo = (v - hi.astype(F32)).astype(BF16)
    return jnp.dot(hi, onehot, preferred_element_type=F32) + jnp.dot(lo, onehot, preferred_element_type=F32)


def _ssd_kernel(reverse, has_prev, nsub, *refs):
    if has_prev:
        (x_ref, b_ref, c_ref, dt_ref, p_ref, expq_ref, dsk_ref, prev_ref, z_ref, ng_ref, o_ref,
         s_ref, at_ref, dtt_ref, acs_ref, eag_ref, twg_ref) = refs
    else:
        (x_ref, b_ref, c_ref, dt_ref, p_ref, expq_ref, dsk_ref, o_ref,
         s_ref, at_ref, dtt_ref, acs_ref, eag_ref, twg_ref) = refs

    @pl.when(pl.program_id(1) == 0)
    def _():
        s_ref[...] = jnp.zeros_like(s_ref)

    cs = SSD_CS
    nh = SSD_HEADS
    tpg = SSD_HPG * SSD_HEADDIM // LANE
    gw = SSD_HPG * SSD_HEADDIM
    row = lax.broadcasted_iota(jnp.int32, (cs, cs), 0)
    col = lax.broadcasted_iota(jnp.int32, (cs, cs), 1)
    keep = (col >= row) if reverse else (col <= row)
    cum_m = keep.astype(BF16)
    last = 0 if reverse else cs - 1
    lane = lax.broadcasted_iota(jnp.int32, (cs, LANE), 1)
    d0 = nh if reverse else 0
    p = p_ref[...]
    dtv = _softplus(dt_ref[0][:, d0:d0 + nh] + p[0:1, d0:d0 + nh])
    a_all = dtv * p[1:2, d0:d0 + nh]
    for c in (range(nsub - 1, -1, -1) if reverse else range(nsub)):
        rs = slice(c * cs, (c + 1) * cs)
        a = a_all[rs]
        hi = a.astype(BF16)
        r1 = a - hi.astype(F32)
        mid = r1.astype(BF16)
        lo = (r1 - mid.astype(F32)).astype(BF16)
        acs = (jnp.dot(cum_m, hi, preferred_element_type=F32) + jnp.dot(cum_m, mid, preferred_element_type=F32)
               + jnp.dot(cum_m, lo, preferred_element_type=F32))
        a_last = acs[last:last + 1]
        acs_ref[...] = acs
        e_acs = jnp.exp(acs)
        tail_w = jnp.exp(a_last - acs) * dtv[rs]
        for gg in range(SSD_GROUPS):
            eag_ref[gg] = e_acs[:, gg * SSD_HPG:(gg + 1) * SSD_HPG]
            twg_ref[gg] = tail_w[:, gg * SSD_HPG:(gg + 1) * SSD_HPG]
        at_ref[...] = acs.T
        dtt_ref[...] = dtv[rs].T

        def group(g, carry):
            h0 = pl.multiple_of(g * SSD_HPG, SSD_HPG)
            bg = b_ref[0, g, rs, :].astype(BF16)
            cg = c_ref[0, g, rs, :].astype(BF16)
            cb = lax.dot_general(cg, bg, (((1,), (1,)), ((), ())), preferred_element_type=F32)
            colb = _spread_dot(acs_ref[...], expq_ref[g])
            rows8 = at_ref[pl.ds(h0, SSD_HPG), :]
            dt8 = dtt_ref[pl.ds(h0, SSD_HPG), :]
            ms = []
            for r in range(SSD_HPG):
                diff = colb[:, r * LANE:(r + 1) * LANE] - rows8[r:r + 1, :]
                decay = jnp.where(keep, jnp.exp(jnp.minimum(diff, 0.0)), 0.0)
                ms.append((cb * decay * dt8[r:r + 1, :]).astype(BF16))
            ea_g = eag_ref[g]
            tw_g = twg_ref[g]
            state = s_ref[g]
            inter = jnp.dot(cg, state.astype(BF16), preferred_element_type=F32)
            xs = [x_ref[0, g * tpg + k, rs, :] for k in range(tpg)]
            dsk = dsk_ref[g]
            first_head = lane < SSD_HEADDIM
            ys, xws, e_last = [], [], []
            for k in range(tpg):
                bd = jnp.concatenate([jnp.where(first_head, xs[k], 0.0),
                                      jnp.where(first_head, 0.0, xs[k])], axis=0).astype(BF16)
                m2 = jnp.concatenate(ms[2 * k:2 * k + 2], axis=1)
                sl = slice(k * LANE, (k + 1) * LANE)
                e_t = jnp.where(first_head, ea_g[:, 2 * k:2 * k + 1], ea_g[:, 2 * k + 1:2 * k + 2])
                w_t = jnp.where(first_head, tw_g[:, 2 * k:2 * k + 1], tw_g[:, 2 * k + 1:2 * k + 2])
                y = jnp.dot(m2, bd, preferred_element_type=F32)
                ys.append(y + e_t * inter[:, sl] + dsk[:, sl] * xs[k])
                xws.append((xs[k] * w_t).astype(BF16))
                e_last.append(e_t[last:last + 1])
            upd = lax.dot_general(bg, jnp.concatenate(xws, axis=1), (((0,), (0,)), ((), ())),
                                  preferred_element_type=F32)
            s_ref[g] = jnp.concatenate(e_last, axis=1) * state + upd
            if has_prev:
                off = pl.multiple_of(g * gw, gw)
                z = z_ref[0, rs, pl.ds(off, gw)]
                yt = [(ys[k] + prev_ref[0, g * tpg + k, rs, :]) * _silu(z[:, k * LANE:(k + 1) * LANE])
                      for k in range(tpg)]
                ss = sum(jnp.sum(v * v, axis=-1, keepdims=True) for v in yt)
                inv = lax.rsqrt(ss * (1.0 / gw) + EPS)
                ng = ng_ref[:, pl.ds(off, gw)]
                o_ref[0, rs, pl.ds(off, gw)] = (jnp.concatenate(yt, axis=1) * inv * ng).astype(o_ref.dtype)
            else:
                for k in range(tpg):
                    o_ref[0, g * tpg + k, rs, :] = ys[k]
            return carry

        lax.fori_loop(0, SSD_GROUPS, group, 0)


def _ssd_direction(xbc_t, u2, pvec, exp_q, dsk, n_lat, reverse, prev=None, norm_g=None):
    bsz, _, t, _ = xbc_t.shape
    n = t // ROW_TILE
    nl = n_lat // ROW_TILE
    nsub = ROW_TILE // SSD_CS
    nx = SSD_D_INNER // LANE
    gw = SSD_HPG * SSD_HEADDIM
    if reverse:
        blk = lambda s: n - 1 - s
    else:
        blk = lambda s: (s + nl) % n
    d = 1 if reverse else 0
    full = lambda a: pl.BlockSpec(a.shape, lambda b, s: (0,) * a.ndim)
    in_specs = [pl.BlockSpec((1, nx, ROW_TILE, LANE), lambda b, s: (b, 0, blk(s), 0)),
                pl.BlockSpec((1, SSD_GROUPS, ROW_TILE, LANE), lambda b, s: (b, nx // SSD_GROUPS, blk(s), 0)),
                pl.BlockSpec((1, SSD_GROUPS, ROW_TILE, LANE), lambda b, s: (b, nx // SSD_GROUPS + 1, blk(s), 0)),
                pl.BlockSpec((1, ROW_TILE, LANE), lambda b, s: (b, blk(s), (SSD_IN - 2 * SSD_HEADS) // LANE)),
                full(pvec), full(exp_q),
                pl.BlockSpec((SSD_GROUPS, 1, gw), lambda b, s: (d, 0, 0))]
    args = [xbc_t, xbc_t, xbc_t, u2, pvec, exp_q, dsk.reshape(2 * SSD_GROUPS, 1, gw)]
    has_prev = prev is not None
    if has_prev:
        in_specs += [pl.BlockSpec((1, nx, ROW_TILE, LANE), lambda b, s: (b, 0, blk(s), 0)),
                     pl.BlockSpec((1, ROW_TILE, SSD_D_INNER), lambda b, s: (b, blk(s), 0)),
                     pl.BlockSpec((1, SSD_D_INNER), lambda b, s: (0, 0))]
        args += [prev, u2, norm_g.reshape(1, SSD_D_INNER)]
        out_spec = pl.BlockSpec((1, ROW_TILE, SSD_D_INNER), lambda b, s: (b, blk(s), 0))
        out_shape = jax.ShapeDtypeStruct((bsz, t, SSD_D_INNER), BF16)
    else:
        out_spec = pl.BlockSpec((1, nx, ROW_TILE, LANE), lambda b, s: (b, 0, blk(s), 0))
        out_shape = jax.ShapeDtypeStruct((bsz, nx, t, LANE), F32)
    return pl.pallas_call(
        functools.partial(_ssd_kernel, reverse, has_prev, nsub),
        grid=(bsz, n),
        in_specs=in_specs,
        out_specs=out_spec,
        out_shape=out_shape,
        scratch_shapes=[pltpu.VMEM((SSD_GROUPS, SSD_STATE, gw), F32),
                        pltpu.VMEM((SSD_HEADS, SSD_CS), F32), pltpu.VMEM((SSD_HEADS, SSD_CS), F32),
                        pltpu.VMEM((SSD_CS, SSD_HEADS), F32),
                        pltpu.VMEM((SSD_GROUPS, SSD_CS, SSD_HPG), F32),
                        pltpu.VMEM((SSD_GROUPS, SSD_CS, SSD_HPG), F32)],
        compiler_params=_cparams("parallel", "arbitrary"),
        name="ssd_bwd_norm" if reverse else "ssd_fwd",
    )(*args)


def _ssd(xbc_t, u2, dt_bias, a_log, d_skip, norm_g, n_lat):
    gw = SSD_HPG * SSD_HEADDIM
    pvec = jnp.zeros((8, LANE), F32)
    pvec = pvec.at[0].set(dt_bias.reshape(-1)).at[1].set(-jnp.exp(a_log.astype(F32)).reshape(-1))
    exp_q = np.zeros((SSD_GROUPS, SSD_HEADS, SSD_HPG * LANE), np.float32)
    for gg in range(SSD_GROUPS):
        for r in range(SSD_HPG):
            exp_q[gg, gg * SSD_HPG + r, r * LANE:(r + 1) * LANE] = 1.0
    exp_q = jnp.asarray(exp_q).astype(BF16)
    dsk = jnp.repeat(d_skip.astype(F32).reshape(2, SSD_GROUPS, SSD_HPG), SSD_HEADDIM, axis=-1)
    y_f = _ssd_direction(xbc_t, u2, pvec, exp_q, dsk, n_lat, False)
    return _ssd_direction(xbc_t, u2, pvec, exp_q, dsk, n_lat, True, prev=y_f, norm_g=norm_g.astype(F32))


HY_PAD = 8


def _hy_conv_kernel(n_lat, x_ref, w_ref, b_ref, o_ref, pad_ref):
    t = x_ref.shape[1]
    w = w_ref[...]
    bias = b_ref[...]
    ch = CONV_CHUNK
    pad_ref[0:HY_PAD, :] = jnp.zeros((HY_PAD, LANE), F32)
    pad_ref[HY_PAD + n_lat:2 * HY_PAD + n_lat, :] = jnp.zeros((HY_PAD, LANE), F32)
    pad_ref[HY_PAD:HY_PAD + n_lat, :] = x_ref[0, 0:n_lat, :]
    rowi = lax.broadcasted_iota(jnp.int32, (ch, LANE), 0)

    def body(ci, carry):
        t0 = pl.multiple_of(ci * ch, ch)
        cur = pad_ref[pl.ds(t0 + HY_PAD, ch), :]
        before = pad_ref[pl.ds(t0, HY_PAD), :][HY_PAD - 1:HY_PAD]
        after = pad_ref[pl.ds(t0 + HY_PAD + ch, HY_PAD), :][0:1]
        down = jnp.where(rowi == 0, before, _shift_rows(cur, -1))
        up = jnp.where(rowi == ch - 1, after, _shift_rows(cur, 1))
        o_ref[0, 0, pl.ds(t0, ch), :] = down * w[0:1] + cur * w[1:2] + up * w[2:3] + bias
        return carry

    lax.fori_loop(0, n_lat // ch, body, 0)
    n_ctx = t - n_lat
    xc = x_ref[0, n_lat:t, :]
    pos = lax.broadcasted_iota(jnp.int32, (n_ctx, LANE), 0)
    acc = xc * w[1:2] + bias
    acc = acc + jnp.where(pos >= 1, _shift_rows(xc, -1), 0.0) * w[0:1]
    acc = acc + jnp.where(pos <= n_ctx - 2, _shift_rows(xc, 1), 0.0) * w[2:3]
    o_ref[0, 0, n_lat:t, :] = acc


def _hy_conv(u_all, conv_w, conv_b, n_lat):
    bsz, t, _ = u_all.shape
    nch = (HY_ORDER + 1) * HY_WIDTH
    tpp = HY_WIDTH // LANE
    return pl.pallas_call(
        functools.partial(_hy_conv_kernel, n_lat),
        grid=(bsz, nch // LANE),
        in_specs=[pl.BlockSpec((1, t, LANE), lambda b, j: (b, 0, j)),
                  pl.BlockSpec((3, LANE), lambda b, j: (0, j)),
                  pl.BlockSpec((1, LANE), lambda b, j: (0, j))],
        out_specs=pl.BlockSpec((1, 1, t, LANE), lambda b, j: (j // tpp, b, 0, j % tpp)),
        out_shape=jax.ShapeDtypeStruct((HY_ORDER + 1, bsz, t, HY_WIDTH), F32),
        scratch_shapes=[pltpu.VMEM((n_lat + 2 * HY_PAD, LANE), F32)],
        compiler_params=_cparams("parallel", "parallel"),
        name="hy_short_conv",
    )(u_all, conv_w, conv_b.reshape(1, nch))


FILT_ROWS = 256


def _hy_filter_kernel(seq_len, bands_ref, w1t_ref, b1_ref, fr1_ref, w2t_ref, b2_ref, fr2_ref, w3a_ref, w3b_ref,
                      dl_ref, o_ref):
    i = pl.program_id(0)

    def lag_of(shape, axis):
        n = i * FILT_ROWS + lax.broadcasted_iota(jnp.int32, shape, axis)
        return n, jnp.where(n < seq_len, n, 2 * seq_len - n).astype(F32)

    _, pos_r = lag_of((1, FILT_ROWS), 1)
    tt_r = pos_r / max(seq_len - 1, 1)
    ang = 2.0 * math.pi * bands_ref[...] * pos_r / seq_len
    w1t = w1t_ref[...]
    nb = HY_BANDS
    hp = functools.partial(jnp.dot, precision=HIGHEST, preferred_element_type=F32)
    pre = w1t[:, 0:1] * tt_r + hp(w1t[:, 1:1 + nb], jnp.cos(ang)) - hp(w1t[:, 1 + nb:1 + 2 * nb], jnp.sin(ang))
    hid = jnp.sin(fr1_ref[...] * (pre + b1_ref[...]))
    hid = jnp.sin(fr2_ref[...] * (hp(w2t_ref[...], hid) + b2_ref[...]))
    n_c, pos_c = lag_of((FILT_ROWS, 1), 0)
    decay = jnp.exp(-(pos_c / max(seq_len - 1, 1)) * dl_ref[...])
    hid_rows = hid.T.astype(BF16)
    for o, w3_ref in enumerate((w3a_ref, w3b_ref)):
        h = jnp.dot(hid_rows, w3_ref[...].astype(BF16), preferred_element_type=F32)
        o_ref[o] = jnp.where(n_c == seq_len, 0.0, h * decay)


def _hy_filter(seq_len, f_w1, f_b1, f_fr1, f_w2, f_b2, f_fr2, f_w3):
    assert HY_ORDER == 2
    fh = f_w1.shape[1]
    emb = f_w1.shape[0]
    half = seq_len // FILT_ROWS
    bands = jnp.asarray(np.linspace(1e-4, HY_BANDS - 1, HY_BANDS, dtype=np.float32)).reshape(HY_BANDS, 1)
    deltas = jnp.asarray(np.abs(np.linspace(HY_MIN_DECAY, HY_MAX_DECAY, HY_WIDTH, dtype=np.float32))).reshape(1, HY_WIDTH)
    col = lambda v: v.reshape(fh, 1).astype(F32)
    full = lambda shape: pl.BlockSpec(shape, lambda i: (0,) * len(shape))
    w3 = f_w3.astype(F32)
    return pl.pallas_call(
        functools.partial(_hy_filter_kernel, seq_len),
        grid=(2 * half,),
        in_specs=[full((HY_BANDS, 1)), full((fh, emb)), full((fh, 1)), full((fh, 1)), full((fh, fh)), full((fh, 1)),
                  full((fh, 1)),
                  pl.BlockSpec((fh, HY_WIDTH), lambda i: (0, i // half)),
                  pl.BlockSpec((fh, HY_WIDTH), lambda i: (0, 2 + i // half)),
                  full((1, HY_WIDTH))],
        out_specs=pl.BlockSpec((HY_ORDER, FILT_ROWS, HY_WIDTH), lambda i: (0, i, 0)),
        out_shape=jax.ShapeDtypeStruct((HY_ORDER, 2 * seq_len, HY_WIDTH), F32),
        compiler_params=_cparams("parallel"),
        name="hy_filter",
    )(bands, f_w1.astype(F32).T, col(f_b1), col(f_fr1), f_w2.astype(F32).T, col(f_b2), col(f_fr2), w3, w3, deltas)


DFT_N2 = 256
DFT_S = 8


def _dft_tables(n1):
    n = n1 * DFT_N2
    k1h = n1 // 2 + 1
    k1p = -(-k1h // 8) * 8
    k1 = np.arange(k1p)[:, None].astype(np.float64)
    valid = (np.arange(k1p) < k1h)[:, None]
    th1 = 2.0 * np.pi * k1 * np.arange(n1)[None, :] / n1
    f1 = np.concatenate([np.where(valid, np.cos(th1), 0.0), np.where(valid, -np.sin(th1), 0.0)], axis=0)
    tw = 2.0 * np.pi * k1 * np.arange(DFT_N2)[None, :] / n
    tw_re = np.repeat(np.where(valid, np.cos(tw), 0.0)[:, :, None], LANE, axis=2)
    tw_im = np.repeat(np.where(valid, -np.sin(tw), 0.0)[:, :, None], LANE, axis=2)
    ph = 2.0 * np.pi * np.outer(np.arange(DFT_N2), np.arange(DFT_N2)) / DFT_N2
    f2re, f2im = np.cos(ph), -np.sin(ph)
    w_fwd = np.block([[f2re, -f2im], [f2im, f2re]])
    w_inv = np.block([[f2re, f2im], [-f2im, f2re]])
    wgt = np.where((np.arange(k1p) == 0) | (np.arange(k1p) == n1 // 2), 1.0, 2.0) * (np.arange(k1p) < k1h)
    th_i = 2.0 * np.pi * np.arange(n1 // 2)[:, None] * np.arange(k1p)[None, :] / n1
    g = np.concatenate([wgt * np.cos(th_i), -wgt * np.sin(th_i)], axis=1) / n
    f = lambda a: jnp.asarray(a.astype(np.float32))
    return dict(k1h=k1h, k1p=k1p, f1=f(f1), tw_re=f(tw_re), tw_im=f(tw_im), w_fwd=f(w_fwd), w_inv=f(w_inv), g=f(g))


def _split_bf16(v):
    hi = v.astype(BF16)
    return hi, (v - hi.astype(F32)).astype(BF16)


def _dft_dot_bf16(a, b):
    return jnp.dot(a.astype(BF16), b.astype(BF16), preferred_element_type=F32)


def _dft_dot(a, b):
    ah, al = _split_bf16(a)
    bh, bl = _split_bf16(b)
    out = jnp.dot(ah, bh, preferred_element_type=F32)
    out = out + jnp.dot(ah, bl, preferred_element_type=F32)
    return out + jnp.dot(al, bh, preferred_element_type=F32)


def _dft_first_kernel(k1p, x_ref, f1_ref, twr_ref, twi_ref, ore_ref, oim_ref):
    _, rows, s, cw = x_ref.shape
    reps = cw // LANE
    x = x_ref[0].reshape(rows * s, cw).astype(BF16)
    a = jnp.dot(f1_ref[...], x, preferred_element_type=F32)
    are = a[:k1p * s].reshape(k1p, s, cw)
    aim = a[k1p * s:].reshape(k1p, s, cw)
    tr = jnp.concatenate([twr_ref[...]] * reps, axis=2)
    ti = jnp.concatenate([twi_ref[...]] * reps, axis=2)
    ore_ref[0] = are * tr - aim * ti
    oim_ref[0] = are * ti + aim * tr


def _dft_first(x4, lead0, nlead, f1, tab):
    rows = f1.shape[1]
    cw = x4.shape[3]
    k1p = tab['k1p']
    out = jax.ShapeDtypeStruct((nlead, k1p, DFT_N2, cw), F32)
    ospec = pl.BlockSpec((1, k1p, DFT_S, cw), lambda b, j: (b, 0, j, 0))
    tspec = pl.BlockSpec((k1p, DFT_S, LANE), lambda b, j: (0, j, 0))
    return pl.pallas_call(
        functools.partial(_dft_first_kernel, k1p),
        grid=(nlead, DFT_N2 // DFT_S),
        in_specs=[pl.BlockSpec((1, rows, DFT_S, cw), lambda b, j: (lead0 + b, 0, j, 0)),
                  pl.BlockSpec((2 * k1p * DFT_S, rows * DFT_S), lambda b, j: (0, 0)), tspec, tspec],
        out_specs=[ospec, ospec],
        out_shape=[out, out],
        compiler_params=_cparams("parallel", "parallel"),
        name="hy_dft_first",
    )(x4, jnp.kron(f1, jnp.eye(DFT_S, dtype=F32)).astype(BF16), tab['tw_re'], tab['tw_im'])


DFT_TC = 512


def _dft_mid_kernel(k1h, fused, *refs):
    if fused:
        are_ref, aim_ref, hre_ref, him_ref, wf_ref, wi_ref, twr_ref, twi_ref, ore_ref, oim_ref = refs
    else:
        are_ref, aim_ref, wf_ref, ore_ref, oim_ref = refs
    n2 = DFT_N2

    @pl.when(pl.program_id(0) < k1h)
    def _():
        a = jnp.concatenate([are_ref[0, 0], aim_ref[0, 0]], axis=0)
        dot = _dft_dot_bf16
        x = dot(wf_ref[...], a)
        xre, xim = x[:n2], x[n2:]
        if not fused:
            ore_ref[0, 0] = xre
            oim_ref[0, 0] = xim
        else:
            hre, him = hre_ref[0, 0], him_ref[0, 0]
            y = jnp.concatenate([xre * hre - xim * him, xre * him + xim * hre], axis=0)
            bb = dot(wi_ref[...], y)
            bre, bim = bb[:n2], bb[n2:]
            twr, twi = twr_ref[0], twi_ref[0]
            for c in range(bre.shape[1] // LANE):
                sl = slice(c * LANE, (c + 1) * LANE)
                ore_ref[0, 0, :, sl] = bre[:, sl] * twr + bim[:, sl] * twi
                oim_ref[0, 0, :, sl] = bim[:, sl] * twr - bre[:, sl] * twi

    @pl.when(pl.program_id(0) >= k1h)
    def _():
        ore_ref[...] = jnp.zeros_like(ore_ref)
        oim_ref[...] = jnp.zeros_like(oim_ref)


def _dft_mid(are, aim, tab, spec=None, order=0):
    bsz, k1p, _, cw = are.shape
    tc = min(DFT_TC, cw)
    blk = pl.BlockSpec((1, 1, DFT_N2, tc), lambda k, c, b: (b, k, 0, c))
    wspec = pl.BlockSpec((2 * DFT_N2, 2 * DFT_N2), lambda k, c, b: (0, 0))
    fused = spec is not None
    if fused:
        hspec = pl.BlockSpec((1, 1, DFT_N2, tc), lambda k, c, b: (order, k, 0, c))
        tspec = pl.BlockSpec((1, DFT_N2, LANE), lambda k, c, b: (k, 0, 0))
        in_specs = [blk, blk, hspec, hspec, wspec, wspec, tspec, tspec]
        args = [are, aim, spec[0], spec[1], tab['w_fwd'].astype(BF16), tab['w_inv'].astype(BF16),
                tab['tw_re'], tab['tw_im']]
    else:
        in_specs = [blk, blk, wspec]
        args = [are, aim, tab['w_fwd']]
    out = jax.ShapeDtypeStruct((bsz, k1p, DFT_N2, cw), F32)
    ore, oim = pl.pallas_call(
        functools.partial(_dft_mid_kernel, tab['k1h'], fused),
        grid=(k1p, cw // tc, bsz),
        in_specs=in_specs,
        out_specs=[blk, blk],
        out_shape=[out, out],
        compiler_params=_cparams("parallel", "parallel", "arbitrary"),
        name="hy_dft_mid_conv" if fused else "hy_dft_mid_filter",
    )(*args)
    return ore, oim


def _dft_last_kernel(bre_ref, bim_ref, g_ref, u_ref, x_ref, d_ref, o_ref):
    _, k1p, s, cw = bre_ref.shape
    rows = u_ref.shape[1]
    bb = jnp.concatenate([bre_ref[0].reshape(k1p * s, cw), bim_ref[0].reshape(k1p * s, cw)], axis=0).astype(BF16)
    y = jnp.dot(g_ref[...], bb, preferred_element_type=F32).reshape(rows, s, cw)
    o_ref[0, 0:rows] = x_ref[0] * (y + u_ref[0] * d_ref[...])
    if o_ref.shape[1] > rows:
        o_ref[0, rows:, :, :] = jnp.zeros((o_ref.shape[1] - rows,) + o_ref.shape[2:], F32)


def _dft_last(bre, bim, tab, u4, u_lead0, x4, x_lead0, dvec, out_rows):
    bsz, k1p, _, cw = bre.shape
    rows = tab['g'].shape[0]
    bspec = pl.BlockSpec((1, k1p, DFT_S, cw), lambda b, j: (b, 0, j, 0))
    return pl.pallas_call(
        _dft_last_kernel,
        grid=(bsz, DFT_N2 // DFT_S),
        in_specs=[bspec, bspec, pl.BlockSpec((rows * DFT_S, 2 * k1p * DFT_S), lambda b, j: (0, 0)),
                  pl.BlockSpec((1, rows, DFT_S, cw), lambda b, j: (u_lead0 + b, 0, j, 0)),
                  pl.BlockSpec((1, rows, DFT_S, cw), lambda b, j: (x_lead0 + b, 0, j, 0)),
                  pl.BlockSpec((1, 1, cw), lambda b, j: (0, 0, 0))],
        out_specs=pl.BlockSpec((1, out_rows, DFT_S, cw), lambda b, j: (b, 0, j, 0)),
        out_shape=jax.ShapeDtypeStruct((bsz, out_rows, DFT_N2, cw), F32),
        compiler_params=_cparams("parallel", "parallel"),
        name="hy_dft_last",
    )(bre, bim, jnp.kron(tab['g'], jnp.eye(DFT_S, dtype=F32)).astype(BF16), u4, x4,
      dvec.reshape(1, 1, cw).astype(F32))


def _ctx_dft_tables(n):
    size = 2 * n
    kp = -(-(n + 1) // 8) * 8
    k = np.arange(kp)[:, None].astype(np.float64)
    valid = (np.arange(kp) <= n)[:, None]
    th = 2.0 * np.pi * k * np.arange(size)[None, :] / size
    fw = np.concatenate([np.where(valid, np.cos(th), 0.0), np.where(valid, -np.sin(th), 0.0)], axis=0)
    wgt = np.where((np.arange(kp) == 0) | (np.arange(kp) == n), 1.0, 2.0) * (np.arange(kp) <= n)
    thi = 2.0 * np.pi * np.arange(n)[:, None] * np.arange(kp)[None, :] / size
    inv = np.concatenate([wgt * np.cos(thi), -wgt * np.sin(thi)], axis=1) / size
    return kp, jnp.asarray(fw.astype(np.float32)), jnp.asarray(inv.astype(np.float32))


def _hy_ctx_kernel(kp, p_ref, f_ref, d_ref, fw_ref, inv_ref, y_ref, o_ref):
    del y_ref
    n = p_ref.shape[2]
    fw = fw_ref[...]
    y = p_ref[0, 0]
    for o in range(HY_ORDER):
        h = _dft_dot(fw, f_ref[o])
        x = _dft_dot(fw[:, :n], y)
        hre, him, xre, xim = h[:kp], h[kp:], x[:kp], x[kp:]
        prod = jnp.concatenate([xre * hre - xim * him, xre * him + xim * hre], axis=0)
        conv = _dft_dot(inv_ref[...], prod)
        y = p_ref[o + 1, 0] * (conv + y * d_ref[o:o + 1, :])
    o_ref[0] = y


CTX_TC = 256


def _hy_ctx(parts, filt_c, long_bias, y_all, n_lat):
    nparts, bsz, t, cw = parts.shape
    n_ctx = t - n_lat
    kp, fw, inv = _ctx_dft_tables(n_ctx)
    return pl.pallas_call(
        functools.partial(_hy_ctx_kernel, kp),
        grid=(bsz, cw // CTX_TC),
        in_specs=[pl.BlockSpec((nparts, 1, n_ctx, CTX_TC), lambda b, c: (0, b, n_lat // n_ctx, c)),
                  pl.BlockSpec((HY_ORDER, 2 * n_ctx, CTX_TC), lambda b, c: (0, 0, c)),
                  pl.BlockSpec((HY_ORDER, CTX_TC), lambda b, c: (0, c)),
                  pl.BlockSpec(fw.shape, lambda b, c: (0, 0)),
                  pl.BlockSpec(inv.shape, lambda b, c: (0, 0)),
                  pl.BlockSpec(memory_space=pl.ANY)],
        out_specs=pl.BlockSpec((1, n_ctx, CTX_TC), lambda b, c: (b, n_lat // n_ctx, c)),
        out_shape=jax.ShapeDtypeStruct(y_all.shape, F32),
        input_output_aliases={5: 0},
        compiler_params=_cparams("parallel", "parallel"),
        name="hy_ctx_conv",
    )(parts, filt_c, long_bias.astype(F32), fw, inv, y_all)


def _hyena(u_all, hy_p, n_lat):
    conv_w, conv_b, f_w1, f_b1, f_fr1, f_w2, f_b2, f_fr2, f_w3, long_bias = hy_p
    bsz, t, _ = u_all.shape
    n_ctx = t - n_lat
    assert n_ctx == DFT_N2 and n_lat % (2 * DFT_N2) == 0
    n1 = 2 * n_lat // DFT_N2
    tab = _dft_tables(n1)
    cw = HY_WIDTH
    parts = _hy_conv(u_all, conv_w, conv_b, n_lat)
    filt_l = _hy_filter(n_lat, f_w1, f_b1, f_fr1, f_w2, f_b2, f_fr2, f_w3)
    filt_c = _hy_filter(n_ctx, f_w1, f_b1, f_fr1, f_w2, f_b2, f_fr2, f_w3)
    spec = _dft_mid(*_dft_first(filt_l.reshape(HY_ORDER, n1, DFT_N2, cw), 0, HY_ORDER, tab['f1'], tab), tab)
    f1_half = tab['f1'][:, :n1 // 2]
    parts4 = parts.reshape((HY_ORDER + 1) * bsz, t // DFT_N2, DFT_N2, cw)
    y4, y_lead0 = parts4, 0
    for o in range(HY_ORDER):
        are, aim = _dft_first(y4, y_lead0, bsz, f1_half, tab)
        bre, bim = _dft_mid(are, aim, tab, spec=spec, order=o)
        last = o + 1 == HY_ORDER
        y4 = _dft_last(bre, bim, tab, y4, y_lead0, parts4, (o + 1) * bsz, long_bias[o],
                       t // DFT_N2 if last else n1 // 2)
        y_lead0 = 0
    return _hy_ctx(parts, filt_c, long_bias, y4.reshape(bsz, t, cw), n_lat)


def _row_tile(m):
    for tm in (1280, 1024, 512, 256):
        if m % tm == 0:
            return tm
    raise ValueError(m)


def _col_tile(n):
    for k in range(n // LANE, 0, -1):
        if n % (k * LANE) == 0 and k * LANE <= 1280:
            return k * LANE
    raise ValueError(n)


def _in_proj(h, w):
    bsz, t, d = h.shape
    n = w.shape[1]
    return _matmul(h.reshape(bsz * t, d), w, _row_tile(bsz * t), _col_tile(n)).reshape(bsz, t, n)


def kernel(x, c, ctx, c_ctx, router_w, router_b, mod_w, mod_b, ln_g, ln_b, exp_w_gate, exp_w_up, exp_w_down, ab_w_in, ab_w_out, hy_conv_w, hy_conv_b, hy_f_w1, hy_f_b1, hy_f_fr1, hy_f_w2, hy_f_b2, hy_f_fr2, hy_f_w3, hy_long_bias, gla_gate_w2, gla_gate_b, gla_norm_g, ssd_w_in, ssd_conv_w, ssd_conv_b, ssd_dt_bias, ssd_a_log, ssd_d, ssd_norm_g, ssd_w_out):
    bsz, n_lat, d = x.shape
    n_ctx = ctx.shape[1]
    assert bsz < 8 and n_lat % ROW_TILE == 0 and n_ctx % ROW_TILE == 0 and d == D_MODEL
    xa = jnp.concatenate([x, ctx], axis=1).astype(F32)
    c8 = jnp.zeros((8, d), F32).at[:bsz].set(c).at[bsz].set(c_ctx)
    router_w_pad = jnp.zeros((d, LANE), F32).at[:, :N_EXPERTS].set(router_w)
    mods = [_mod_vectors(c8, mod_w[i].astype(F32), mod_b[i].astype(F32)).reshape(8, 6, d) for i in range(DEPTH)]

    def rows(i, idx_l):
        lat = jnp.stack([mods[i][:bsz, k] for k in idx_l], axis=1)
        cx = jnp.broadcast_to(jnp.stack([mods[i][bsz, k] for k in idx_l], axis=0)[None], lat.shape)
        return jnp.concatenate([lat, cx], axis=1)

    def with_ln(p, g, b):
        extra = jnp.broadcast_to(jnp.stack([g, b], axis=0).astype(F32)[None], (bsz, 2, d))
        return jnp.concatenate([p, extra], axis=1)

    h = _modulate(xa, rows(0, (0, 1)), n_lat)
    for i in range(DEPTH):
        j = i // 2
        if i % 2 == 0:
            w_pad = -(-AB_IN // LANE) * LANE
            w_in = jnp.zeros((d, w_pad), BF16).at[:, :AB_IN].set(ab_w_in[j].astype(BF16))
            u_all = _in_proj(h, w_in)
            hy_p = (hy_conv_w[j], hy_conv_b[j], hy_f_w1[j], hy_f_b1[j], hy_f_fr1[j], hy_f_w2[j], hy_f_b2[j],
                    hy_f_fr2[j], hy_f_w3[j], hy_long_bias[j])
            y_hy = _hyena(u_all, hy_p, n_lat)
            y_gla = _gla(u_all, gla_gate_w2[j], gla_gate_b[j], gla_norm_g[j], n_lat)
            w_out = ab_w_out[j].astype(BF16)
            a_list, w_list = [y_hy, y_gla], [w_out[:HY_WIDTH], w_out[HY_WIDTH:]]
        else:
            u2 = _in_proj(h, ssd_w_in[j].astype(BF16))
            xbc_t = _ssd_conv(u2, ssd_conv_w[j].astype(F32), ssd_conv_b[j].astype(F32), n_lat)
            y_ssd = _ssd(xbc_t, u2, ssd_dt_bias[j].astype(F32), ssd_a_log[j], ssd_d[j].astype(F32), ssd_norm_g[j], n_lat)
            a_list, w_list = [y_ssd], [ssd_w_out[j].astype(BF16)]
        p1 = with_ln(rows(i, (2, 4, 3)), ln_g[i, 0], ln_b[i, 0])
        xa, tok, logits = _proj_update(a_list, w_list, xa, p1, router_w_pad, n_lat)
        y0, y1, gates = _moe(tok, logits, router_b, exp_w_gate.astype(F32), exp_w_up.astype(F32),
                             exp_w_down.astype(F32), i)
        gate2 = rows(i, (5,))
        if i + 1 < DEPTH:
            nxt = rows(i + 1, (1, 0))
        else:
            nxt = jnp.zeros((bsz, 4, d), F32)
        p2 = jnp.concatenate([gate2[:, 0:1], nxt[:, 0:2], gate2[:, 1:2], nxt[:, 2:4]], axis=1)
        xa, h = _combine_update(y0, y1, gates, xa, with_ln(p2, ln_g[i, 1], ln_b[i, 1]), n_lat, i + 1 == DEPTH)
    return xa.astype(x.dtype)
```

```python
import functools
import math

import numpy as np
import jax
import jax.numpy as jnp
from jax import lax
from jax.experimental import pallas as pl
from jax.experimental.pallas import tpu as pltpu

F32 = jnp.float32
BF16 = jnp.bfloat16
HIGHEST = lax.Precision.HIGHEST

D_MODEL = 2048
DEPTH = 2
GRID_W = 64
HY_WIDTH = D_MODEL // 2
HY_ORDER = 2
HY_BANDS = 16
HY_MIN_DECAY = math.log(1e-2) / 1.5
HY_MAX_DECAY = math.log(1e-2) / 0.3
GLA_HEADS = 4
GLA_DK = D_MODEL // 4
GLA_DV = D_MODEL // 2
GLA_HEAD_K = GLA_DK // GLA_HEADS
GLA_HEAD_V = GLA_DV // GLA_HEADS
GLA_RANK = 16
GLA_TAU = 16.0
AB_SIZES = ((HY_ORDER + 1) * HY_WIDTH, GLA_DK, GLA_DK, GLA_DV, GLA_DV, GLA_RANK, GLA_RANK)
AB_IN = sum(AB_SIZES)
SSD_D_INNER = 2 * D_MODEL
SSD_HEADDIM = 64
SSD_HEADS = SSD_D_INNER // SSD_HEADDIM
SSD_GROUPS = 8
SSD_HPG = SSD_HEADS // SSD_GROUPS
SSD_STATE = 128
SSD_CONV_DIM = SSD_D_INNER + 2 * SSD_GROUPS * SSD_STATE
SSD_IN = SSD_D_INNER + SSD_CONV_DIM + 2 * SSD_HEADS
N_EXPERTS = 16
N_EXPERT_GROUPS = 4
EXPERTS_PER_GROUP = N_EXPERTS // N_EXPERT_GROUPS
TOP_K = 2
D_EXPERT = D_MODEL // 2
ALPHA = (2 * DEPTH) ** 0.25
EPS = 1e-6

LANE = 128
ROW_TILE = 256
MOE_BLOCK = 256
VMEM_LIMIT = 56 * 1024 * 1024


def _cparams(*sem):
    return pltpu.CompilerParams(dimension_semantics=sem, vmem_limit_bytes=VMEM_LIMIT)


def _silu(x):
    return x * (1.0 / (1.0 + jnp.exp(-x)))


def _mod_kernel(c_ref, w_ref, b_ref, o_ref):
    o_ref[...] = jnp.dot(_silu(c_ref[...]), w_ref[...], precision=HIGHEST, preferred_element_type=F32) + b_ref[...]


def _mod_vectors(c8, w, b):
    d, n = w.shape
    tn = 1536
    return pl.pallas_call(
        _mod_kernel,
        grid=(n // tn,),
        in_specs=[pl.BlockSpec((8, d), lambda j: (0, 0)),
                  pl.BlockSpec((d, tn), lambda j: (0, j)),
                  pl.BlockSpec((1, tn), lambda j: (0, j))],
        out_specs=pl.BlockSpec((8, tn), lambda j: (0, j)),
        out_shape=jax.ShapeDtypeStruct((8, n), F32),
        compiler_params=_cparams("arbitrary"),
        name="adaln_vectors",
    )(c8, w, b.reshape(1, n))


def _modulate_kernel(n_lat_tiles, x_ref, c_ref, p_ref, xa_ref, o_ref):
    i = pl.program_id(1)
    is_ctx = i >= n_lat_tiles
    p = p_ref[0]
    shift = jnp.where(is_ctx, p[2:3], p[0:1])
    scale = jnp.where(is_ctx, p[3:4], p[1:2])
    ci = jnp.maximum(i - n_lat_tiles, 0)
    v = jnp.where(is_ctx, c_ref[0, pl.ds(pl.multiple_of(ci * ROW_TILE, ROW_TILE), ROW_TILE), :], x_ref[0])
    xa_ref[0] = v
    o_ref[0] = (v * (1.0 + scale) + shift).astype(o_ref.dtype)


def _modulate(x, ctx, params):
    bsz, n_lat, d = x.shape
    n_ctx = ctx.shape[1]
    t = n_lat + n_ctx
    nl = n_lat // ROW_TILE
    row = pl.BlockSpec((1, ROW_TILE, d), lambda b, i: (b, i, 0))
    return pl.pallas_call(
        functools.partial(_modulate_kernel, nl),
        grid=(bsz, t // ROW_TILE),
        in_specs=[pl.BlockSpec((1, ROW_TILE, d), lambda b, i: (b, jnp.minimum(i, nl - 1), 0)),
                  pl.BlockSpec((1, n_ctx, d), lambda b, i: (b, 0, 0)),
                  pl.BlockSpec((1, 4, d), lambda b, i: (b, 0, 0))],
        out_specs=[row, row],
        out_shape=[jax.ShapeDtypeStruct((bsz, t, d), F32), jax.ShapeDtypeStruct((bsz, t, d), BF16)],
        compiler_params=_cparams("parallel", "parallel"),
        name="modulate",
    )(x.astype(F32), ctx.astype(F32), params)


def _matmul_kernel(a_ref, w_ref, o_ref):
    o_ref[...] = jnp.dot(a_ref[...], w_ref[...], preferred_element_type=F32).astype(o_ref.dtype)


def _matmul(a, w, tm, tn, out_dtype=F32):
    m, k = a.shape
    n = w.shape[1]
    return pl.pallas_call(
        _matmul_kernel,
        grid=(m // tm, n // tn),
        in_specs=[pl.BlockSpec((tm, k), lambda i, j: (i, 0)),
                  pl.BlockSpec((k, tn), lambda i, j: (0, j))],
        out_specs=pl.BlockSpec((tm, tn), lambda i, j: (i, j)),
        out_shape=jax.ShapeDtypeStruct((m, n), out_dtype),
        compiler_params=_cparams("parallel", "arbitrary"),
        name="in_proj",
    )(a, w)


def _layer_norm_rows(v, g, b):
    mu = jnp.mean(v, axis=-1, keepdims=True)
    vc = v - mu
    var = jnp.mean(vc * vc, axis=-1, keepdims=True)
    return vc * lax.rsqrt(var + EPS) * g + b


def _param_table(p):
    bsz, _, d = p.shape
    pad = jnp.zeros((bsz, 3, d), F32)
    lat = jnp.concatenate([p[:, 0:3], p[:, 6:8], pad], axis=1)
    cx = jnp.concatenate([p[:, 3:6], p[:, 6:8], pad], axis=1)
    return jnp.stack([lat, cx], axis=1).reshape(2 * bsz, 8, d)


def _post_update(tiles_per_batch, n_lat_tiles, tile, y, x, p_ref):
    b = tile // tiles_per_batch
    is_ctx = (tile - b * tiles_per_batch >= n_lat_tiles).astype(jnp.int32)
    p = p_ref[2 * b + is_ctx]
    xn = _layer_norm_rows(ALPHA * x + p[0:1] * y, p[3:4], p[4:5])
    return xn, xn * (1.0 + p[1:2]) + p[2:3]


PROJ_TILES = 2


def _proj_update_kernel(tiles_per_batch, n_lat_tiles, n_a, *refs):
    a_refs = refs[:n_a]
    w_refs = refs[n_a:2 * n_a]
    x_ref, p_ref, rw_ref, xo_ref, tok_ref, lg_ref, y_ref = refs[2 * n_a:]
    i = pl.program_id(0)

    @pl.when(i == 0)
    def _():
        y_ref[...] = jnp.zeros_like(y_ref)

    prev = jnp.maximum(i - 1, 0)
    for hh in range(PROJ_TILES):
        rs = slice(hh * ROW_TILE, (hh + 1) * ROW_TILE)
        xn, tok = _post_update(tiles_per_batch, n_lat_tiles, prev * PROJ_TILES + hh, y_ref[rs, :], x_ref[rs, :], p_ref)
        xo_ref[rs, :] = xn
        tok_ref[rs, :] = tok.astype(tok_ref.dtype)
        lg_ref[rs, :] = _dft_dot(tok, rw_ref[...])
    for hh in range(PROJ_TILES):
        rs = slice(hh * ROW_TILE, (hh + 1) * ROW_TILE)
        y = jnp.dot(a_refs[0][rs, :].astype(BF16), w_refs[0][...], preferred_element_type=F32)
        for a_ref, w_ref in zip(a_refs[1:], w_refs[1:]):
            y = y + jnp.dot(a_ref[rs, :].astype(BF16), w_ref[...], preferred_element_type=F32)
        y_ref[rs, :] = y


def _proj_update(a_list, w_list, x, params, router_w_pad, n_lat):
    bsz, t, d = x.shape
    n = bsz * t
    tm = PROJ_TILES * ROW_TILE
    assert n % tm == 0
    steps = n // tm
    n_a = len(a_list)
    flat = lambda v: v.reshape(n, v.shape[-1])
    cur = lambda i: (jnp.minimum(i, steps - 1), 0)
    prev = lambda i: (jnp.maximum(i - 1, 0), 0)
    in_specs = [pl.BlockSpec((tm, a.shape[-1]), cur) for a in a_list]
    in_specs += [pl.BlockSpec(w.shape, lambda i: (0, 0)) for w in w_list]
    in_specs += [pl.BlockSpec((tm, d), prev),
                 pl.BlockSpec((2 * bsz, 8, d), lambda i: (0, 0, 0)),
                 pl.BlockSpec((d, LANE), lambda i: (0, 0))]
    xo, tok, lg = pl.pallas_call(
        functools.partial(_proj_update_kernel, t // ROW_TILE, n_lat // ROW_TILE, n_a),
        grid=(steps + 1,),
        in_specs=in_specs,
        out_specs=[pl.BlockSpec((tm, d), prev), pl.BlockSpec((tm, d), prev), pl.BlockSpec((tm, LANE), prev)],
        out_shape=[jax.ShapeDtypeStruct((n, d), F32),
                   jax.ShapeDtypeStruct((n, d), BF16),
                   jax.ShapeDtypeStruct((n, LANE), F32)],
        scratch_shapes=[pltpu.VMEM((tm, d), F32)],
        compiler_params=_cparams("arbitrary"),
        name="out_proj_ln",
    )(*[flat(a) for a in a_list], *w_list, flat(x), _param_table(params), router_w_pad)
    return xo.reshape(bsz, t, d), tok.reshape(bsz, t, d), lg.reshape(bsz, t, LANE)


def _route_kernel(lg_ref, rb_ref, tri_ref, oi_ref, og_ref, cnt_ref, carry_ref):
    @pl.when(pl.program_id(0) == 0)
    def _():
        carry_ref[...] = jnp.zeros_like(carry_ref)

    tm = lg_ref.shape[0]
    epg = EXPERTS_PER_GROUP
    lt = lg_ref[...].T[:N_EXPERTS]
    score = 1.0 / (1.0 + jnp.exp(-lt))
    sel = score + rb_ref[...]
    best_g = None
    for q in range(N_EXPERT_GROUPS):
        rows = [sel[q * epg + r:q * epg + r + 1] for r in range(epg)]
        gs = None
        for a in range(epg):
            for b in range(a + 1, epg):
                ps = rows[a] + rows[b]
                gs = ps if gs is None else jnp.maximum(gs, ps)
        if best_g is None:
            best_g, grp = gs, jnp.zeros_like(gs, dtype=jnp.int32)
        else:
            better = gs > best_g
            grp = jnp.where(better, q, grp)
            best_g = jnp.where(better, gs, best_g)
    in_sel, in_score = [], []
    for r in range(epg):
        v = sel[r:r + 1]
        s = score[r:r + 1]
        for q in range(1, N_EXPERT_GROUPS):
            v = jnp.where(grp == q, sel[q * epg + r:q * epg + r + 1], v)
            s = jnp.where(grp == q, score[q * epg + r:q * epg + r + 1], s)
        in_sel.append(v)
        in_score.append(s)
    i1, v1 = jnp.zeros_like(grp), in_sel[0]
    for r in range(1, epg):
        better = in_sel[r] > v1
        i1 = jnp.where(better, r, i1)
        v1 = jnp.where(better, in_sel[r], v1)
    i2, v2 = None, None
    for r in range(epg):
        cand = jnp.where(i1 == r, -jnp.inf, in_sel[r])
        if i2 is None:
            i2, v2 = jnp.zeros_like(grp), cand
        else:
            better = cand > v2
            i2 = jnp.where(better, r, i2)
            v2 = jnp.where(better, cand, v2)
    s1, s2 = in_score[0], in_score[0]
    for r in range(1, epg):
        s1 = jnp.where(i1 == r, in_score[r], s1)
        s2 = jnp.where(i2 == r, in_score[r], s2)
    e1 = grp * epg + i1
    e2 = grp * epg + i2
    tot = s1 + s2
    g1, g2 = s1 / tot, s2 / tot
    eid = lax.broadcasted_iota(jnp.int32, (N_EXPERTS, tm), 0)
    oh1 = eid == e1
    oh2 = eid == e2
    both = jnp.where(oh1 | oh2, 1.0, 0.0)
    before = jnp.dot(both.astype(BF16), tri_ref[...], preferred_element_type=F32) + carry_ref[:, 0:1]
    r1 = jnp.sum(jnp.where(oh1, before, 0.0), axis=0, keepdims=True)
    r2 = jnp.sum(jnp.where(oh2, before, 0.0), axis=0, keepdims=True)
    new_carry = carry_ref[...] + jnp.sum(both, axis=1, keepdims=True)
    carry_ref[...] = new_carry
    cnt_ref[...] = new_carry.astype(jnp.int32)
    zi = jnp.zeros((4, tm), jnp.int32)
    oi_ref[...] = jnp.concatenate([e1, e2, r1.astype(jnp.int32), r2.astype(jnp.int32), zi], axis=0)
    gt = jnp.concatenate([g1, g2, jnp.zeros((LANE - 2, tm), F32)], axis=0)
    og_ref[...] = gt.T


def _route(logits, router_b):
    n = logits.shape[0]
    tm = ROW_TILE
    tri = jnp.asarray(np.triu(np.ones((tm, tm), np.float32), 1)).astype(BF16)
    oi, og, cnt = pl.pallas_call(
        _route_kernel,
        grid=(n // tm,),
        in_specs=[pl.BlockSpec((tm, LANE), lambda i: (i, 0)),
                  pl.BlockSpec((N_EXPERTS, 1), lambda i: (0, 0)),
                  pl.BlockSpec((tm, tm), lambda i: (0, 0))],
        out_specs=[pl.BlockSpec((8, tm), lambda i: (0, i)),
                   pl.BlockSpec((tm, LANE), lambda i: (i, 0)),
                   pl.BlockSpec((N_EXPERTS, LANE), lambda i: (0, 0))],
        out_shape=[jax.ShapeDtypeStruct((8, n), jnp.int32),
                   jax.ShapeDtypeStruct((n, LANE), F32),
                   jax.ShapeDtypeStruct((N_EXPERTS, LANE), jnp.int32)],
        scratch_shapes=[pltpu.VMEM((N_EXPERTS, LANE), F32)],
        compiler_params=_cparams("arbitrary"),
        name="moe_route",
    )(logits, router_b.reshape(N_EXPERTS, 1).astype(F32), tri)
    return oi, og, cnt[:, 0]


W_CONV_ROWS = 256


def _expert_kernel(layer, be_ref, nb_ref, first_ref, nxt_ref, x_ref, wg_hbm, wu_hbm, wd_hbm, o_ref,
                   sg_ref, su_ref, sd_ref, wg_ref, wu_ref, wd_ref, sem):
    i = pl.program_id(0)

    def copies(e):
        return (pltpu.make_async_copy(wg_hbm.at[layer, e], sg_ref, sem.at[0]),
                pltpu.make_async_copy(wu_hbm.at[layer, e], su_ref, sem.at[1]),
                pltpu.make_async_copy(wd_hbm.at[layer, e], sd_ref, sem.at[2]))

    def convert(src, dst):
        def body(r, carry):
            r0 = pl.multiple_of(r * W_CONV_ROWS, W_CONV_ROWS)
            dst[pl.ds(r0, W_CONV_ROWS), :] = src[pl.ds(r0, W_CONV_ROWS), :].astype(BF16)
            return carry
        lax.fori_loop(0, src.shape[0] // W_CONV_ROWS, body, 0)

    active = i < nb_ref[0]

    @pl.when(active & (i == 0))
    def _():
        for cp in copies(be_ref[0]):
            cp.start()

    @pl.when(active & (first_ref[i] == 1))
    def _():
        for cp in copies(be_ref[i]):
            cp.wait()
        convert(sg_ref, wg_ref)
        convert(su_ref, wu_ref)
        convert(sd_ref, wd_ref)

        @pl.when(nxt_ref[i] >= 0)
        def _():
            for cp in copies(nxt_ref[i]):
                cp.start()

    @pl.when(active)
    def _():
        x = x_ref[...]
        hg = jnp.dot(x, wg_ref[...], preferred_element_type=F32)
        hu = jnp.dot(x, wu_ref[...], preferred_element_type=F32)
        h = (_silu(hg) * hu).astype(BF16)
        o_ref[...] = jnp.dot(h, wd_ref[...], preferred_element_type=F32).astype(o_ref.dtype)

    @pl.when(jnp.logical_not(active))
    def _():
        o_ref[...] = jnp.zeros_like(o_ref)


def _expert_blocks(buf, block_expert, n_used, first, nxt, wg, wu, wd, layer):
    rows, d = buf.shape
    n_blocks = rows // MOE_BLOCK
    de = wg.shape[-1]
    grid_spec = pltpu.PrefetchScalarGridSpec(
        num_scalar_prefetch=4,
        grid=(n_blocks,),
        in_specs=[pl.BlockSpec((MOE_BLOCK, d), lambda i, *_: (i, 0)),
                  pl.BlockSpec(memory_space=pl.ANY),
                  pl.BlockSpec(memory_space=pl.ANY),
                  pl.BlockSpec(memory_space=pl.ANY)],
        out_specs=pl.BlockSpec((MOE_BLOCK, d), lambda i, *_: (i, 0)),
        scratch_shapes=[pltpu.VMEM((d, de), F32), pltpu.VMEM((d, de), F32), pltpu.VMEM((de, d), F32),
                        pltpu.VMEM((d, de), BF16), pltpu.VMEM((d, de), BF16), pltpu.VMEM((de, d), BF16),
                        pltpu.SemaphoreType.DMA((3,))],
    )
    return pl.pallas_call(
        functools.partial(_expert_kernel, layer),
        grid_spec=grid_spec,
        out_shape=jax.ShapeDtypeStruct((rows, d), BF16),
        compiler_params=_cparams("arbitrary"),
        name="moe_experts",
    )(block_expert, n_used, first, nxt, buf, wg, wu, wd)


def _combine_update_kernel(tiles_per_batch, n_lat_tiles, last, y0_ref, y1_ref, g_ref, x_ref, p_ref, xo_ref, *h_ref):
    i = pl.program_id(0)
    tile = (i // n_lat_tiles) * tiles_per_batch + i % n_lat_tiles if last else i
    g = g_ref[...]
    y = y0_ref[...].astype(F32) * g[:, 0:1] + y1_ref[...].astype(F32) * g[:, 1:2]
    xn, h = _post_update(tiles_per_batch, n_lat_tiles, tile, y, x_ref[...], p_ref)
    xo_ref[...] = xn
    if not last:
        h_ref[0][...] = h.astype(h_ref[0].dtype)


def _combine_update(y0, y1, gates, x, params, n_lat, last):
    bsz, t, d = x.shape
    n = bsz * t
    tpb, nl = t // ROW_TILE, n_lat // ROW_TILE
    if last:
        src = lambda i: ((i // nl) * tpb + i % nl, 0)
        steps, out_rows = bsz * nl, bsz * n_lat
    else:
        src = lambda i: (i, 0)
        steps, out_rows = n // ROW_TILE, n
    row_in = pl.BlockSpec((ROW_TILE, d), src)
    row_out = pl.BlockSpec((ROW_TILE, d), lambda i: (i, 0))
    out_shape = [jax.ShapeDtypeStruct((out_rows, d), F32)] + ([] if last else [jax.ShapeDtypeStruct((n, d), BF16)])
    outs = pl.pallas_call(
        functools.partial(_combine_update_kernel, tpb, nl, last),
        grid=(steps,),
        in_specs=[row_in, row_in, pl.BlockSpec((ROW_TILE, LANE), src), row_in,
                  pl.BlockSpec((2 * bsz, 8, d), lambda i: (0, 0, 0))],
        out_specs=[row_out] * len(out_shape),
        out_shape=out_shape,
        compiler_params=_cparams("parallel"),
        name="moe_combine_ln",
    )(y0, y1, gates, x.reshape(n, d), _param_table(params))
    if last:
        return outs[0].reshape(bsz, n_lat, d), None
    return outs[0].reshape(bsz, t, d), outs[1].reshape(bsz, t, d)


def _moe(tok, logits, router_b, wg, wu, wd, layer):
    bsz, t, d = tok.shape
    n = bsz * t
    tk = n * TOP_K
    oi, gates, counts = _route(logits.reshape(n, LANE), router_b)
    experts, ranks = oi[0:2], oi[2:4]
    padded = (counts + MOE_BLOCK - 1) // MOE_BLOCK * MOE_BLOCK
    pends = jnp.cumsum(padded)
    pstarts = pends - padded
    eids = jnp.arange(N_EXPERTS, dtype=jnp.int32)
    table = lambda idx, tab: jnp.sum(jnp.where(idx[..., None] == eids, tab.astype(jnp.int32), 0), axis=-1)
    dest = table(experts, pstarts) + ranks
    n_blocks = -(-tk // MOE_BLOCK) + N_EXPERTS
    blk_start = jnp.arange(n_blocks, dtype=jnp.int32) * MOE_BLOCK
    block_expert = jnp.minimum(jnp.sum(pends[None, :] <= blk_start[:, None], axis=1), N_EXPERTS - 1).astype(jnp.int32)
    n_used = (pends[-1] // MOE_BLOCK).astype(jnp.int32).reshape(1)
    prev_e = jnp.concatenate([jnp.full((1,), -1, jnp.int32), block_expert[:-1]])
    first = (block_expert != prev_e).astype(jnp.int32)
    later = jnp.where((counts[None, :] > 0) & (jnp.arange(N_EXPERTS)[None, :] > jnp.arange(N_EXPERTS)[:, None]),
                      jnp.arange(N_EXPERTS)[None, :], N_EXPERTS)
    nxt_e = jnp.min(later, axis=1)
    nxt_e = jnp.where(nxt_e >= N_EXPERTS, -1, nxt_e).astype(jnp.int32)
    nxt = table(block_expert, nxt_e)
    tok_id = jnp.broadcast_to(jnp.arange(n, dtype=jnp.int32)[None], (TOP_K, n))
    src = jnp.zeros((n_blocks * MOE_BLOCK,), jnp.int32).at[dest.reshape(-1)].set(
        tok_id.reshape(-1), mode='promise_in_bounds', unique_indices=True)
    take = lambda a, idx: a.at[idx].get(mode='promise_in_bounds')
    buf = take(tok.reshape(n, d), src)
    out = _expert_blocks(buf, block_expert, n_used, first, nxt, wg, wu, wd, layer)
    return take(out, dest[0]), take(out, dest[1]), gates


GLA_CS = 64
EXP_CLAMP = 80.0


def _gla_kernel(reverse, has_prev, nsub, *refs):
    if has_prev:
        (q_ref, k_ref, v_ref, g1_ref, w2_ref, gb_ref, prev_ref, r_ref, ng_ref, o_ref, s_ref) = refs
    else:
        (q_ref, k_ref, v_ref, g1_ref, w2_ref, gb_ref, o_ref, s_ref) = refs

    @pl.when(pl.program_id(1) == 0)
    def _():
        s_ref[...] = jnp.zeros_like(s_ref)

    cs = GLA_CS
    row = lax.broadcasted_iota(jnp.int32, (cs, cs), 0)
    col = lax.broadcasted_iota(jnp.int32, (cs, cs), 1)
    keep = (col >= row) if reverse else (col <= row)
    cum_m = keep.astype(BF16)
    z = _dft_dot(g1_ref[0], w2_ref[0]) + gb_ref[0]
    log_gate = (jnp.minimum(z, 0.0) - jnp.log(1.0 + jnp.exp(-jnp.abs(z)))) * (1.0 / GLA_TAU)
    q_all = q_ref[0] * (GLA_HEAD_K ** -0.5)
    k_all = k_ref[0]
    v_all = v_ref[0]
    last = 0 if reverse else cs - 1
    states = [s_ref[h] for h in range(GLA_HEADS)]
    for c in (range(nsub - 1, -1, -1) if reverse else range(nsub)):
        rs = slice(c * cs, (c + 1) * cs)
        lg_hi, lg_lo = _split_bf16(log_gate[rs])
        b_all = jnp.dot(cum_m, lg_hi, preferred_element_type=F32) + jnp.dot(cum_m, lg_lo, preferred_element_type=F32)
        for h in range(GLA_HEADS):
            ks = slice(h * GLA_HEAD_K, (h + 1) * GLA_HEAD_K)
            vs = slice(h * GLA_HEAD_V, (h + 1) * GLA_HEAD_V)
            b = b_all[:, ks]
            b_last = b[last:last + 1]
            ref = 0.5 * b_last
            qc, kc, vc = q_all[rs, ks], k_all[rs, ks], v_all[rs, vs].astype(BF16)
            q_in = (qc * jnp.exp(jnp.minimum(b - ref, EXP_CLAMP))).astype(BF16)
            k_in = (kc * jnp.exp(jnp.minimum(ref - b, EXP_CLAMP))).astype(BF16)
            att = lax.dot_general(q_in, k_in, (((1,), (1,)), ((), ())), preferred_element_type=F32)
            att = jnp.where(keep, att, 0.0).astype(BF16)
            o_h = jnp.dot(att, vc, preferred_element_type=F32)
            o_h = o_h + lax.dot_general((qc * jnp.exp(b)).astype(BF16), states[h].astype(BF16),
                                        (((1,), (1,)), ((), ())), preferred_element_type=F32)
            k_tail = (kc * jnp.exp(b_last - b)).astype(BF16)
            upd = lax.dot_general(vc, k_tail, (((0,), (0,)), ((), ())), preferred_element_type=F32)
            states[h] = jnp.exp(b_last) * states[h] + upd
            if has_prev:
                o_h = o_h + prev_ref[0, rs, vs]
                o_h = o_h * lax.rsqrt(jnp.mean(o_h * o_h, axis=-1, keepdims=True) + EPS) * ng_ref[:, vs]
                o_h = o_h * _silu(r_ref[0, rs, vs])
            o_ref[0, rs, vs] = o_h.astype(o_ref.dtype)
    for h in range(GLA_HEADS):
        s_ref[h] = states[h]


def _gla_direction(u_all, w2pad, gbias, n_lat, reverse, prev=None, norm_g=None):
    bsz, t, _ = u_all.shape
    n = t // ROW_TILE
    nl = n_lat // ROW_TILE
    nsub = ROW_TILE // GLA_CS
    if reverse:
        blk = lambda s: n - 1 - s
    else:
        blk = lambda s: (s + nl) % n
    q0, k0, v0, r0, g0 = (AB_SIZES[0] // GLA_DK, (AB_SIZES[0] + GLA_DK) // GLA_DK, (AB_SIZES[0] + 2 * GLA_DK) // GLA_DV,
                          (AB_SIZES[0] + 2 * GLA_DK + GLA_DV) // GLA_DV, (AB_IN - 2 * GLA_RANK) // LANE)
    d = 1 if reverse else 0
    in_specs = [pl.BlockSpec((1, ROW_TILE, GLA_DK), lambda b, s: (b, blk(s), q0)),
                pl.BlockSpec((1, ROW_TILE, GLA_DK), lambda b, s: (b, blk(s), k0)),
                pl.BlockSpec((1, ROW_TILE, GLA_DV), lambda b, s: (b, blk(s), v0)),
                pl.BlockSpec((1, ROW_TILE, LANE), lambda b, s: (b, blk(s), g0)),
                pl.BlockSpec((1, LANE, GLA_DK), lambda b, s: (d, 0, 0)),
                pl.BlockSpec((1, 1, GLA_DK), lambda b, s: (d, 0, 0))]
    args = [u_all, u_all, u_all, u_all, w2pad, gbias]
    has_prev = prev is not None
    if has_prev:
        in_specs += [pl.BlockSpec((1, ROW_TILE, GLA_DV), lambda b, s: (b, blk(s), 0)),
                     pl.BlockSpec((1, ROW_TILE, GLA_DV), lambda b, s: (b, blk(s), r0)),
                     pl.BlockSpec((1, GLA_DV), lambda b, s: (0, 0))]
        args += [prev, u_all, norm_g.reshape(1, GLA_DV)]
    return pl.pallas_call(
        functools.partial(_gla_kernel, reverse, has_prev, nsub),
        grid=(bsz, n),
        in_specs=in_specs,
        out_specs=pl.BlockSpec((1, ROW_TILE, GLA_DV), lambda b, s: (b, blk(s), 0)),
        out_shape=jax.ShapeDtypeStruct((bsz, t, GLA_DV), BF16 if has_prev else F32),
        scratch_shapes=[pltpu.VMEM((GLA_HEADS, GLA_HEAD_V, GLA_HEAD_K), F32)],
        compiler_params=_cparams("parallel", "arbitrary"),
        name="gla_bwd_norm" if reverse else "gla_fwd",
    )(*args)


def _gla(u_all, gate_w2, gate_b, norm_g, n_lat):
    w2pad = jnp.zeros((2, LANE, GLA_DK), F32)
    w2pad = w2pad.at[0, :GLA_RANK].set(gate_w2[0]).at[1, GLA_RANK:2 * GLA_RANK].set(gate_w2[1])
    gbias = gate_b.reshape(2, 1, GLA_DK).astype(F32)
    o_f = _gla_direction(u_all, w2pad, gbias, n_lat, False)
    return _gla_direction(u_all, w2pad, gbias, n_lat, True, prev=o_f, norm_g=norm_g.astype(F32))


CONV_CHUNK = 512


def _shift_rows(v, dh):
    return v if dh == 0 else pltpu.roll(v, (-dh) % v.shape[0], 0)


def _ssd_conv_kernel(n_lat, x_ref, w_ref, b_ref, o_ref, pad_ref):
    t = x_ref.shape[1]
    n_ctx = t - n_lat
    pad_ref[0:GRID_W, :] = jnp.zeros((GRID_W, LANE), F32)
    pad_ref[GRID_W + n_lat:2 * GRID_W + n_lat, :] = jnp.zeros((GRID_W, LANE), F32)
    pad_ref[GRID_W:GRID_W + n_lat, :] = x_ref[0, 0:n_lat, :]
    w = w_ref[...]
    bias = b_ref[...]
    col = lax.broadcasted_iota(jnp.int32, (CONV_CHUNK, LANE), 0) % GRID_W
    masks = {-1: col >= 1, 0: None, 1: col <= GRID_W - 2}

    def body(ci, carry):
        t0 = pl.multiple_of(ci * CONV_CHUNK, CONV_CHUNK)
        slabs = [pad_ref[pl.ds(t0 + i * GRID_W, CONV_CHUNK), :] for i in range(3)]
        part = [sum(slabs[i] * w[i * 3 + j:i * 3 + j + 1] for i in range(3)) for j in range(3)]
        acc = part[1] + bias
        acc = acc + jnp.where(masks[-1], _shift_rows(part[0], -1), 0.0)
        acc = acc + jnp.where(masks[1], _shift_rows(part[2], 1), 0.0)
        o_ref[0, 0, pl.ds(t0, CONV_CHUNK), :] = _silu(acc)
        return carry

    lax.fori_loop(0, n_lat // CONV_CHUNK, body, 0)
    xc = x_ref[0, n_lat:t, :]
    pos = lax.broadcasted_iota(jnp.int32, (n_ctx, LANE), 0)
    acc = xc * w[4:5] + bias
    acc = acc + jnp.where(pos >= 1, _shift_rows(xc, -1), 0.0) * w[3:4]
    acc = acc + jnp.where(pos <= n_ctx - 2, _shift_rows(xc, 1), 0.0) * w[5:6]
    o_ref[0, 0, n_lat:t, :] = _silu(acc)


def _ssd_conv(u2, conv_w, conv_b, n_lat):
    bsz, t, _ = u2.shape
    n_tiles = SSD_CONV_DIM // LANE
    c0 = SSD_D_INNER // LANE
    return pl.pallas_call(
        functools.partial(_ssd_conv_kernel, n_lat),
        grid=(bsz, n_tiles),
        in_specs=[pl.BlockSpec((1, t, LANE), lambda b, j: (b, 0, c0 + j)),
                  pl.BlockSpec((9, LANE), lambda b, j: (0, j)),
                  pl.BlockSpec((1, LANE), lambda b, j: (0, j))],
        out_specs=pl.BlockSpec((1, 1, t, LANE), lambda b, j: (b, j, 0, 0)),
        out_shape=jax.ShapeDtypeStruct((bsz, n_tiles, t, LANE), F32),
        scratch_shapes=[pltpu.VMEM((n_lat + 2 * GRID_W, LANE), F32)],
        compiler_params=_cparams("parallel", "parallel"),
        name="ssd_conv",
    )(u2, conv_w.reshape(9, SSD_CONV_DIM), conv_b.reshape(1, SSD_CONV_DIM))


SSD_CS = 128


def _softplus(x):
    return jnp.maximum(x, 0.0) + jnp.log(1.0 + jnp.exp(-jnp.abs(x)))


def _spread_dot(v, onehot):
    hi = v.astype(BF16)
    lo = (v - hi.astype(F32)).astype(BF16)
    return jnp.dot(hi, onehot, preferred_element_type=F32) + jnp.dot(lo, onehot, preferred_element_type=F32)


def _ssd_kernel(reverse, has_prev, nsub, *refs):
    if has_prev:
        (x_ref, b_ref, c_ref, dt_ref, p_ref, expq_ref, dsk_ref, prev_ref, z_ref, ng_ref, o_ref,
         s_ref, at_ref, dtt_ref, acs_ref, eag_ref, twg_ref) = refs
    else:
        (x_ref, b_ref, c_ref, dt_ref, p_ref, expq_ref, dsk_ref, o_ref,
         s_ref, at_ref, dtt_ref, acs_ref, eag_ref, twg_ref) = refs

    @pl.when(pl.program_id(1) == 0)
    def _():
        s_ref[...] = jnp.zeros_like(s_ref)

    cs = SSD_CS
    nh = SSD_HEADS
    tpg = SSD_HPG * SSD_HEADDIM // LANE
    gw = SSD_HPG * SSD_HEADDIM
    row = lax.broadcasted_iota(jnp.int32, (cs, cs), 0)
    col = lax.broadcasted_iota(jnp.int32, (cs, cs), 1)
    keep = (col >= row) if reverse else (col <= row)
    cum_m = keep.astype(BF16)
    last = 0 if reverse else cs - 1
    lane = lax.broadcasted_iota(jnp.int32, (cs, LANE), 1)
    d0 = nh if reverse else 0
    p = p_ref[...]
    dtv = _softplus(dt_ref[0][:, d0:d0 + nh] + p[0:1, d0:d0 + nh])
    a_all = dtv * p[1:2, d0:d0 + nh]
    for c in (range(nsub - 1, -1, -1) if reverse else range(nsub)):
        rs = slice(c * cs, (c + 1) * cs)
        a = a_all[rs]
        hi = a.astype(BF16)
        r1 = a - hi.astype(F32)
        mid = r1.astype(BF16)
        lo = (r1 - mid.astype(F32)).astype(BF16)
        acs = (jnp.dot(cum_m, hi, preferred_element_type=F32) + jnp.dot(cum_m, mid, preferred_element_type=F32)
               + jnp.dot(cum_m, lo, preferred_element_type=F32))
        a_last = acs[last:last + 1]
        acs_ref[...] = acs
        e_acs = jnp.exp(acs)
        tail_w = jnp.exp(a_last - acs) * dtv[rs]
        for gg in range(SSD_GROUPS):
            eag_ref[gg] = e_acs[:, gg * SSD_HPG:(gg + 1) * SSD_HPG]
            twg_ref[gg] = tail_w[:, gg * SSD_HPG:(gg + 1) * SSD_HPG]
        at_ref[...] = acs.T
        dtt_ref[...] = dtv[rs].T

        def group(g, carry):
            h0 = pl.multiple_of(g * SSD_HPG, SSD_HPG)
            bg = b_ref[0, g, rs, :].astype(BF16)
            cg = c_ref[0, g, rs, :].astype(BF16)
            cb = lax.dot_general(cg, bg, (((1,), (1,)), ((), ())), preferred_element_type=F32)
            colb = _spread_dot(acs_ref[...], expq_ref[g])
            rows8 = at_ref[pl.ds(h0, SSD_HPG), :]
            dt8 = dtt_ref[pl.ds(h0, SSD_HPG), :]
            ms = []
            for r in range(SSD_HPG):
                diff = colb[:, r * LANE:(r + 1) * LANE] - rows8[r:r + 1, :]
                decay = jnp.where(keep, jnp.exp(jnp.minimum(diff, 0.0)), 0.0)
                ms.append((cb * decay * dt8[r:r + 1, :]).astype(BF16))
            ea_g = eag_ref[g]
            tw_g = twg_ref[g]
            state = s_ref[g]
            inter = jnp.dot(cg, state.astype(BF16), preferred_element_type=F32)
            xs = [x_ref[0, g * tpg + k, rs, :] for k in range(tpg)]
            dsk = dsk_ref[g]
            first_head = lane < SSD_HEADDIM
            ys, xws, e_last = [], [], []
            for k in range(tpg):
                bd = jnp.concatenate([jnp.where(first_head, xs[k], 0.0),
                                      jnp.where(first_head, 0.0, xs[k])], axis=0).astype(BF16)
                m2 = jnp.concatenate(ms[2 * k:2 * k + 2], axis=1)
                sl = slice(k * LANE, (k + 1) * LANE)
                e_t = jnp.where(first_head, ea_g[:, 2 * k:2 * k + 1], ea_g[:, 2 * k + 1:2 * k + 2])
                w_t = jnp.where(first_head, tw_g[:, 2 * k:2 * k + 1], tw_g[:, 2 * k + 1:2 * k + 2])
                y = jnp.dot(m2, bd, preferred_element_type=F32)
                ys.append(y + e_t * inter[:, sl] + dsk[:, sl] * xs[k])
                xws.append((xs[k] * w_t).astype(BF16))
                e_last.append(e_t[last:last + 1])
            upd = lax.dot_general(bg, jnp.concatenate(xws, axis=1), (((0,), (0,)), ((), ())),
                                  preferred_element_type=F32)
            s_ref[g] = jnp.concatenate(e_last, axis=1) * state + upd
            if has_prev:
                off = pl.multiple_of(g * gw, gw)
                z = z_ref[0, rs, pl.ds(off, gw)]
                yt = [(ys[k] + prev_ref[0, g * tpg + k, rs, :]) * _silu(z[:, k * LANE:(k + 1) * LANE])
                      for k in range(tpg)]
                ss = sum(jnp.sum(v * v, axis=-1, keepdims=True) for v in yt)
                inv = lax.rsqrt(ss * (1.0 / gw) + EPS)
                ng = ng_ref[:, pl.ds(off, gw)]
                o_ref[0, rs, pl.ds(off, gw)] = (jnp.concatenate(yt, axis=1) * inv * ng).astype(o_ref.dtype)
            else:
                for k in range(tpg):
                    o_ref[0, g * tpg + k, rs, :] = ys[k]
            return carry

        lax.fori_loop(0, SSD_GROUPS, group, 0)


def _ssd_direction(xbc_t, u2, pvec, exp_q, dsk, n_lat, reverse, prev=None, norm_g=None):
    bsz, _, t, _ = xbc_t.shape
    n = t // ROW_TILE
    nl = n_lat // ROW_TILE
    nsub = ROW_TILE // SSD_CS
    nx = SSD_D_INNER // LANE
    gw = SSD_HPG * SSD_HEADDIM
    if reverse:
        blk = lambda s: n - 1 - s
    else:
        blk = lambda s: (s + nl) % n
    d = 1 if reverse else 0
    full = lambda a: pl.BlockSpec(a.shape, lambda b, s: (0,) * a.ndim)
    in_specs = [pl.BlockSpec((1, nx, ROW_TILE, LANE), lambda b, s: (b, 0, blk(s), 0)),
                pl.BlockSpec((1, SSD_GROUPS, ROW_TILE, LANE), lambda b, s: (b, nx // SSD_GROUPS, blk(s), 0)),
                pl.BlockSpec((1, SSD_GROUPS, ROW_TILE, LANE), lambda b, s: (b, nx // SSD_GROUPS + 1, blk(s), 0)),
                pl.BlockSpec((1, ROW_TILE, LANE), lambda b, s: (b, blk(s), (SSD_IN - 2 * SSD_HEADS) // LANE)),
                full(pvec), full(exp_q),
                pl.BlockSpec((SSD_GROUPS, 1, gw), lambda b, s: (d, 0, 0))]
    args = [xbc_t, xbc_t, xbc_t, u2, pvec, exp_q, dsk.reshape(2 * SSD_GROUPS, 1, gw)]
    has_prev = prev is not None
    if has_prev:
        in_specs += [pl.BlockSpec((1, nx, ROW_TILE, LANE), lambda b, s: (b, 0, blk(s), 0)),
                     pl.BlockSpec((1, ROW_TILE, SSD_D_INNER), lambda b, s: (b, blk(s), 0)),
                     pl.BlockSpec((1, SSD_D_INNER), lambda b, s: (0, 0))]
        args += [prev, u2, norm_g.reshape(1, SSD_D_INNER)]
        out_spec = pl.BlockSpec((1, ROW_TILE, SSD_D_INNER), lambda b, s: (b, blk(s), 0))
        out_shape = jax.ShapeDtypeStruct((bsz, t, SSD_D_INNER), BF16)
    else:
        out_spec = pl.BlockSpec((1, nx, ROW_TILE, LANE), lambda b, s: (b, 0, blk(s), 0))
        out_shape = jax.ShapeDtypeStruct((bsz, nx, t, LANE), F32)
    return pl.pallas_call(
        functools.partial(_ssd_kernel, reverse, has_prev, nsub),
        grid=(bsz, n),
        in_specs=in_specs,
        out_specs=out_spec,
        out_shape=out_shape,
        scratch_shapes=[pltpu.VMEM((SSD_GROUPS, SSD_STATE, gw), F32),
                        pltpu.VMEM((SSD_HEADS, SSD_CS), F32), pltpu.VMEM((SSD_HEADS, SSD_CS), F32),
                        pltpu.VMEM((SSD_CS, SSD_HEADS), F32),
                        pltpu.VMEM((SSD_GROUPS, SSD_CS, SSD_HPG), F32),
                        pltpu.VMEM((SSD_GROUPS, SSD_CS, SSD_HPG), F32)],
        compiler_params=_cparams("parallel", "arbitrary"),
        name="ssd_bwd_norm" if reverse else "ssd_fwd",
    )(*args)


def _ssd(xbc_t, u2, dt_bias, a_log, d_skip, norm_g, n_lat):
    gw = SSD_HPG * SSD_HEADDIM
    pvec = jnp.zeros((8, LANE), F32)
    pvec = pvec.at[0].set(dt_bias.reshape(-1)).at[1].set(-jnp.exp(a_log.astype(F32)).reshape(-1))
    exp_q = np.zeros((SSD_GROUPS, SSD_HEADS, SSD_HPG * LANE), np.float32)
    for gg in range(SSD_GROUPS):
        for r in range(SSD_HPG):
            exp_q[gg, gg * SSD_HPG + r, r * LANE:(r + 1) * LANE] = 1.0
    exp_q = jnp.asarray(exp_q).astype(BF16)
    dsk = jnp.repeat(d_skip.astype(F32).reshape(2, SSD_GROUPS, SSD_HPG), SSD_HEADDIM, axis=-1)
    y_f = _ssd_direction(xbc_t, u2, pvec, exp_q, dsk, n_lat, False)
    return _ssd_direction(xbc_t, u2, pvec, exp_q, dsk, n_lat, True, prev=y_f, norm_g=norm_g.astype(F32))


HY_PAD = 8


def _hy_conv_kernel(n_lat, x_ref, w_ref, b_ref, o_ref, pad_ref):
    t = x_ref.shape[1]
    w = w_ref[...]
    bias = b_ref[...]
    ch = CONV_CHUNK
    pad_ref[0:HY_PAD, :] = jnp.zeros((HY_PAD, LANE), F32)
    pad_ref[HY_PAD + n_lat:2 * HY_PAD + n_lat, :] = jnp.zeros((HY_PAD, LANE), F32)
    pad_ref[HY_PAD:HY_PAD + n_lat, :] = x_ref[0, 0:n_lat, :]
    rowi = lax.broadcasted_iota(jnp.int32, (ch, LANE), 0)

    def body(ci, carry):
        t0 = pl.multiple_of(ci * ch, ch)
        cur = pad_ref[pl.ds(t0 + HY_PAD, ch), :]
        before = pad_ref[pl.ds(t0, HY_PAD), :][HY_PAD - 1:HY_PAD]
        after = pad_ref[pl.ds(t0 + HY_PAD + ch, HY_PAD), :][0:1]
        down = jnp.where(rowi == 0, before, _shift_rows(cur, -1))
        up = jnp.where(rowi == ch - 1, after, _shift_rows(cur, 1))
        o_ref[0, 0, pl.ds(t0, ch), :] = down * w[0:1] + cur * w[1:2] + up * w[2:3] + bias
        return carry

    lax.fori_loop(0, n_lat // ch, body, 0)
    n_ctx = t - n_lat
    xc = x_ref[0, n_lat:t, :]
    pos = lax.broadcasted_iota(jnp.int32, (n_ctx, LANE), 0)
    acc = xc * w[1:2] + bias
    acc = acc + jnp.where(pos >= 1, _shift_rows(xc, -1), 0.0) * w[0:1]
    acc = acc + jnp.where(pos <= n_ctx - 2, _shift_rows(xc, 1), 0.0) * w[2:3]
    o_ref[0, 0, n_lat:t, :] = acc


def _hy_conv(u_all, conv_w, conv_b, n_lat):
    bsz, t, _ = u_all.shape
    nch = (HY_ORDER + 1) * HY_WIDTH
    tpp = HY_WIDTH // LANE
    return pl.pallas_call(
        functools.partial(_hy_conv_kernel, n_lat),
        grid=(bsz, nch // LANE),
        in_specs=[pl.BlockSpec((1, t, LANE), lambda b, j: (b, 0, j)),
                  pl.BlockSpec((3, LANE), lambda b, j: (0, j)),
                  pl.BlockSpec((1, LANE), lambda b, j: (0, j))],
        out_specs=pl.BlockSpec((1, 1, t, LANE), lambda b, j: (j // tpp, b, 0, j % tpp)),
        out_shape=jax.ShapeDtypeStruct((HY_ORDER + 1, bsz, t, HY_WIDTH), F32),
        scratch_shapes=[pltpu.VMEM((n_lat + 2 * HY_PAD, LANE), F32)],
        compiler_params=_cparams("parallel", "parallel"),
        name="hy_short_conv",
    )(u_all, conv_w, conv_b.reshape(1, nch))


FILT_ROWS = 256


def _hy_filter_kernel(seq_len, bands_ref, w1t_ref, b1_ref, fr1_ref, w2t_ref, b2_ref, fr2_ref, w3a_ref, w3b_ref,
                      dl_ref, o_ref):
    i = pl.program_id(0)

    def lag_of(shape, axis):
        n = i * FILT_ROWS + lax.broadcasted_iota(jnp.int32, shape, axis)
        return n, jnp.where(n < seq_len, n, 2 * seq_len - n).astype(F32)

    _, pos_r = lag_of((1, FILT_ROWS), 1)
    tt_r = pos_r / max(seq_len - 1, 1)
    ang = 2.0 * math.pi * bands_ref[...] * pos_r / seq_len
    w1t = w1t_ref[...]
    nb = HY_BANDS
    hp = functools.partial(jnp.dot, precision=HIGHEST, preferred_element_type=F32)
    pre = w1t[:, 0:1] * tt_r + hp(w1t[:, 1:1 + nb], jnp.cos(ang)) - hp(w1t[:, 1 + nb:1 + 2 * nb], jnp.sin(ang))
    hid = jnp.sin(fr1_ref[...] * (pre + b1_ref[...]))
    hid = jnp.sin(fr2_ref[...] * (hp(w2t_ref[...], hid) + b2_ref[...]))
    n_c, pos_c = lag_of((FILT_ROWS, 1), 0)
    decay = jnp.exp(-(pos_c / max(seq_len - 1, 1)) * dl_ref[...])
    hid_rows = hid.T.astype(BF16)
    for o, w3_ref in enumerate((w3a_ref, w3b_ref)):
        h = jnp.dot(hid_rows, w3_ref[...].astype(BF16), preferred_element_type=F32)
        o_ref[o] = jnp.where(n_c == seq_len, 0.0, h * decay)


def _hy_filter(seq_len, f_w1, f_b1, f_fr1, f_w2, f_b2, f_fr2, f_w3):
    assert HY_ORDER == 2
    fh = f_w1.shape[1]
    emb = f_w1.shape[0]
    half = seq_len // FILT_ROWS
    bands = jnp.asarray(np.linspace(1e-4, HY_BANDS - 1, HY_BANDS, dtype=np.float32)).reshape(HY_BANDS, 1)
    deltas = jnp.asarray(np.abs(np.linspace(HY_MIN_DECAY, HY_MAX_DECAY, HY_WIDTH, dtype=np.float32))).reshape(1, HY_WIDTH)
    col = lambda v: v.reshape(fh, 1).astype(F32)
    full = lambda shape: pl.BlockSpec(shape, lambda i: (0,) * len(shape))
    w3 = f_w3.astype(F32)
    return pl.pallas_call(
        functools.partial(_hy_filter_kernel, seq_len),
        grid=(2 * half,),
        in_specs=[full((HY_BANDS, 1)), full((fh, emb)), full((fh, 1)), full((fh, 1)), full((fh, fh)), full((fh, 1)),
                  full((fh, 1)),
                  pl.BlockSpec((fh, HY_WIDTH), lambda i: (0, i // half)),
                  pl.BlockSpec((fh, HY_WIDTH), lambda i: (0, 2 + i // half)),
                  full((1, HY_WIDTH))],
        out_specs=pl.BlockSpec((HY_ORDER, FILT_ROWS, HY_WIDTH), lambda i: (0, i, 0)),
        out_shape=jax.ShapeDtypeStruct((HY_ORDER, 2 * seq_len, HY_WIDTH), F32),
        compiler_params=_cparams("parallel"),
        name="hy_filter",
    )(bands, f_w1.astype(F32).T, col(f_b1), col(f_fr1), f_w2.astype(F32).T, col(f_b2), col(f_fr2), w3, w3, deltas)


DFT_N2 = 256
DFT_S = 8


def _dft_tables(n1):
    n = n1 * DFT_N2
    k1h = n1 // 2 + 1
    k1p = -(-k1h // 8) * 8
    k1 = np.arange(k1p)[:, None].astype(np.float64)
    valid = (np.arange(k1p) < k1h)[:, None]
    th1 = 2.0 * np.pi * k1 * np.arange(n1)[None, :] / n1
    f1 = np.concatenate([np.where(valid, np.cos(th1), 0.0), np.where(valid, -np.sin(th1), 0.0)], axis=0)
    tw = 2.0 * np.pi * k1 * np.arange(DFT_N2)[None, :] / n
    tw_re = np.repeat(np.where(valid, np.cos(tw), 0.0)[:, :, None], LANE, axis=2)
    tw_im = np.repeat(np.where(valid, -np.sin(tw), 0.0)[:, :, None], LANE, axis=2)
    ph = 2.0 * np.pi * np.outer(np.arange(DFT_N2), np.arange(DFT_N2)) / DFT_N2
    f2re, f2im = np.cos(ph), -np.sin(ph)
    w_fwd = np.block([[f2re, -f2im], [f2im, f2re]])
    w_inv = np.block([[f2re, f2im], [-f2im, f2re]])
    wgt = np.where((np.arange(k1p) == 0) | (np.arange(k1p) == n1 // 2), 1.0, 2.0) * (np.arange(k1p) < k1h)
    th_i = 2.0 * np.pi * np.arange(n1 // 2)[:, None] * np.arange(k1p)[None, :] / n1
    g = np.concatenate([wgt * np.cos(th_i), -wgt * np.sin(th_i)], axis=1) / n
    f = lambda a: jnp.asarray(a.astype(np.float32))
    return dict(k1h=k1h, k1p=k1p, f1=f(f1), tw_re=f(tw_re), tw_im=f(tw_im), w_fwd=f(w_fwd), w_inv=f(w_inv), g=f(g))


def _split_bf16(v):
    hi = v.astype(BF16)
    return hi, (v - hi.astype(F32)).astype(BF16)


def _dft_dot_bf16(a, b):
    return jnp.dot(a.astype(BF16), b.astype(BF16), preferred_element_type=F32)


def _dft_dot(a, b):
    ah, al = _split_bf16(a)
    bh, bl = _split_bf16(b)
    out = jnp.dot(ah, bh, preferred_element_type=F32)
    out = out + jnp.dot(ah, bl, preferred_element_type=F32)
    return out + jnp.dot(al, bh, preferred_element_type=F32)


def _dft_first_kernel(k1p, x_ref, f1_ref, twr_ref, twi_ref, ore_ref, oim_ref):
    _, rows, s, cw = x_ref.shape
    reps = cw // LANE
    x = x_ref[0].reshape(rows * s, cw).astype(BF16)
    a = jnp.dot(f1_ref[...], x, preferred_element_type=F32)
    are = a[:k1p * s].reshape(k1p, s, cw)
    aim = a[k1p * s:].reshape(k1p, s, cw)
    tr = jnp.concatenate([twr_ref[...]] * reps, axis=2)
    ti = jnp.concatenate([twi_ref[...]] * reps, axis=2)
    ore_ref[0] = are * tr - aim * ti
    oim_ref[0] = are * ti + aim * tr


def _dft_first(x4, lead0, nlead, f1, tab):
    rows = f1.shape[1]
    cw = x4.shape[3]
    k1p = tab['k1p']
    out = jax.ShapeDtypeStruct((nlead, k1p, DFT_N2, cw), F32)
    ospec = pl.BlockSpec((1, k1p, DFT_S, cw), lambda b, j: (b, 0, j, 0))
    tspec = pl.BlockSpec((k1p, DFT_S, LANE), lambda b, j: (0, j, 0))
    return pl.pallas_call(
        functools.partial(_dft_first_kernel, k1p),
        grid=(nlead, DFT_N2 // DFT_S),
        in_specs=[pl.BlockSpec((1, rows, DFT_S, cw), lambda b, j: (lead0 + b, 0, j, 0)),
                  pl.BlockSpec((2 * k1p * DFT_S, rows * DFT_S), lambda b, j: (0, 0)), tspec, tspec],
        out_specs=[ospec, ospec],
        out_shape=[out, out],
        compiler_params=_cparams("parallel", "parallel"),
        name="hy_dft_first",
    )(x4, jnp.kron(f1, jnp.eye(DFT_S, dtype=F32)).astype(BF16), tab['tw_re'], tab['tw_im'])


DFT_TC = 512


def _dft_mid_kernel(k1h, fused, *refs):
    if fused:
        are_ref, aim_ref, hre_ref, him_ref, wf_ref, wi_ref, twr_ref, twi_ref, ore_ref, oim_ref = refs
    else:
        are_ref, aim_ref, wf_ref, ore_ref, oim_ref = refs
    n2 = DFT_N2

    @pl.when(pl.program_id(0) < k1h)
    def _():
        a = jnp.concatenate([are_ref[0, 0], aim_ref[0, 0]], axis=0)
        dot = _dft_dot_bf16
        x = dot(wf_ref[...], a)
        xre, xim = x[:n2], x[n2:]
        if not fused:
            ore_ref[0, 0] = xre
            oim_ref[0, 0] = xim
        else:
            hre, him = hre_ref[0, 0], him_ref[0, 0]
            y = jnp.concatenate([xre * hre - xim * him, xre * him + xim * hre], axis=0)
            bb = dot(wi_ref[...], y)
            bre, bim = bb[:n2], bb[n2:]
            twr, twi = twr_ref[0], twi_ref[0]
            for c in range(bre.shape[1] // LANE):
                sl = slice(c * LANE, (c + 1) * LANE)
                ore_ref[0, 0, :, sl] = bre[:, sl] * twr + bim[:, sl] * twi
                oim_ref[0, 0, :, sl] = bim[:, sl] * twr - bre[:, sl] * twi

    @pl.when(pl.program_id(0) >= k1h)
    def _():
        ore_ref[...] = jnp.zeros_like(ore_ref)
        oim_ref[...] = jnp.zeros_like(oim_ref)


def _dft_mid(are, aim, tab, spec=None, order=0):
    bsz, k1p, _, cw = are.shape
    tc = min(DFT_TC, cw)
    blk = pl.BlockSpec((1, 1, DFT_N2, tc), lambda k, c, b: (b, k, 0, c))
    wspec = pl.BlockSpec((2 * DFT_N2, 2 * DFT_N2), lambda k, c, b: (0, 0))
    fused = spec is not None
    if fused:
        hspec = pl.BlockSpec((1, 1, DFT_N2, tc), lambda k, c, b: (order, k, 0, c))
        tspec = pl.BlockSpec((1, DFT_N2, LANE), lambda k, c, b: (k, 0, 0))
        in_specs = [blk, blk, hspec, hspec, wspec, wspec, tspec, tspec]
        args = [are, aim, spec[0], spec[1], tab['w_fwd'].astype(BF16), tab['w_inv'].astype(BF16),
                tab['tw_re'], tab['tw_im']]
    else:
        in_specs = [blk, blk, wspec]
        args = [are, aim, tab['w_fwd']]
    out = jax.ShapeDtypeStruct((bsz, k1p, DFT_N2, cw), F32)
    ore, oim = pl.pallas_call(
        functools.partial(_dft_mid_kernel, tab['k1h'], fused),
        grid=(k1p, cw // tc, bsz),
        in_specs=in_specs,
        out_specs=[blk, blk],
        out_shape=[out, out],
        compiler_params=_cparams("parallel", "parallel", "arbitrary"),
        name="hy_dft_mid_conv" if fused else "hy_dft_mid_filter",
    )(*args)
    return ore, oim


def _dft_last_kernel(bre_ref, bim_ref, g_ref, u_ref, x_ref, d_ref, o_ref):
    _, k1p, s, cw = bre_ref.shape
    rows = u_ref.shape[1]
    bb = jnp.concatenate([bre_ref[0].reshape(k1p * s, cw), bim_ref[0].reshape(k1p * s, cw)], axis=0).astype(BF16)
    y = jnp.dot(g_ref[...], bb, preferred_element_type=F32).reshape(rows, s, cw)
    o_ref[0, 0:rows] = x_ref[0] * (y + u_ref[0] * d_ref[...])
    if o_ref.shape[1] > rows:
        o_ref[0, rows:, :, :] = jnp.zeros((o_ref.shape[1] - rows,) + o_ref.shape[2:], F32)


def _dft_last(bre, bim, tab, u4, u_lead0, x4, x_lead0, dvec, out_rows):
    bsz, k1p, _, cw = bre.shape
    rows = tab['g'].shape[0]
    bspec = pl.BlockSpec((1, k1p, DFT_S, cw), lambda b, j: (b, 0, j, 0))
    return pl.pallas_call(
        _dft_last_kernel,
        grid=(bsz, DFT_N2 // DFT_S),
        in_specs=[bspec, bspec, pl.BlockSpec((rows * DFT_S, 2 * k1p * DFT_S), lambda b, j: (0, 0)),
                  pl.BlockSpec((1, rows, DFT_S, cw), lambda b, j: (u_lead0 + b, 0, j, 0)),
                  pl.BlockSpec((1, rows, DFT_S, cw), lambda b, j: (x_lead0 + b, 0, j, 0)),
                  pl.BlockSpec((1, 1, cw), lambda b, j: (0, 0, 0))],
        out_specs=pl.BlockSpec((1, out_rows, DFT_S, cw), lambda b, j: (b, 0, j, 0)),
        out_shape=jax.ShapeDtypeStruct((bsz, out_rows, DFT_N2, cw), F32),
        compiler_params=_cparams("parallel", "parallel"),
        name="hy_dft_last",
    )(bre, bim, jnp.kron(tab['g'], jnp.eye(DFT_S, dtype=F32)).astype(BF16), u4, x4,
      dvec.reshape(1, 1, cw).astype(F32))


def _ctx_dft_tables(n):
    size = 2 * n
    kp = -(-(n + 1) // 8) * 8
    k = np.arange(kp)[:, None].astype(np.float64)
    valid = (np.arange(kp) <= n)[:, None]
    th = 2.0 * np.pi * k * np.arange(size)[None, :] / size
    fw = np.concatenate([np.where(valid, np.cos(th), 0.0), np.where(valid, -np.sin(th), 0.0)], axis=0)
    wgt = np.where((np.arange(kp) == 0) | (np.arange(kp) == n), 1.0, 2.0) * (np.arange(kp) <= n)
    thi = 2.0 * np.pi * np.arange(n)[:, None] * np.arange(kp)[None, :] / size
    inv = np.concatenate([wgt * np.cos(thi), -wgt * np.sin(thi)], axis=1) / size
    return kp, jnp.asarray(fw.astype(np.float32)), jnp.asarray(inv.astype(np.float32))


def _hy_ctx_kernel(kp, p_ref, f_ref, d_ref, fw_ref, inv_ref, y_ref, o_ref):
    del y_ref
    n = p_ref.shape[2]
    fw = fw_ref[...]
    y = p_ref[0, 0]
    for o in range(HY_ORDER):
        h = _dft_dot(fw, f_ref[o])
        x = _dft_dot(fw[:, :n], y)
        hre, him, xre, xim = h[:kp], h[kp:], x[:kp], x[kp:]
        prod = jnp.concatenate([xre * hre - xim * him, xre * him + xim * hre], axis=0)
        conv = _dft_dot(inv_ref[...], prod)
        y = p_ref[o + 1, 0] * (conv + y * d_ref[o:o + 1, :])
    o_ref[0] = y


CTX_TC = 256


def _hy_ctx(parts, filt_c, long_bias, y_all, n_lat):
    nparts, bsz, t, cw = parts.shape
    n_ctx = t - n_lat
    kp, fw, inv = _ctx_dft_tables(n_ctx)
    return pl.pallas_call(
        functools.partial(_hy_ctx_kernel, kp),
        grid=(bsz, cw // CTX_TC),
        in_specs=[pl.BlockSpec((nparts, 1, n_ctx, CTX_TC), lambda b, c: (0, b, n_lat // n_ctx, c)),
                  pl.BlockSpec((HY_ORDER, 2 * n_ctx, CTX_TC), lambda b, c: (0, 0, c)),
                  pl.BlockSpec((HY_ORDER, CTX_TC), lambda b, c: (0, c)),
                  pl.BlockSpec(fw.shape, lambda b, c: (0, 0)),
                  pl.BlockSpec(inv.shape, lambda b, c: (0, 0)),
                  pl.BlockSpec(memory_space=pl.ANY)],
        out_specs=pl.BlockSpec((1, n_ctx, CTX_TC), lambda b, c: (b, n_lat // n_ctx, c)),
        out_shape=jax.ShapeDtypeStruct(y_all.shape, F32),
        input_output_aliases={5: 0},
        compiler_params=_cparams("parallel", "parallel"),
        name="hy_ctx_conv",
    )(parts, filt_c, long_bias.astype(F32), fw, inv, y_all)


def _hyena(u_all, hy_p, n_lat):
    conv_w, conv_b, f_w1, f_b1, f_fr1, f_w2, f_b2, f_fr2, f_w3, long_bias = hy_p
    bsz, t, _ = u_all.shape
    n_ctx = t - n_lat
    assert n_ctx == DFT_N2 and n_lat % (2 * DFT_N2) == 0
    n1 = 2 * n_lat // DFT_N2
    tab = _dft_tables(n1)
    cw = HY_WIDTH
    parts = _hy_conv(u_all, conv_w, conv_b, n_lat)
    filt_l = _hy_filter(n_lat, f_w1, f_b1, f_fr1, f_w2, f_b2, f_fr2, f_w3)
    filt_c = _hy_filter(n_ctx, f_w1, f_b1, f_fr1, f_w2, f_b2, f_fr2, f_w3)
    spec = _dft_mid(*_dft_first(filt_l.reshape(HY_ORDER, n1, DFT_N2, cw), 0, HY_ORDER, tab['f1'], tab), tab)
    f1_half = tab['f1'][:, :n1 // 2]
    parts4 = parts.reshape((HY_ORDER + 1) * bsz, t // DFT_N2, DFT_N2, cw)
    y4, y_lead0 = parts4, 0
    for o in range(HY_ORDER):
        are, aim = _dft_first(y4, y_lead0, bsz, f1_half, tab)
        bre, bim = _dft_mid(are, aim, tab, spec=spec, order=o)
        last = o + 1 == HY_ORDER
        y4 = _dft_last(bre, bim, tab, y4, y_lead0, parts4, (o + 1) * bsz, long_bias[o],
                       t // DFT_N2 if last else n1 // 2)
        y_lead0 = 0
    return _hy_ctx(parts, filt_c, long_bias, y4.reshape(bsz, t, cw), n_lat)


def _row_tile(m):
    for tm in (1280, 1024, 512, 256):
        if m % tm == 0:
            return tm
    raise ValueError(m)


def _col_tile(n):
    for k in range(n // LANE, 0, -1):
        if n % (k * LANE) == 0 and k * LANE <= 1280:
            return k * LANE
    raise ValueError(n)


def _in_proj(h, w):
    bsz, t, d = h.shape
    n = w.shape[1]
    return _matmul(h.reshape(bsz * t, d), w, _row_tile(bsz * t), _col_tile(n)).reshape(bsz, t, n)


def kernel(x, c, ctx, c_ctx, router_w, router_b, mod_w, mod_b, ln_g, ln_b, exp_w_gate, exp_w_up, exp_w_down, ab_w_in, ab_w_out, hy_conv_w, hy_conv_b, hy_f_w1, hy_f_b1, hy_f_fr1, hy_f_w2, hy_f_b2, hy_f_fr2, hy_f_w3, hy_long_bias, gla_gate_w2, gla_gate_b, gla_norm_g, ssd_w_in, ssd_conv_w, ssd_conv_b, ssd_dt_bias, ssd_a_log, ssd_d, ssd_norm_g, ssd_w_out):
    bsz, n_lat, d = x.shape
    n_ctx = ctx.shape[1]
    assert bsz < 8 and n_lat % ROW_TILE == 0 and n_ctx % ROW_TILE == 0 and d == D_MODEL
    c8 = jnp.zeros((8, d), F32).at[:bsz].set(c).at[bsz].set(c_ctx)
    router_w_pad = jnp.zeros((d, LANE), F32).at[:, :N_EXPERTS].set(router_w)
    mods = [_mod_vectors(c8, mod_w[i].astype(F32), mod_b[i].astype(F32)).reshape(8, 6, d) for i in range(DEPTH)]

    def rows(i, idx_l):
        lat = jnp.stack([mods[i][:bsz, k] for k in idx_l], axis=1)
        cx = jnp.broadcast_to(jnp.stack([mods[i][bsz, k] for k in idx_l], axis=0)[None], lat.shape)
        return jnp.concatenate([lat, cx], axis=1)

    def with_ln(p, g, b):
        extra = jnp.broadcast_to(jnp.stack([g, b], axis=0).astype(F32)[None], (bsz, 2, d))
        return jnp.concatenate([p, extra], axis=1)

    xa, h = _modulate(x, ctx, rows(0, (0, 1)))
    for i in range(DEPTH):
        j = i // 2
        if i % 2 == 0:
            w_pad = -(-AB_IN // LANE) * LANE
            w_in = jnp.zeros((d, w_pad), BF16).at[:, :AB_IN].set(ab_w_in[j].astype(BF16))
            u_all = _in_proj(h, w_in)
            hy_p = (hy_conv_w[j], hy_conv_b[j], hy_f_w1[j], hy_f_b1[j], hy_f_fr1[j], hy_f_w2[j], hy_f_b2[j],
                    hy_f_fr2[j], hy_f_w3[j], hy_long_bias[j])
            y_hy = _hyena(u_all, hy_p, n_lat)
            y_gla = _gla(u_all, gla_gate_w2[j], gla_gate_b[j], gla_norm_g[j], n_lat)
            w_out = ab_w_out[j].astype(BF16)
            a_list, w_list = [y_hy, y_gla], [w_out[:HY_WIDTH], w_out[HY_WIDTH:]]
        else:
            u2 = _in_proj(h, ssd_w_in[j].astype(BF16))
            xbc_t = _ssd_conv(u2, ssd_conv_w[j].astype(F32), ssd_conv_b[j].astype(F32), n_lat)
            y_ssd = _ssd(xbc_t, u2, ssd_dt_bias[j].astype(F32), ssd_a_log[j], ssd_d[j].astype(F32), ssd_norm_g[j], n_lat)
            a_list, w_list = [y_ssd], [ssd_w_out[j].astype(BF16)]
        p1 = with_ln(rows(i, (2, 4, 3)), ln_g[i, 0], ln_b[i, 0])
        xa, tok, logits = _proj_update(a_list, w_list, xa, p1, router_w_pad, n_lat)
        y0, y1, gates = _moe(tok, logits, router_b, exp_w_gate.astype(F32), exp_w_up.astype(F32),
                             exp_w_down.astype(F32), i)
        gate2 = rows(i, (5,))
        if i + 1 < DEPTH:
            nxt = rows(i + 1, (1, 0))
        else:
            nxt = jnp.zeros((bsz, 4, d), F32)
        p2 = jnp.concatenate([gate2[:, 0:1], nxt[:, 0:2], gate2[:, 1:2], nxt[:, 2:4]], axis=1)
        xa, h = _combine_update(y0, y1, gates, xa, with_ln(p2, ln_g[i, 1], ln_b[i, 1]), n_lat, i + 1 == DEPTH)
    return xa.astype(x.dtype)
```

```python
import functools
import math

import numpy as np
import jax
import jax.numpy as jnp
from jax import lax
from jax.experimental import pallas as pl
from jax.experimental.pallas import tpu as pltpu

F32 = jnp.float32
BF16 = jnp.bfloat16
HIGHEST = lax.Precision.HIGHEST

D_MODEL = 2048
DEPTH = 2
GRID_W = 64
HY_WIDTH = D_MODEL // 2
HY_ORDER = 2
HY_BANDS = 16
HY_MIN_DECAY = math.log(1e-2) / 1.5
HY_MAX_DECAY = math.log(1e-2) / 0.3
GLA_HEADS = 4
GLA_DK = D_MODEL // 4
GLA_DV = D_MODEL // 2
GLA_HEAD_K = GLA_DK // GLA_HEADS
GLA_HEAD_V = GLA_DV // GLA_HEADS
GLA_RANK = 16
GLA_TAU = 16.0
AB_SIZES = ((HY_ORDER + 1) * HY_WIDTH, GLA_DK, GLA_DK, GLA_DV, GLA_DV, GLA_RANK, GLA_RANK)
AB_IN = sum(AB_SIZES)
SSD_D_INNER = 2 * D_MODEL
SSD_HEADDIM = 64
SSD_HEADS = SSD_D_INNER // SSD_HEADDIM
SSD_GROUPS = 8
SSD_HPG = SSD_HEADS // SSD_GROUPS
SSD_STATE = 128
SSD_CONV_DIM = SSD_D_INNER + 2 * SSD_GROUPS * SSD_STATE
SSD_IN = SSD_D_INNER + SSD_CONV_DIM + 2 * SSD_HEADS
N_EXPERTS = 16
N_EXPERT_GROUPS = 4
EXPERTS_PER_GROUP = N_EXPERTS // N_EXPERT_GROUPS
TOP_K = 2
D_EXPERT = D_MODEL // 2
ALPHA = (2 * DEPTH) ** 0.25
EPS = 1e-6

LANE = 128
ROW_TILE = 256
MOE_BLOCK = 256
VMEM_LIMIT = 56 * 1024 * 1024


def _cparams(*sem):
    return pltpu.CompilerParams(dimension_semantics=sem, vmem_limit_bytes=VMEM_LIMIT)


def _silu(x):
    return x * (1.0 / (1.0 + jnp.exp(-x)))


def _mod_kernel(c_ref, w_ref, b_ref, o_ref):
    o_ref[...] = jnp.dot(_silu(c_ref[...]), w_ref[...], precision=HIGHEST, preferred_element_type=F32) + b_ref[...]


def _mod_vectors(c8, w, b):
    d, n = w.shape
    tn = 1536
    return pl.pallas_call(
        _mod_kernel,
        grid=(n // tn,),
        in_specs=[pl.BlockSpec((8, d), lambda j: (0, 0)),
                  pl.BlockSpec((d, tn), lambda j: (0, j)),
                  pl.BlockSpec((1, tn), lambda j: (0, j))],
        out_specs=pl.BlockSpec((8, tn), lambda j: (0, j)),
        out_shape=jax.ShapeDtypeStruct((8, n), F32),
        compiler_params=_cparams("arbitrary"),
        name="adaln_vectors",
    )(c8, w, b.reshape(1, n))


def _modulate_kernel(n_lat_tiles, x_ref, c_ref, p_ref, xa_ref, o_ref):
    i = pl.program_id(1)
    is_ctx = i >= n_lat_tiles
    p = p_ref[0]
    shift = jnp.where(is_ctx, p[2:3], p[0:1])
    scale = jnp.where(is_ctx, p[3:4], p[1:2])
    ci = jnp.maximum(i - n_lat_tiles, 0)
    v = jnp.where(is_ctx, c_ref[0, pl.ds(pl.multiple_of(ci * ROW_TILE, ROW_TILE), ROW_TILE), :], x_ref[0])
    xa_ref[0] = v
    o_ref[0] = (v * (1.0 + scale) + shift).astype(o_ref.dtype)


def _modulate(x, ctx, params):
    bsz, n_lat, d = x.shape
    n_ctx = ctx.shape[1]
    t = n_lat + n_ctx
    nl = n_lat // ROW_TILE
    row = pl.BlockSpec((1, ROW_TILE, d), lambda b, i: (b, i, 0))
    return pl.pallas_call(
        functools.partial(_modulate_kernel, nl),
        grid=(bsz, t // ROW_TILE),
        in_specs=[pl.BlockSpec((1, ROW_TILE, d), lambda b, i: (b, jnp.minimum(i, nl - 1), 0)),
                  pl.BlockSpec((1, n_ctx, d), lambda b, i: (b, 0, 0)),
                  pl.BlockSpec((1, 4, d), lambda b, i: (b, 0, 0))],
        out_specs=[row, row],
        out_shape=[jax.ShapeDtypeStruct((bsz, t, d), F32), jax.ShapeDtypeStruct((bsz, t, d), BF16)],
        compiler_params=_cparams("parallel", "parallel"),
        name="modulate",
    )(x.astype(F32), ctx.astype(F32), params)


def _matmul_kernel(a_ref, w_ref, o_ref):
    o_ref[...] = jnp.dot(a_ref[...], w_ref[...], preferred_element_type=F32).astype(o_ref.dtype)


def _matmul(a, w, tm, tn, out_dtype=F32):
    m, k = a.shape
    n = w.shape[1]
    return pl.pallas_call(
        _matmul_kernel,
        grid=(m // tm, n // tn),
        in_specs=[pl.BlockSpec((tm, k), lambda i, j: (i, 0)),
                  pl.BlockSpec((k, tn), lambda i, j: (0, j))],
        out_specs=pl.BlockSpec((tm, tn), lambda i, j: (i, j)),
        out_shape=jax.ShapeDtypeStruct((m, n), out_dtype),
        compiler_params=_cparams("parallel", "arbitrary"),
        name="in_proj",
    )(a, w)


def _layer_norm_rows(v, g, b):
    mu = jnp.mean(v, axis=-1, keepdims=True)
    vc = v - mu
    var = jnp.mean(vc * vc, axis=-1, keepdims=True)
    return vc * lax.rsqrt(var + EPS) * g + b


def _param_table(p):
    bsz, _, d = p.shape
    pad = jnp.zeros((bsz, 3, d), F32)
    lat = jnp.concatenate([p[:, 0:3], p[:, 6:8], pad], axis=1)
    cx = jnp.concatenate([p[:, 3:6], p[:, 6:8], pad], axis=1)
    return jnp.stack([lat, cx], axis=1).reshape(2 * bsz, 8, d)


def _post_update(tiles_per_batch, n_lat_tiles, tile, y, x, p_ref):
    b = tile // tiles_per_batch
    is_ctx = (tile - b * tiles_per_batch >= n_lat_tiles).astype(jnp.int32)
    p = p_ref[2 * b + is_ctx]
    xn = _layer_norm_rows(ALPHA * x + p[0:1] * y, p[3:4], p[4:5])
    return xn, xn * (1.0 + p[1:2]) + p[2:3]


PROJ_TILES = 2


def _proj_update_kernel(tiles_per_batch, n_lat_tiles, n_a, *refs):
    a_refs = refs[:n_a]
    w_refs = refs[n_a:2 * n_a]
    x_ref, p_ref, rw_ref, xo_ref, tok_ref, lg_ref, y_ref = refs[2 * n_a:]
    i = pl.program_id(0)

    @pl.when(i == 0)
    def _():
        y_ref[...] = jnp.zeros_like(y_ref)

    prev = jnp.maximum(i - 1, 0)
    for hh in range(PROJ_TILES):
        rs = slice(hh * ROW_TILE, (hh + 1) * ROW_TILE)
        xn, tok = _post_update(tiles_per_batch, n_lat_tiles, prev * PROJ_TILES + hh, y_ref[rs, :], x_ref[rs, :], p_ref)
        xo_ref[rs, :] = xn
        tok_ref[rs, :] = tok.astype(tok_ref.dtype)
        lg_ref[rs, :] = _dft_dot(tok, rw_ref[...])
    for hh in range(PROJ_TILES):
        rs = slice(hh * ROW_TILE, (hh + 1) * ROW_TILE)
        y = jnp.dot(a_refs[0][rs, :].astype(BF16), w_refs[0][...], preferred_element_type=F32)
        for a_ref, w_ref in zip(a_refs[1:], w_refs[1:]):
            y = y + jnp.dot(a_ref[rs, :].astype(BF16), w_ref[...], preferred_element_type=F32)
        y_ref[rs, :] = y


def _proj_update(a_list, w_list, x, params, router_w_pad, n_lat):
    bsz, t, d = x.shape
    n = bsz * t
    tm = PROJ_TILES * ROW_TILE
    assert n % tm == 0
    steps = n // tm
    n_a = len(a_list)
    flat = lambda v: v.reshape(n, v.shape[-1])
    cur = lambda i: (jnp.minimum(i, steps - 1), 0)
    prev = lambda i: (jnp.maximum(i - 1, 0), 0)
    in_specs = [pl.BlockSpec((tm, a.shape[-1]), cur) for a in a_list]
    in_specs += [pl.BlockSpec(w.shape, lambda i: (0, 0)) for w in w_list]
    in_specs += [pl.BlockSpec((tm, d), prev),
                 pl.BlockSpec((2 * bsz, 8, d), lambda i: (0, 0, 0)),
                 pl.BlockSpec((d, LANE), lambda i: (0, 0))]
    xo, tok, lg = pl.pallas_call(
        functools.partial(_proj_update_kernel, t // ROW_TILE, n_lat // ROW_TILE, n_a),
        grid=(steps + 1,),
        in_specs=in_specs,
        out_specs=[pl.BlockSpec((tm, d), prev), pl.BlockSpec((tm, d), prev), pl.BlockSpec((tm, LANE), prev)],
        out_shape=[jax.ShapeDtypeStruct((n, d), F32),
                   jax.ShapeDtypeStruct((n, d), BF16),
                   jax.ShapeDtypeStruct((n, LANE), F32)],
        scratch_shapes=[pltpu.VMEM((tm, d), F32)],
        compiler_params=_cparams("arbitrary"),
        name="out_proj_ln",
    )(*[flat(a) for a in a_list], *w_list, flat(x), _param_table(params), router_w_pad)
    return xo.reshape(bsz, t, d), tok.reshape(bsz, t, d), lg.reshape(bsz, t, LANE)


def _route_kernel(lg_ref, rb_ref, tri_ref, oi_ref, og_ref, cnt_ref, carry_ref):
    @pl.when(pl.program_id(0) == 0)
    def _():
        carry_ref[...] = jnp.zeros_like(carry_ref)

    tm = lg_ref.shape[0]
    epg = EXPERTS_PER_GROUP
    lt = lg_ref[...].T[:N_EXPERTS]
    score = 1.0 / (1.0 + jnp.exp(-lt))
    sel = score + rb_ref[...]
    best_g = None
    for q in range(N_EXPERT_GROUPS):
        rows = [sel[q * epg + r:q * epg + r + 1] for r in range(epg)]
        gs = None
        for a in range(epg):
            for b in range(a + 1, epg):
                ps = rows[a] + rows[b]
                gs = ps if gs is None else jnp.maximum(gs, ps)
        if best_g is None:
            best_g, grp = gs, jnp.zeros_like(gs, dtype=jnp.int32)
        else:
            better = gs > best_g
            grp = jnp.where(better, q, grp)
            best_g = jnp.where(better, gs, best_g)
    in_sel, in_score = [], []
    for r in range(epg):
        v = sel[r:r + 1]
        s = score[r:r + 1]
        for q in range(1, N_EXPERT_GROUPS):
            v = jnp.where(grp == q, sel[q * epg + r:q * epg + r + 1], v)
            s = jnp.where(grp == q, score[q * epg + r:q * epg + r + 1], s)
        in_sel.append(v)
        in_score.append(s)
    i1, v1 = jnp.zeros_like(grp), in_sel[0]
    for r in range(1, epg):
        better = in_sel[r] > v1
        i1 = jnp.where(better, r, i1)
        v1 = jnp.where(better, in_sel[r], v1)
    i2, v2 = None, None
    for r in range(epg):
        cand = jnp.where(i1 == r, -jnp.inf, in_sel[r])
        if i2 is None:
            i2, v2 = jnp.zeros_like(grp), cand
        else:
            better = cand > v2
            i2 = jnp.where(better, r, i2)
            v2 = jnp.where(better, cand, v2)
    s1, s2 = in_score[0], in_score[0]
    for r in range(1, epg):
        s1 = jnp.where(i1 == r, in_score[r], s1)
        s2 = jnp.where(i2 == r, in_score[r], s2)
    e1 = grp * epg + i1
    e2 = grp * epg + i2
    tot = s1 + s2
    g1, g2 = s1 / tot, s2 / tot
    eid = lax.broadcasted_iota(jnp.int32, (N_EXPERTS, tm), 0)
    oh1 = eid == e1
    oh2 = eid == e2
    both = jnp.where(oh1 | oh2, 1.0, 0.0)
    before = jnp.dot(both.astype(BF16), tri_ref[...], preferred_element_type=F32) + carry_ref[:, 0:1]
    r1 = jnp.sum(jnp.where(oh1, before, 0.0), axis=0, keepdims=True)
    r2 = jnp.sum(jnp.where(oh2, before, 0.0), axis=0, keepdims=True)
    new_carry = carry_ref[...] + jnp.sum(both, axis=1, keepdims=True)
    carry_ref[...] = new_carry
    cnt_ref[...] = new_carry.astype(jnp.int32)
    zi = jnp.zeros((4, tm), jnp.int32)
    oi_ref[...] = jnp.concatenate([e1, e2, r1.astype(jnp.int32), r2.astype(jnp.int32), zi], axis=0)
    gt = jnp.concatenate([g1, g2, jnp.zeros((LANE - 2, tm), F32)], axis=0)
    og_ref[...] = gt.T


def _route(logits, router_b):
    n = logits.shape[0]
    tm = ROW_TILE
    tri = jnp.asarray(np.triu(np.ones((tm, tm), np.float32), 1)).astype(BF16)
    oi, og, cnt = pl.pallas_call(
        _route_kernel,
        grid=(n // tm,),
        in_specs=[pl.BlockSpec((tm, LANE), lambda i: (i, 0)),
                  pl.BlockSpec((N_EXPERTS, 1), lambda i: (0, 0)),
                  pl.BlockSpec((tm, tm), lambda i: (0, 0))],
        out_specs=[pl.BlockSpec((8, tm), lambda i: (0, i)),
                   pl.BlockSpec((tm, LANE), lambda i: (i, 0)),
                   pl.BlockSpec((N_EXPERTS, LANE), lambda i: (0, 0))],
        out_shape=[jax.ShapeDtypeStruct((8, n), jnp.int32),
                   jax.ShapeDtypeStruct((n, LANE), F32),
                   jax.ShapeDtypeStruct((N_EXPERTS, LANE), jnp.int32)],
        scratch_shapes=[pltpu.VMEM((N_EXPERTS, LANE), F32)],
        compiler_params=_cparams("arbitrary"),
        name="moe_route",
    )(logits, router_b.reshape(N_EXPERTS, 1).astype(F32), tri)
    return oi, og, cnt[:, 0]


W_CONV_ROWS = 256


def _expert_kernel(layer, be_ref, nb_ref, first_ref, nxt_ref, x_ref, wg_hbm, wu_hbm, wd_hbm, o_ref,
                   sg_ref, su_ref, sd_ref, wg_ref, wu_ref, wd_ref, sem):
    i = pl.program_id(0)

    def copies(e):
        return (pltpu.make_async_copy(wg_hbm.at[layer, e], sg_ref, sem.at[0]),
                pltpu.make_async_copy(wu_hbm.at[layer, e], su_ref, sem.at[1]),
                pltpu.make_async_copy(wd_hbm.at[layer, e], sd_ref, sem.at[2]))

    def convert(src, dst):
        def body(r, carry):
            r0 = pl.multiple_of(r * W_CONV_ROWS, W_CONV_ROWS)
            dst[pl.ds(r0, W_CONV_ROWS), :] = src[pl.ds(r0, W_CONV_ROWS), :].astype(BF16)
            return carry
        lax.fori_loop(0, src.shape[0] // W_CONV_ROWS, body, 0)

    active = i < nb_ref[0]

    @pl.when(active & (i == 0))
    def _():
        for cp in copies(be_ref[0]):
            cp.start()

    @pl.when(active & (first_ref[i] == 1))
    def _():
        for cp in copies(be_ref[i]):
            cp.wait()
        convert(sg_ref, wg_ref)
        convert(su_ref, wu_ref)
        convert(sd_ref, wd_ref)

        @pl.when(nxt_ref[i] >= 0)
        def _():
            for cp in copies(nxt_ref[i]):
                cp.start()

    @pl.when(active)
    def _():
        x = x_ref[...]
        hg = jnp.dot(x, wg_ref[...], preferred_element_type=F32)
        hu = jnp.dot(x, wu_ref[...], preferred_element_type=F32)
        h = (_silu(hg) * hu).astype(BF16)
        o_ref[...] = jnp.dot(h, wd_ref[...], preferred_element_type=F32).astype(o_ref.dtype)

    @pl.when(jnp.logical_not(active))
    def _():
        o_ref[...] = jnp.zeros_like(o_ref)


def _expert_blocks(buf, block_expert, n_used, first, nxt, wg, wu, wd, layer):
    rows, d = buf.shape
    n_blocks = rows // MOE_BLOCK
    de = wg.shape[-1]
    grid_spec = pltpu.PrefetchScalarGridSpec(
        num_scalar_prefetch=4,
        grid=(n_blocks,),
        in_specs=[pl.BlockSpec((MOE_BLOCK, d), lambda i, *_: (i, 0)),
                  pl.BlockSpec(memory_space=pl.ANY),
                  pl.BlockSpec(memory_space=pl.ANY),
                  pl.BlockSpec(memory_space=pl.ANY)],
        out_specs=pl.BlockSpec((MOE_BLOCK, d), lambda i, *_: (i, 0)),
        scratch_shapes=[pltpu.VMEM((d, de), F32), pltpu.VMEM((d, de), F32), pltpu.VMEM((de, d), F32),
                        pltpu.VMEM((d, de), BF16), pltpu.VMEM((d, de), BF16), pltpu.VMEM((de, d), BF16),
                        pltpu.SemaphoreType.DMA((3,))],
    )
    return pl.pallas_call(
        functools.partial(_expert_kernel, layer),
        grid_spec=grid_spec,
        out_shape=jax.ShapeDtypeStruct((rows, d), BF16),
        compiler_params=_cparams("arbitrary"),
        name="moe_experts",
    )(block_expert, n_used, first, nxt, buf, wg, wu, wd)


def _combine_update_kernel(tiles_per_batch, n_lat_tiles, last, y0_ref, y1_ref, g_ref, x_ref, p_ref, xo_ref, *h_ref):
    i = pl.program_id(0)
    tile = (i // n_lat_tiles) * tiles_per_batch + i % n_lat_tiles if last else i
    g = g_ref[...]
    y = y0_ref[...].astype(F32) * g[:, 0:1] + y1_ref[...].astype(F32) * g[:, 1:2]
    xn, h = _post_update(tiles_per_batch, n_lat_tiles, tile, y, x_ref[...], p_ref)
    xo_ref[...] = xn
    if not last:
        h_ref[0][...] = h.astype(h_ref[0].dtype)


def _combine_update(y0, y1, gates, x, params, n_lat, last):
    bsz, t, d = x.shape
    n = bsz * t
    tpb, nl = t // ROW_TILE, n_lat // ROW_TILE
    if last:
        src = lambda i: ((i // nl) * tpb + i % nl, 0)
        steps, out_rows = bsz * nl, bsz * n_lat
    else:
        src = lambda i: (i, 0)
        steps, out_rows = n // ROW_TILE, n
    row_in = pl.BlockSpec((ROW_TILE, d), src)
    row_out = pl.BlockSpec((ROW_TILE, d), lambda i: (i, 0))
    out_shape = [jax.ShapeDtypeStruct((out_rows, d), F32)] + ([] if last else [jax.ShapeDtypeStruct((n, d), BF16)])
    outs = pl.pallas_call(
        functools.partial(_combine_update_kernel, tpb, nl, last),
        grid=(steps,),
        in_specs=[row_in, row_in, pl.BlockSpec((ROW_TILE, LANE), src), row_in,
                  pl.BlockSpec((2 * bsz, 8, d), lambda i: (0, 0, 0))],
        out_specs=[row_out] * len(out_shape),
        out_shape=out_shape,
        compiler_params=_cparams("parallel"),
        name="moe_combine_ln",
    )(y0, y1, gates, x.reshape(n, d), _param_table(params))
    if last:
        return outs[0].reshape(bsz, n_lat, d), None
    return outs[0].reshape(bsz, t, d), outs[1].reshape(bsz, t, d)


def _moe(tok, logits, router_b, wg, wu, wd, layer):
    bsz, t, d = tok.shape
    n = bsz * t
    tk = n * TOP_K
    oi, gates, counts = _route(logits.reshape(n, LANE), router_b)
    experts, ranks = oi[0:2], oi[2:4]
    padded = (counts + MOE_BLOCK - 1) // MOE_BLOCK * MOE_BLOCK
    pends = jnp.cumsum(padded)
    pstarts = pends - padded
    eids = jnp.arange(N_EXPERTS, dtype=jnp.int32)
    table = lambda idx, tab: jnp.sum(jnp.where(idx[..., None] == eids, tab.astype(jnp.int32), 0), axis=-1)
    dest = table(experts, pstarts) + ranks
    n_blocks = -(-tk // MOE_BLOCK) + N_EXPERTS
    blk_start = jnp.arange(n_blocks, dtype=jnp.int32) * MOE_BLOCK
    block_expert = jnp.minimum(jnp.sum(pends[None, :] <= blk_start[:, None], axis=1), N_EXPERTS - 1).astype(jnp.int32)
    n_used = (pends[-1] // MOE_BLOCK).astype(jnp.int32).reshape(1)
    prev_e = jnp.concatenate([jnp.full((1,), -1, jnp.int32), block_expert[:-1]])
    first = (block_expert != prev_e).astype(jnp.int32)
    later = jnp.where((counts[None, :] > 0) & (jnp.arange(N_EXPERTS)[None, :] > jnp.arange(N_EXPERTS)[:, None]),
                      jnp.arange(N_EXPERTS)[None, :], N_EXPERTS)
    nxt_e = jnp.min(later, axis=1)
    nxt_e = jnp.where(nxt_e >= N_EXPERTS, -1, nxt_e).astype(jnp.int32)
    nxt = table(block_expert, nxt_e)
    tok_id = jnp.broadcast_to(jnp.arange(n, dtype=jnp.int32)[None], (TOP_K, n))
    src = jnp.zeros((n_blocks * MOE_BLOCK,), jnp.int32).at[dest.reshape(-1)].set(
        tok_id.reshape(-1), mode='promise_in_bounds', unique_indices=True)
    take = lambda a, idx: a.at[idx].get(mode='promise_in_bounds')
    buf = take(tok.reshape(n, d), src)
    out = _expert_blocks(buf, block_expert, n_used, first, nxt, wg, wu, wd, layer)
    return take(out, dest[0]), take(out, dest[1]), gates


GLA_CS = 64
EXP_CLAMP = 80.0


def _gla_kernel(reverse, has_prev, nsub, *refs):
    if has_prev:
        (q_ref, k_ref, v_ref, g1_ref, w2_ref, gb_ref, prev_ref, r_ref, ng_ref, o_ref, s_ref) = refs
    else:
        (q_ref, k_ref, v_ref, g1_ref, w2_ref, gb_ref, o_ref, s_ref) = refs

    @pl.when(pl.program_id(1) == 0)
    def _():
        s_ref[...] = jnp.zeros_like(s_ref)

    cs = GLA_CS
    row = lax.broadcasted_iota(jnp.int32, (cs, cs), 0)
    col = lax.broadcasted_iota(jnp.int32, (cs, cs), 1)
    keep = (col >= row) if reverse else (col <= row)
    cum_m = keep.astype(BF16)
    z = _dft_dot(g1_ref[0], w2_ref[0]) + gb_ref[0]
    log_gate = (jnp.minimum(z, 0.0) - jnp.log(1.0 + jnp.exp(-jnp.abs(z)))) * (1.0 / GLA_TAU)
    q_all = q_ref[0] * (GLA_HEAD_K ** -0.5)
    k_all = k_ref[0]
    v_all = v_ref[0]
    last = 0 if reverse else cs - 1
    states = [s_ref[h] for h in range(GLA_HEADS)]
    for c in (range(nsub - 1, -1, -1) if reverse else range(nsub)):
        rs = slice(c * cs, (c + 1) * cs)
        lg_hi, lg_lo = _split_bf16(log_gate[rs])
        b_all = jnp.dot(cum_m, lg_hi, preferred_element_type=F32) + jnp.dot(cum_m, lg_lo, preferred_element_type=F32)
        for h in range(GLA_HEADS):
            ks = slice(h * GLA_HEAD_K, (h + 1) * GLA_HEAD_K)
            vs = slice(h * GLA_HEAD_V, (h + 1) * GLA_HEAD_V)
            b = b_all[:, ks]
            b_last = b[last:last + 1]
            ref = 0.5 * b_last
            qc, kc, vc = q_all[rs, ks], k_all[rs, ks], v_all[rs, vs].astype(BF16)
            q_in = (qc * jnp.exp(jnp.minimum(b - ref, EXP_CLAMP))).astype(BF16)
            k_in = (kc * jnp.exp(jnp.minimum(ref - b, EXP_CLAMP))).astype(BF16)
            att = lax.dot_general(q_in, k_in, (((1,), (1,)), ((), ())), preferred_element_type=F32)
            att = jnp.where(keep, att, 0.0).astype(BF16)
            o_h = jnp.dot(att, vc, preferred_element_type=F32)
            o_h = o_h + lax.dot_general((qc * jnp.exp(b)).astype(BF16), states[h].astype(BF16),
                                        (((1,), (1,)), ((), ())), preferred_element_type=F32)
            k_tail = (kc * jnp.exp(b_last - b)).astype(BF16)
            upd = lax.dot_general(vc, k_tail, (((0,), (0,)), ((), ())), preferred_element_type=F32)
            states[h] = jnp.exp(b_last) * states[h] + upd
            if has_prev:
                o_h = o_h + prev_ref[0, rs, vs]
                o_h = o_h * lax.rsqrt(jnp.mean(o_h * o_h, axis=-1, keepdims=True) + EPS) * ng_ref[:, vs]
                o_h = o_h * _silu(r_ref[0, rs, vs])
            o_ref[0, rs, vs] = o_h.astype(o_ref.dtype)
    for h in range(GLA_HEADS):
        s_ref[h] = states[h]


def _gla_direction(u_all, w2pad, gbias, n_lat, reverse, prev=None, norm_g=None):
    bsz, t, _ = u_all.shape
    n = t // ROW_TILE
    nl = n_lat // ROW_TILE
    nsub = ROW_TILE // GLA_CS
    if reverse:
        blk = lambda s: n - 1 - s
    else:
        blk = lambda s: (s + nl) % n
    q0, k0, v0, r0, g0 = (AB_SIZES[0] // GLA_DK, (AB_SIZES[0] + GLA_DK) // GLA_DK, (AB_SIZES[0] + 2 * GLA_DK) // GLA_DV,
                          (AB_SIZES[0] + 2 * GLA_DK + GLA_DV) // GLA_DV, (AB_IN - 2 * GLA_RANK) // LANE)
    d = 1 if reverse else 0
    in_specs = [pl.BlockSpec((1, ROW_TILE, GLA_DK), lambda b, s: (b, blk(s), q0)),
                pl.BlockSpec((1, ROW_TILE, GLA_DK), lambda b, s: (b, blk(s), k0)),
                pl.BlockSpec((1, ROW_TILE, GLA_DV), lambda b, s: (b, blk(s), v0)),
                pl.BlockSpec((1, ROW_TILE, LANE), lambda b, s: (b, blk(s), g0)),
                pl.BlockSpec((1, LANE, GLA_DK), lambda b, s: (d, 0, 0)),
                pl.BlockSpec((1, 1, GLA_DK), lambda b, s: (d, 0, 0))]
    args = [u_all, u_all, u_all, u_all, w2pad, gbias]
    has_prev = prev is not None
    if has_prev:
        in_specs += [pl.BlockSpec((1, ROW_TILE, GLA_DV), lambda b, s: (b, blk(s), 0)),
                     pl.BlockSpec((1, ROW_TILE, GLA_DV), lambda b, s: (b, blk(s), r0)),
                     pl.BlockSpec((1, GLA_DV), lambda b, s: (0, 0))]
        args += [prev, u_all, norm_g.reshape(1, GLA_DV)]
    return pl.pallas_call(
        functools.partial(_gla_kernel, reverse, has_prev, nsub),
        grid=(bsz, n),
        in_specs=in_specs,
        out_specs=pl.BlockSpec((1, ROW_TILE, GLA_DV), lambda b, s: (b, blk(s), 0)),
        out_shape=jax.ShapeDtypeStruct((bsz, t, GLA_DV), BF16 if has_prev else F32),
        scratch_shapes=[pltpu.VMEM((GLA_HEADS, GLA_HEAD_V, GLA_HEAD_K), F32)],
        compiler_params=_cparams("parallel", "arbitrary"),
        name="gla_bwd_norm" if reverse else "gla_fwd",
    )(*args)


def _gla(u_all, gate_w2, gate_b, norm_g, n_lat):
    w2pad = jnp.zeros((2, LANE, GLA_DK), F32)
    w2pad = w2pad.at[0, :GLA_RANK].set(gate_w2[0]).at[1, GLA_RANK:2 * GLA_RANK].set(gate_w2[1])
    gbias = gate_b.reshape(2, 1, GLA_DK).astype(F32)
    o_f = _gla_direction(u_all, w2pad, gbias, n_lat, False)
    return _gla_direction(u_all, w2pad, gbias, n_lat, True, prev=o_f, norm_g=norm_g.astype(F32))


CONV_CHUNK = 512


def _shift_rows(v, dh):
    return v if dh == 0 else pltpu.roll(v, (-dh) % v.shape[0], 0)


def _ssd_conv_kernel(n_lat, x_ref, w_ref, b_ref, o_ref, pad_ref):
    t = x_ref.shape[1]
    n_ctx = t - n_lat
    pad_ref[0:GRID_W, :] = jnp.zeros((GRID_W, LANE), F32)
    pad_ref[GRID_W + n_lat:2 * GRID_W + n_lat, :] = jnp.zeros((GRID_W, LANE), F32)
    pad_ref[GRID_W:GRID_W + n_lat, :] = x_ref[0, 0:n_lat, :]
    w = w_ref[...]
    bias = b_ref[...]
    col = lax.broadcasted_iota(jnp.int32, (CONV_CHUNK, LANE), 0) % GRID_W
    masks = {-1: col >= 1, 0: None, 1: col <= GRID_W - 2}

    def body(ci, carry):
        t0 = pl.multiple_of(ci * CONV_CHUNK, CONV_CHUNK)
        slabs = [pad_ref[pl.ds(t0 + i * GRID_W, CONV_CHUNK), :] for i in range(3)]
        part = [sum(slabs[i] * w[i * 3 + j:i * 3 + j + 1] for i in range(3)) for j in range(3)]
        acc = part[1] + bias
        acc = acc + jnp.where(masks[-1], _shift_rows(part[0], -1), 0.0)
        acc = acc + jnp.where(masks[1], _shift_rows(part[2], 1), 0.0)
        o_ref[0, 0, pl.ds(t0, CONV_CHUNK), :] = _silu(acc).astype(o_ref.dtype)
        return carry

    lax.fori_loop(0, n_lat // CONV_CHUNK, body, 0)
    xc = x_ref[0, n_lat:t, :]
    pos = lax.broadcasted_iota(jnp.int32, (n_ctx, LANE), 0)
    acc = xc * w[4:5] + bias
    acc = acc + jnp.where(pos >= 1, _shift_rows(xc, -1), 0.0) * w[3:4]
    acc = acc + jnp.where(pos <= n_ctx - 2, _shift_rows(xc, 1), 0.0) * w[5:6]
    o_ref[0, 0, n_lat:t, :] = _silu(acc).astype(o_ref.dtype)


def _ssd_conv(u2, conv_w, conv_b, n_lat, tile0, n_tiles, out_dtype):
    bsz, t, _ = u2.shape
    c0 = SSD_D_INNER // LANE + tile0
    return pl.pallas_call(
        functools.partial(_ssd_conv_kernel, n_lat),
        grid=(bsz, n_tiles),
        in_specs=[pl.BlockSpec((1, t, LANE), lambda b, j: (b, 0, c0 + j)),
                  pl.BlockSpec((9, LANE), lambda b, j: (0, tile0 + j)),
                  pl.BlockSpec((1, LANE), lambda b, j: (0, tile0 + j))],
        out_specs=pl.BlockSpec((1, 1, t, LANE), lambda b, j: (b, j, 0, 0)),
        out_shape=jax.ShapeDtypeStruct((bsz, n_tiles, t, LANE), out_dtype),
        scratch_shapes=[pltpu.VMEM((n_lat + 2 * GRID_W, LANE), F32)],
        compiler_params=_cparams("parallel", "parallel"),
        name="ssd_conv",
    )(u2, conv_w.reshape(9, SSD_CONV_DIM), conv_b.reshape(1, SSD_CONV_DIM))


SSD_CS = 128


def _softplus(x):
    return jnp.maximum(x, 0.0) + jnp.log(1.0 + jnp.exp(-jnp.abs(x)))


def _spread_dot(v, onehot):
    hi = v.astype(BF16)
    lo = (v - hi.astype(F32)).astype(BF16)
    return jnp.dot(hi, onehot, preferred_element_type=F32) + jnp.dot(lo, onehot, preferred_element_type=F32)


def _ssd_kernel(reverse, has_prev, nsub, *refs):
    if has_prev:
        (x_ref, b_ref, c_ref, dt_ref, p_ref, expq_ref, dsk_ref, prev_ref, z_ref, ng_ref, o_ref,
         s_ref, at_ref, dtt_ref, acs_ref, eag_ref, twg_ref) = refs
    else:
        (x_ref, b_ref, c_ref, dt_ref, p_ref, expq_ref, dsk_ref, o_ref,
         s_ref, at_ref, dtt_ref, acs_ref, eag_ref, twg_ref) = refs

    @pl.when(pl.program_id(1) == 0)
    def _():
        s_ref[...] = jnp.zeros_like(s_ref)

    cs = SSD_CS
    nh = SSD_HEADS
    tpg = SSD_HPG * SSD_HEADDIM // LANE
    gw = SSD_HPG * SSD_HEADDIM
    row = lax.broadcasted_iota(jnp.int32, (cs, cs), 0)
    col = lax.broadcasted_iota(jnp.int32, (cs, cs), 1)
    keep = (col >= row) if reverse else (col <= row)
    cum_m = keep.astype(BF16)
    last = 0 if reverse else cs - 1
    lane = lax.broadcasted_iota(jnp.int32, (cs, LANE), 1)
    d0 = nh if reverse else 0
    p = p_ref[...]
    dtv = _softplus(dt_ref[0][:, d0:d0 + nh] + p[0:1, d0:d0 + nh])
    a_all = dtv * p[1:2, d0:d0 + nh]
    for c in (range(nsub - 1, -1, -1) if reverse else range(nsub)):
        rs = slice(c * cs, (c + 1) * cs)
        a = a_all[rs]
        hi = a.astype(BF16)
        r1 = a - hi.astype(F32)
        mid = r1.astype(BF16)
        lo = (r1 - mid.astype(F32)).astype(BF16)
        acs = (jnp.dot(cum_m, hi, preferred_element_type=F32) + jnp.dot(cum_m, mid, preferred_element_type=F32)
               + jnp.dot(cum_m, lo, preferred_element_type=F32))
        a_last = acs[last:last + 1]
        acs_ref[...] = acs
        e_acs = jnp.exp(acs)
        tail_w = jnp.exp(a_last - acs) * dtv[rs]
        for gg in range(SSD_GROUPS):
            eag_ref[gg] = e_acs[:, gg * SSD_HPG:(gg + 1) * SSD_HPG]
            twg_ref[gg] = tail_w[:, gg * SSD_HPG:(gg + 1) * SSD_HPG]
        at_ref[...] = acs.T
        dtt_ref[...] = dtv[rs].T

        def group(g, carry):
            h0 = pl.multiple_of(g * SSD_HPG, SSD_HPG)
            bg = b_ref[0, g, rs, :].astype(BF16)
            cg = c_ref[0, g, rs, :].astype(BF16)
            cb = lax.dot_general(cg, bg, (((1,), (1,)), ((), ())), preferred_element_type=F32)
            colb = _spread_dot(acs_ref[...], expq_ref[g])
            rows8 = at_ref[pl.ds(h0, SSD_HPG), :]
            dt8 = dtt_ref[pl.ds(h0, SSD_HPG), :]
            ms = []
            for r in range(SSD_HPG):
                diff = colb[:, r * LANE:(r + 1) * LANE] - rows8[r:r + 1, :]
                decay = jnp.where(keep, jnp.exp(jnp.minimum(diff, 0.0)), 0.0)
                ms.append((cb * decay * dt8[r:r + 1, :]).astype(BF16))
            ea_g = eag_ref[g]
            tw_g = twg_ref[g]
            state = s_ref[g]
            inter = jnp.dot(cg, state.astype(BF16), preferred_element_type=F32)
            xs = [x_ref[0, g * tpg + k, rs, :] for k in range(tpg)]
            dsk = dsk_ref[g]
            first_head = lane < SSD_HEADDIM
            ys, xws, e_last = [], [], []
            for k in range(tpg):
                bd = jnp.concatenate([jnp.where(first_head, xs[k], 0.0),
                                      jnp.where(first_head, 0.0, xs[k])], axis=0).astype(BF16)
                m2 = jnp.concatenate(ms[2 * k:2 * k + 2], axis=1)
                sl = slice(k * LANE, (k + 1) * LANE)
                e_t = jnp.where(first_head, ea_g[:, 2 * k:2 * k + 1], ea_g[:, 2 * k + 1:2 * k + 2])
                w_t = jnp.where(first_head, tw_g[:, 2 * k:2 * k + 1], tw_g[:, 2 * k + 1:2 * k + 2])
                y = jnp.dot(m2, bd, preferred_element_type=F32)
                ys.append(y + e_t * inter[:, sl] + dsk[:, sl] * xs[k])
                xws.append((xs[k] * w_t).astype(BF16))
                e_last.append(e_t[last:last + 1])
            upd = lax.dot_general(bg, jnp.concatenate(xws, axis=1), (((0,), (0,)), ((), ())),
                                  preferred_element_type=F32)
            s_ref[g] = jnp.concatenate(e_last, axis=1) * state + upd
            if has_prev:
                off = pl.multiple_of(g * gw, gw)
                z = z_ref[0, rs, pl.ds(off, gw)]
                yt = [(ys[k] + prev_ref[0, g * tpg + k, rs, :]) * _silu(z[:, k * LANE:(k + 1) * LANE])
                      for k in range(tpg)]
                ss = sum(jnp.sum(v * v, axis=-1, keepdims=True) for v in yt)
                inv = lax.rsqrt(ss * (1.0 / gw) + EPS)
                ng = ng_ref[:, pl.ds(off, gw)]
                o_ref[0, rs, pl.ds(off, gw)] = (jnp.concatenate(yt, axis=1) * inv * ng).astype(o_ref.dtype)
            else:
                for k in range(tpg):
                    o_ref[0, g * tpg + k, rs, :] = ys[k]
            return carry

        lax.fori_loop(0, SSD_GROUPS, group, 0, unroll=2)


def _ssd_direction(xs_t, bc_t, u2, pvec, exp_q, dsk, n_lat, reverse, prev=None, norm_g=None):
    bsz, _, t, _ = xs_t.shape
    n = t // ROW_TILE
    nl = n_lat // ROW_TILE
    nsub = ROW_TILE // SSD_CS
    nx = SSD_D_INNER // LANE
    gw = SSD_HPG * SSD_HEADDIM
    if reverse:
        blk = lambda s: n - 1 - s
    else:
        blk = lambda s: (s + nl) % n
    d = 1 if reverse else 0
    full = lambda a: pl.BlockSpec(a.shape, lambda b, s: (0,) * a.ndim)
    in_specs = [pl.BlockSpec((1, nx, ROW_TILE, LANE), lambda b, s: (b, 0, blk(s), 0)),
                pl.BlockSpec((1, SSD_GROUPS, ROW_TILE, LANE), lambda b, s: (b, 0, blk(s), 0)),
                pl.BlockSpec((1, SSD_GROUPS, ROW_TILE, LANE), lambda b, s: (b, 1, blk(s), 0)),
                pl.BlockSpec((1, ROW_TILE, LANE), lambda b, s: (b, blk(s), (SSD_IN - 2 * SSD_HEADS) // LANE)),
                full(pvec), full(exp_q),
                pl.BlockSpec((SSD_GROUPS, 1, gw), lambda b, s: (d, 0, 0))]
    args = [xs_t, bc_t, bc_t, u2, pvec, exp_q, dsk.reshape(2 * SSD_GROUPS, 1, gw)]
    has_prev = prev is not None
    if has_prev:
        in_specs += [pl.BlockSpec((1, nx, ROW_TILE, LANE), lambda b, s: (b, 0, blk(s), 0)),
                     pl.BlockSpec((1, ROW_TILE, SSD_D_INNER), lambda b, s: (b, blk(s), 0)),
                     pl.BlockSpec((1, SSD_D_INNER), lambda b, s: (0, 0))]
        args += [prev, u2, norm_g.reshape(1, SSD_D_INNER)]
        out_spec = pl.BlockSpec((1, ROW_TILE, SSD_D_INNER), lambda b, s: (b, blk(s), 0))
        out_shape = jax.ShapeDtypeStruct((bsz, t, SSD_D_INNER), BF16)
    else:
        out_spec = pl.BlockSpec((1, nx, ROW_TILE, LANE), lambda b, s: (b, 0, blk(s), 0))
        out_shape = jax.ShapeDtypeStruct((bsz, nx, t, LANE), F32)
    return pl.pallas_call(
        functools.partial(_ssd_kernel, reverse, has_prev, nsub),
        grid=(bsz, n),
        in_specs=in_specs,
        out_specs=out_spec,
        out_shape=out_shape,
        scratch_shapes=[pltpu.VMEM((SSD_GROUPS, SSD_STATE, gw), F32),
                        pltpu.VMEM((SSD_HEADS, SSD_CS), F32), pltpu.VMEM((SSD_HEADS, SSD_CS), F32),
                        pltpu.VMEM((SSD_CS, SSD_HEADS), F32),
                        pltpu.VMEM((SSD_GROUPS, SSD_CS, SSD_HPG), F32),
                        pltpu.VMEM((SSD_GROUPS, SSD_CS, SSD_HPG), F32)],
        compiler_params=_cparams("parallel", "arbitrary"),
        name="ssd_bwd_norm" if reverse else "ssd_fwd",
    )(*args)


def _ssd(xs_t, bc_t, u2, dt_bias, a_log, d_skip, norm_g, n_lat):
    gw = SSD_HPG * SSD_HEADDIM
    pvec = jnp.zeros((8, LANE), F32)
    pvec = pvec.at[0].set(dt_bias.reshape(-1)).at[1].set(-jnp.exp(a_log.astype(F32)).reshape(-1))
    exp_q = np.zeros((SSD_GROUPS, SSD_HEADS, SSD_HPG * LANE), np.float32)
    for gg in range(SSD_GROUPS):
        for r in range(SSD_HPG):
            exp_q[gg, gg * SSD_HPG + r, r * LANE:(r + 1) * LANE] = 1.0
    exp_q = jnp.asarray(exp_q).astype(BF16)
    dsk = jnp.repeat(d_skip.astype(F32).reshape(2, SSD_GROUPS, SSD_HPG), SSD_HEADDIM, axis=-1)
    y_f = _ssd_direction(xs_t, bc_t, u2, pvec, exp_q, dsk, n_lat, False)
    return _ssd_direction(xs_t, bc_t, u2, pvec, exp_q, dsk, n_lat, True, prev=y_f, norm_g=norm_g.astype(F32))


HY_PAD = 8


def _hy_conv_kernel(n_lat, x_ref, w_ref, b_ref, o_ref, pad_ref):
    t = x_ref.shape[1]
    w = w_ref[...]
    bias = b_ref[...]
    ch = CONV_CHUNK
    pad_ref[0:HY_PAD, :] = jnp.zeros((HY_PAD, LANE), F32)
    pad_ref[HY_PAD + n_lat:2 * HY_PAD + n_lat, :] = jnp.zeros((HY_PAD, LANE), F32)
    pad_ref[HY_PAD:HY_PAD + n_lat, :] = x_ref[0, 0:n_lat, :]
    rowi = lax.broadcasted_iota(jnp.int32, (ch, LANE), 0)

    def body(ci, carry):
        t0 = pl.multiple_of(ci * ch, ch)
        cur = pad_ref[pl.ds(t0 + HY_PAD, ch), :]
        before = pad_ref[pl.ds(t0, HY_PAD), :][HY_PAD - 1:HY_PAD]
        after = pad_ref[pl.ds(t0 + HY_PAD + ch, HY_PAD), :][0:1]
        down = jnp.where(rowi == 0, before, _shift_rows(cur, -1))
        up = jnp.where(rowi == ch - 1, after, _shift_rows(cur, 1))
        o_ref[0, 0, pl.ds(t0, ch), :] = down * w[0:1] + cur * w[1:2] + up * w[2:3] + bias
        return carry

    lax.fori_loop(0, n_lat // ch, body, 0)
    n_ctx = t - n_lat
    xc = x_ref[0, n_lat:t, :]
    pos = lax.broadcasted_iota(jnp.int32, (n_ctx, LANE), 0)
    acc = xc * w[1:2] + bias
    acc = acc + jnp.where(pos >= 1, _shift_rows(xc, -1), 0.0) * w[0:1]
    acc = acc + jnp.where(pos <= n_ctx - 2, _shift_rows(xc, 1), 0.0) * w[2:3]
    o_ref[0, 0, n_lat:t, :] = acc


def _hy_conv(u_all, conv_w, conv_b, n_lat):
    bsz, t, _ = u_all.shape
    nch = (HY_ORDER + 1) * HY_WIDTH
    tpp = HY_WIDTH // LANE
    return pl.pallas_call(
        functools.partial(_hy_conv_kernel, n_lat),
        grid=(bsz, nch // LANE),
        in_specs=[pl.BlockSpec((1, t, LANE), lambda b, j: (b, 0, j)),
                  pl.BlockSpec((3, LANE), lambda b, j: (0, j)),
                  pl.BlockSpec((1, LANE), lambda b, j: (0, j))],
        out_specs=pl.BlockSpec((1, 1, t, LANE), lambda b, j: (j // tpp, b, 0, j % tpp)),
        out_shape=jax.ShapeDtypeStruct((HY_ORDER + 1, bsz, t, HY_WIDTH), F32),
        scratch_shapes=[pltpu.VMEM((n_lat + 2 * HY_PAD, LANE), F32)],
        compiler_params=_cparams("parallel", "parallel"),
        name="hy_short_conv",
    )(u_all, conv_w, conv_b.reshape(1, nch))


FILT_ROWS = 256


def _hy_filter_kernel(seq_len, bands_ref, w1t_ref, b1_ref, fr1_ref, w2t_ref, b2_ref, fr2_ref, w3a_ref, w3b_ref,
                      dl_ref, o_ref):
    i = pl.program_id(0)

    def lag_of(shape, axis):
        n = i * FILT_ROWS + lax.broadcasted_iota(jnp.int32, shape, axis)
        return n, jnp.where(n < seq_len, n, 2 * seq_len - n).astype(F32)

    _, pos_r = lag_of((1, FILT_ROWS), 1)
    tt_r = pos_r / max(seq_len - 1, 1)
    ang = 2.0 * math.pi * bands_ref[...] * pos_r / seq_len
    w1t = w1t_ref[...]
    nb = HY_BANDS
    hp = functools.partial(jnp.dot, precision=HIGHEST, preferred_element_type=F32)
    pre = w1t[:, 0:1] * tt_r + hp(w1t[:, 1:1 + nb], jnp.cos(ang)) - hp(w1t[:, 1 + nb:1 + 2 * nb], jnp.sin(ang))
    hid = jnp.sin(fr1_ref[...] * (pre + b1_ref[...]))
    hid = jnp.sin(fr2_ref[...] * (hp(w2t_ref[...], hid) + b2_ref[...]))
    n_c, pos_c = lag_of((FILT_ROWS, 1), 0)
    decay = jnp.exp(-(pos_c / max(seq_len - 1, 1)) * dl_ref[...])
    hid_rows = hid.T.astype(BF16)
    for o, w3_ref in enumerate((w3a_ref, w3b_ref)):
        h = jnp.dot(hid_rows, w3_ref[...].astype(BF16), preferred_element_type=F32)
        o_ref[o] = jnp.where(n_c == seq_len, 0.0, h * decay)


def _hy_filter(seq_len, f_w1, f_b1, f_fr1, f_w2, f_b2, f_fr2, f_w3):
    assert HY_ORDER == 2
    fh = f_w1.shape[1]
    emb = f_w1.shape[0]
    half = seq_len // FILT_ROWS
    bands = jnp.asarray(np.linspace(1e-4, HY_BANDS - 1, HY_BANDS, dtype=np.float32)).reshape(HY_BANDS, 1)
    deltas = jnp.asarray(np.abs(np.linspace(HY_MIN_DECAY, HY_MAX_DECAY, HY_WIDTH, dtype=np.float32))).reshape(1, HY_WIDTH)
    col = lambda v: v.reshape(fh, 1).astype(F32)
    full = lambda shape: pl.BlockSpec(shape, lambda i: (0,) * len(shape))
    w3 = f_w3.astype(F32)
    return pl.pallas_call(
        functools.partial(_hy_filter_kernel, seq_len),
        grid=(2 * half,),
        in_specs=[full((HY_BANDS, 1)), full((fh, emb)), full((fh, 1)), full((fh, 1)), full((fh, fh)), full((fh, 1)),
                  full((fh, 1)),
                  pl.BlockSpec((fh, HY_WIDTH), lambda i: (0, i // half)),
                  pl.BlockSpec((fh, HY_WIDTH), lambda i: (0, 2 + i // half)),
                  full((1, HY_WIDTH))],
        out_specs=pl.BlockSpec((HY_ORDER, FILT_ROWS, HY_WIDTH), lambda i: (0, i, 0)),
        out_shape=jax.ShapeDtypeStruct((HY_ORDER, 2 * seq_len, HY_WIDTH), F32),
        compiler_params=_cparams("parallel"),
        name="hy_filter",
    )(bands, f_w1.astype(F32).T, col(f_b1), col(f_fr1), f_w2.astype(F32).T, col(f_b2), col(f_fr2), w3, w3, deltas)


DFT_N2 = 256
DFT_S = 8


def _dft_tables(n1):
    n = n1 * DFT_N2
    k1h = n1 // 2 + 1
    k1p = -(-k1h // 8) * 8
    k1 = np.arange(k1p)[:, None].astype(np.float64)
    valid = (np.arange(k1p) < k1h)[:, None]
    th1 = 2.0 * np.pi * k1 * np.arange(n1)[None, :] / n1
    f1 = np.concatenate([np.where(valid, np.cos(th1), 0.0), np.where(valid, -np.sin(th1), 0.0)], axis=0)
    tw = 2.0 * np.pi * k1 * np.arange(DFT_N2)[None, :] / n
    tw_re = np.repeat(np.where(valid, np.cos(tw), 0.0)[:, :, None], LANE, axis=2)
    tw_im = np.repeat(np.where(valid, -np.sin(tw), 0.0)[:, :, None], LANE, axis=2)
    ph = 2.0 * np.pi * np.outer(np.arange(DFT_N2), np.arange(DFT_N2)) / DFT_N2
    f2re, f2im = np.cos(ph), -np.sin(ph)
    w_fwd = np.block([[f2re, -f2im], [f2im, f2re]])
    w_inv = np.block([[f2re, f2im], [-f2im, f2re]])
    wgt = np.where((np.arange(k1p) == 0) | (np.arange(k1p) == n1 // 2), 1.0, 2.0) * (np.arange(k1p) < k1h)
    th_i = 2.0 * np.pi * np.arange(n1 // 2)[:, None] * np.arange(k1p)[None, :] / n1
    g = np.concatenate([wgt * np.cos(th_i), -wgt * np.sin(th_i)], axis=1) / n
    f = lambda a: jnp.asarray(a.astype(np.float32))
    return dict(k1h=k1h, k1p=k1p, f1=f(f1), tw_re=f(tw_re), tw_im=f(tw_im), w_fwd=f(w_fwd), w_inv=f(w_inv), g=f(g))


def _split_bf16(v):
    hi = v.astype(BF16)
    return hi, (v - hi.astype(F32)).astype(BF16)


def _dft_dot_bf16(a, b):
    return jnp.dot(a.astype(BF16), b.astype(BF16), preferred_element_type=F32)


def _dft_dot(a, b):
    ah, al = _split_bf16(a)
    bh, bl = _split_bf16(b)
    out = jnp.dot(ah, bh, preferred_element_type=F32)
    out = out + jnp.dot(ah, bl, preferred_element_type=F32)
    return out + jnp.dot(al, bh, preferred_element_type=F32)


def _dft_first_kernel(k1p, x_ref, f1_ref, twr_ref, twi_ref, ore_ref, oim_ref):
    _, rows, s, cw = x_ref.shape
    reps = cw // LANE
    x = x_ref[0].reshape(rows * s, cw).astype(BF16)
    a = jnp.dot(f1_ref[...], x, preferred_element_type=F32)
    are = a[:k1p * s].reshape(k1p, s, cw)
    aim = a[k1p * s:].reshape(k1p, s, cw)
    tr = jnp.concatenate([twr_ref[...]] * reps, axis=2)
    ti = jnp.concatenate([twi_ref[...]] * reps, axis=2)
    ore_ref[0] = are * tr - aim * ti
    oim_ref[0] = are * ti + aim * tr


def _dft_first(x4, lead0, nlead, f1, tab):
    rows = f1.shape[1]
    cw = x4.shape[3]
    k1p = tab['k1p']
    out = jax.ShapeDtypeStruct((nlead, k1p, DFT_N2, cw), F32)
    ospec = pl.BlockSpec((1, k1p, DFT_S, cw), lambda b, j: (b, 0, j, 0))
    tspec = pl.BlockSpec((k1p, DFT_S, LANE), lambda b, j: (0, j, 0))
    return pl.pallas_call(
        functools.partial(_dft_first_kernel, k1p),
        grid=(nlead, DFT_N2 // DFT_S),
        in_specs=[pl.BlockSpec((1, rows, DFT_S, cw), lambda b, j: (lead0 + b, 0, j, 0)),
                  pl.BlockSpec((2 * k1p * DFT_S, rows * DFT_S), lambda b, j: (0, 0)), tspec, tspec],
        out_specs=[ospec, ospec],
        out_shape=[out, out],
        compiler_params=_cparams("parallel", "parallel"),
        name="hy_dft_first",
    )(x4, jnp.kron(f1, jnp.eye(DFT_S, dtype=F32)).astype(BF16), tab['tw_re'], tab['tw_im'])


DFT_TC = 1024


def _dft_mid_kernel(k1h, fused, *refs):
    if fused:
        are_ref, aim_ref, hre_ref, him_ref, wf_ref, wi_ref, twr_ref, twi_ref, ore_ref, oim_ref = refs
    else:
        are_ref, aim_ref, wf_ref, ore_ref, oim_ref = refs
    n2 = DFT_N2

    @pl.when(pl.program_id(0) < k1h)
    def _():
        a = jnp.concatenate([are_ref[0, 0], aim_ref[0, 0]], axis=0)
        dot = _dft_dot_bf16
        x = dot(wf_ref[...], a)
        xre, xim = x[:n2], x[n2:]
        if not fused:
            ore_ref[0, 0] = xre
            oim_ref[0, 0] = xim
        else:
            hre, him = hre_ref[0, 0], him_ref[0, 0]
            y = jnp.concatenate([xre * hre - xim * him, xre * him + xim * hre], axis=0)
            bb = dot(wi_ref[...], y)
            bre, bim = bb[:n2], bb[n2:]
            twr, twi = twr_ref[0], twi_ref[0]
            for c in range(bre.shape[1] // LANE):
                sl = slice(c * LANE, (c + 1) * LANE)
                ore_ref[0, 0, :, sl] = bre[:, sl] * twr + bim[:, sl] * twi
                oim_ref[0, 0, :, sl] = bim[:, sl] * twr - bre[:, sl] * twi

    @pl.when(pl.program_id(0) >= k1h)
    def _():
        ore_ref[...] = jnp.zeros_like(ore_ref)
        oim_ref[...] = jnp.zeros_like(oim_ref)


def _dft_mid(are, aim, tab, spec=None, order=0):
    bsz, k1p, _, cw = are.shape
    tc = min(DFT_TC, cw)
    blk = pl.BlockSpec((1, 1, DFT_N2, tc), lambda k, c, b: (b, k, 0, c))
    wspec = pl.BlockSpec((2 * DFT_N2, 2 * DFT_N2), lambda k, c, b: (0, 0))
    fused = spec is not None
    if fused:
        hspec = pl.BlockSpec((1, 1, DFT_N2, tc), lambda k, c, b: (order, k, 0, c))
        tspec = pl.BlockSpec((1, DFT_N2, LANE), lambda k, c, b: (k, 0, 0))
        in_specs = [blk, blk, hspec, hspec, wspec, wspec, tspec, tspec]
        args = [are, aim, spec[0], spec[1], tab['w_fwd'].astype(BF16), tab['w_inv'].astype(BF16),
                tab['tw_re'], tab['tw_im']]
    else:
        in_specs = [blk, blk, wspec]
        args = [are, aim, tab['w_fwd']]
    out = jax.ShapeDtypeStruct((bsz, k1p, DFT_N2, cw), F32)
    ore, oim = pl.pallas_call(
        functools.partial(_dft_mid_kernel, tab['k1h'], fused),
        grid=(k1p, cw // tc, bsz),
        in_specs=in_specs,
        out_specs=[blk, blk],
        out_shape=[out, out],
        compiler_params=_cparams("parallel", "parallel", "arbitrary"),
        name="hy_dft_mid_conv" if fused else "hy_dft_mid_filter",
    )(*args)
    return ore, oim


def _dft_last_kernel(bre_ref, bim_ref, g_ref, u_ref, x_ref, d_ref, o_ref):
    _, k1p, s, cw = bre_ref.shape
    rows = u_ref.shape[1]
    bb = jnp.concatenate([bre_ref[0].reshape(k1p * s, cw), bim_ref[0].reshape(k1p * s, cw)], axis=0).astype(BF16)
    y = jnp.dot(g_ref[...], bb, preferred_element_type=F32).reshape(rows, s, cw)
    o_ref[0, 0:rows] = x_ref[0] * (y + u_ref[0] * d_ref[...])
    if o_ref.shape[1] > rows:
        o_ref[0, rows:, :, :] = jnp.zeros((o_ref.shape[1] - rows,) + o_ref.shape[2:], F32)


def _dft_last(bre, bim, tab, u4, u_lead0, x4, x_lead0, dvec, out_rows):
    bsz, k1p, _, cw = bre.shape
    rows = tab['g'].shape[0]
    bspec = pl.BlockSpec((1, k1p, DFT_S, cw), lambda b, j: (b, 0, j, 0))
    return pl.pallas_call(
        _dft_last_kernel,
        grid=(bsz, DFT_N2 // DFT_S),
        in_specs=[bspec, bspec, pl.BlockSpec((rows * DFT_S, 2 * k1p * DFT_S), lambda b, j: (0, 0)),
                  pl.BlockSpec((1, rows, DFT_S, cw), lambda b, j: (u_lead0 + b, 0, j, 0)),
                  pl.BlockSpec((1, rows, DFT_S, cw), lambda b, j: (x_lead0 + b, 0, j, 0)),
                  pl.BlockSpec((1, 1, cw), lambda b, j: (0, 0, 0))],
        out_specs=pl.BlockSpec((1, out_rows, DFT_S, cw), lambda b, j: (b, 0, j, 0)),
        out_shape=jax.ShapeDtypeStruct((bsz, out_rows, DFT_N2, cw), F32),
        compiler_params=_cparams("parallel", "parallel"),
        name="hy_dft_last",
    )(bre, bim, jnp.kron(tab['g'], jnp.eye(DFT_S, dtype=F32)).astype(BF16), u4, x4,
      dvec.reshape(1, 1, cw).astype(F32))


def _ctx_dft_tables(n):
    size = 2 * n
    kp = -(-(n + 1) // 8) * 8
    k = np.arange(kp)[:, None].astype(np.float64)
    valid = (np.arange(kp) <= n)[:, None]
    th = 2.0 * np.pi * k * np.arange(size)[None, :] / size
    fw = np.concatenate([np.where(valid, np.cos(th), 0.0), np.where(valid, -np.sin(th), 0.0)], axis=0)
    wgt = np.where((np.arange(kp) == 0) | (np.arange(kp) == n), 1.0, 2.0) * (np.arange(kp) <= n)
    thi = 2.0 * np.pi * np.arange(n)[:, None] * np.arange(kp)[None, :] / size
    inv = np.concatenate([wgt * np.cos(thi), -wgt * np.sin(thi)], axis=1) / size
    return kp, jnp.asarray(fw.astype(np.float32)), jnp.asarray(inv.astype(np.float32))


def _hy_ctx_kernel(kp, p_ref, f_ref, d_ref, fw_ref, inv_ref, y_ref, o_ref):
    del y_ref
    n = p_ref.shape[2]
    fw = fw_ref[...]
    y = p_ref[0, 0]
    for o in range(HY_ORDER):
        h = _dft_dot(fw, f_ref[o])
        x = _dft_dot(fw[:, :n], y)
        hre, him, xre, xim = h[:kp], h[kp:], x[:kp], x[kp:]
        prod = jnp.concatenate([xre * hre - xim * him, xre * him + xim * hre], axis=0)
        conv = _dft_dot(inv_ref[...], prod)
        y = p_ref[o + 1, 0] * (conv + y * d_ref[o:o + 1, :])
    o_ref[0] = y


CTX_TC = 256


def _hy_ctx(parts, filt_c, long_bias, y_all, n_lat):
    nparts, bsz, t, cw = parts.shape
    n_ctx = t - n_lat
    kp, fw, inv = _ctx_dft_tables(n_ctx)
    return pl.pallas_call(
        functools.partial(_hy_ctx_kernel, kp),
        grid=(bsz, cw // CTX_TC),
        in_specs=[pl.BlockSpec((nparts, 1, n_ctx, CTX_TC), lambda b, c: (0, b, n_lat // n_ctx, c)),
                  pl.BlockSpec((HY_ORDER, 2 * n_ctx, CTX_TC), lambda b, c: (0, 0, c)),
                  pl.BlockSpec((HY_ORDER, CTX_TC), lambda b, c: (0, c)),
                  pl.BlockSpec(fw.shape, lambda b, c: (0, 0)),
                  pl.BlockSpec(inv.shape, lambda b, c: (0, 0)),
                  pl.BlockSpec(memory_space=pl.ANY)],
        out_specs=pl.BlockSpec((1, n_ctx, CTX_TC), lambda b, c: (b, n_lat // n_ctx, c)),
        out_shape=jax.ShapeDtypeStruct(y_all.shape, F32),
        input_output_aliases={5: 0},
        compiler_params=_cparams("parallel", "parallel"),
        name="hy_ctx_conv",
    )(parts, filt_c, long_bias.astype(F32), fw, inv, y_all)


def _hyena(u_all, hy_p, n_lat):
    conv_w, conv_b, f_w1, f_b1, f_fr1, f_w2, f_b2, f_fr2, f_w3, long_bias = hy_p
    bsz, t, _ = u_all.shape
    n_ctx = t - n_lat
    assert n_ctx == DFT_N2 and n_lat % (2 * DFT_N2) == 0
    n1 = 2 * n_lat // DFT_N2
    tab = _dft_tables(n1)
    cw = HY_WIDTH
    parts = _hy_conv(u_all, conv_w, conv_b, n_lat)
    filt_l = _hy_filter(n_lat, f_w1, f_b1, f_fr1, f_w2, f_b2, f_fr2, f_w3)
    filt_c = _hy_filter(n_ctx, f_w1, f_b1, f_fr1, f_w2, f_b2, f_fr2, f_w3)
    spec = _dft_mid(*_dft_first(filt_l.reshape(HY_ORDER, n1, DFT_N2, cw), 0, HY_ORDER, tab['f1'], tab), tab)
    f1_half = tab['f1'][:, :n1 // 2]
    parts4 = parts.reshape((HY_ORDER + 1) * bsz, t // DFT_N2, DFT_N2, cw)
    y4, y_lead0 = parts4, 0
    for o in range(HY_ORDER):
        are, aim = _dft_first(y4, y_lead0, bsz, f1_half, tab)
        bre, bim = _dft_mid(are, aim, tab, spec=spec, order=o)
        last = o + 1 == HY_ORDER
        y4 = _dft_last(bre, bim, tab, y4, y_lead0, parts4, (o + 1) * bsz, long_bias[o],
                       t // DFT_N2 if last else n1 // 2)
        y_lead0 = 0
    return _hy_ctx(parts, filt_c, long_bias, y4.reshape(bsz, t, cw), n_lat)


def _row_tile(m):
    for tm in (1280, 1024, 512, 256):
        if m % tm == 0:
            return tm
    raise ValueError(m)


def _col_tile(n):
    for k in range(n // LANE, 0, -1):
        if n % (k * LANE) == 0 and k * LANE <= 1280:
            return k * LANE
    raise ValueError(n)


def _in_proj(h, w):
    bsz, t, d = h.shape
    n = w.shape[1]
    return _matmul(h.reshape(bsz * t, d), w, _row_tile(bsz * t), _col_tile(n)).reshape(bsz, t, n)


def kernel(x, c, ctx, c_ctx, router_w, router_b, mod_w, mod_b, ln_g, ln_b, exp_w_gate, exp_w_up, exp_w_down, ab_w_in, ab_w_out, hy_conv_w, hy_conv_b, hy_f_w1, hy_f_b1, hy_f_fr1, hy_f_w2, hy_f_b2, hy_f_fr2, hy_f_w3, hy_long_bias, gla_gate_w2, gla_gate_b, gla_norm_g, ssd_w_in, ssd_conv_w, ssd_conv_b, ssd_dt_bias, ssd_a_log, ssd_d, ssd_norm_g, ssd_w_out):
    bsz, n_lat, d = x.shape
    n_ctx = ctx.shape[1]
    assert bsz < 8 and n_lat % ROW_TILE == 0 and n_ctx % ROW_TILE == 0 and d == D_MODEL
    c8 = jnp.zeros((8, d), F32).at[:bsz].set(c).at[bsz].set(c_ctx)
    router_w_pad = jnp.zeros((d, LANE), F32).at[:, :N_EXPERTS].set(router_w)
    mods = [_mod_vectors(c8, mod_w[i].astype(F32), mod_b[i].astype(F32)).reshape(8, 6, d) for i in range(DEPTH)]

    def rows(i, idx_l):
        lat = jnp.stack([mods[i][:bsz, k] for k in idx_l], axis=1)
        cx = jnp.broadcast_to(jnp.stack([mods[i][bsz, k] for k in idx_l], axis=0)[None], lat.shape)
        return jnp.concatenate([lat, cx], axis=1)

    def with_ln(p, g, b):
        extra = jnp.broadcast_to(jnp.stack([g, b], axis=0).astype(F32)[None], (bsz, 2, d))
        return jnp.concatenate([p, extra], axis=1)

    xa, h = _modulate(x, ctx, rows(0, (0, 1)))
    for i in range(DEPTH):
        j = i // 2
        if i % 2 == 0:
            w_pad = -(-AB_IN // LANE) * LANE
            w_in = jnp.zeros((d, w_pad), BF16).at[:, :AB_IN].set(ab_w_in[j].astype(BF16))
            u_all = _in_proj(h, w_in)
            hy_p = (hy_conv_w[j], hy_conv_b[j], hy_f_w1[j], hy_f_b1[j], hy_f_fr1[j], hy_f_w2[j], hy_f_b2[j],
                    hy_f_fr2[j], hy_f_w3[j], hy_long_bias[j])
            y_hy = _hyena(u_all, hy_p, n_lat)
            y_gla = _gla(u_all, gla_gate_w2[j], gla_gate_b[j], gla_norm_g[j], n_lat)
            w_out = ab_w_out[j].astype(BF16)
            a_list, w_list = [y_hy, y_gla], [w_out[:HY_WIDTH], w_out[HY_WIDTH:]]
        else:
            u2 = _in_proj(h, ssd_w_in[j].astype(BF16))
            cw9, cb1 = ssd_conv_w[j].astype(F32), ssd_conv_b[j].astype(F32)
            nx = SSD_D_INNER // LANE
            xs_t = _ssd_conv(u2, cw9, cb1, n_lat, 0, nx, F32)
            bc_t = _ssd_conv(u2, cw9, cb1, n_lat, nx, SSD_CONV_DIM // LANE - nx, BF16)
            y_ssd = _ssd(xs_t, bc_t, u2, ssd_dt_bias[j].astype(F32), ssd_a_log[j], ssd_d[j].astype(F32),
                         ssd_norm_g[j], n_lat)
            a_list, w_list = [y_ssd], [ssd_w_out[j].astype(BF16)]
        p1 = with_ln(rows(i, (2, 4, 3)), ln_g[i, 0], ln_b[i, 0])
        xa, tok, logits = _proj_update(a_list, w_list, xa, p1, router_w_pad, n_lat)
        y0, y1, gates = _moe(tok, logits, router_b, exp_w_gate.astype(F32), exp_w_up.astype(F32),
                             exp_w_down.astype(F32), i)
        gate2 = rows(i, (5,))
        if i + 1 < DEPTH:
            nxt = rows(i + 1, (1, 0))
        else:
            nxt = jnp.zeros((bsz, 4, d), F32)
        p2 = jnp.concatenate([gate2[:, 0:1], nxt[:, 0:2], gate2[:, 1:2], nxt[:, 2:4]], axis=1)
        xa, h = _combine_update(y0, y1, gates, xa, with_ln(p2, ln_g[i, 1], ln_b[i, 1]), n_lat, i + 1 == DEPTH)
    return xa.astype(x.dtype)
```

```python
import functools
import math

import numpy as np
import jax
import jax.numpy as jnp
from jax import lax
from jax.experimental import pallas as pl
from jax.experimental.pallas import tpu as pltpu

F32 = jnp.float32
BF16 = jnp.bfloat16
HIGHEST = lax.Precision.HIGHEST

D_MODEL = 2048
DEPTH = 2
GRID_W = 64
HY_WIDTH = D_MODEL // 2
HY_ORDER = 2
HY_BANDS = 16
HY_MIN_DECAY = math.log(1e-2) / 1.5
HY_MAX_DECAY = math.log(1e-2) / 0.3
GLA_HEADS = 4
GLA_DK = D_MODEL // 4
GLA_DV = D_MODEL // 2
GLA_HEAD_K = GLA_DK // GLA_HEADS
GLA_HEAD_V = GLA_DV // GLA_HEADS
GLA_RANK = 16
GLA_TAU = 16.0
AB_SIZES = ((HY_ORDER + 1) * HY_WIDTH, GLA_DK, GLA_DK, GLA_DV, GLA_DV, GLA_RANK, GLA_RANK)
AB_IN = sum(AB_SIZES)
SSD_D_INNER = 2 * D_MODEL
SSD_HEADDIM = 64
SSD_HEADS = SSD_D_INNER // SSD_HEADDIM
SSD_GROUPS = 8
SSD_HPG = SSD_HEADS // SSD_GROUPS
SSD_STATE = 128
SSD_CONV_DIM = SSD_D_INNER + 2 * SSD_GROUPS * SSD_STATE
SSD_IN = SSD_D_INNER + SSD_CONV_DIM + 2 * SSD_HEADS
N_EXPERTS = 16
N_EXPERT_GROUPS = 4
EXPERTS_PER_GROUP = N_EXPERTS // N_EXPERT_GROUPS
TOP_K = 2
D_EXPERT = D_MODEL // 2
ALPHA = (2 * DEPTH) ** 0.25
EPS = 1e-6

LANE = 128
ROW_TILE = 256
MOE_BLOCK = 512
VMEM_LIMIT = 56 * 1024 * 1024


def _cparams(*sem):
    return pltpu.CompilerParams(dimension_semantics=sem, vmem_limit_bytes=VMEM_LIMIT)


def _silu(x):
    return x * (1.0 / (1.0 + jnp.exp(-x)))


def _mod_kernel(c_ref, w_ref, b_ref, o_ref):
    o_ref[...] = jnp.dot(_silu(c_ref[...]), w_ref[...], precision=HIGHEST, preferred_element_type=F32) + b_ref[...]


def _mod_vectors(c8, w, b):
    d, n = w.shape
    tn = 1536
    return pl.pallas_call(
        _mod_kernel,
        grid=(n // tn,),
        in_specs=[pl.BlockSpec((8, d), lambda j: (0, 0)),
                  pl.BlockSpec((d, tn), lambda j: (0, j)),
                  pl.BlockSpec((1, tn), lambda j: (0, j))],
        out_specs=pl.BlockSpec((8, tn), lambda j: (0, j)),
        out_shape=jax.ShapeDtypeStruct((8, n), F32),
        compiler_params=_cparams("arbitrary"),
        name="adaln_vectors",
    )(c8, w, b.reshape(1, n))


def _modulate_kernel(n_lat_tiles, x_ref, c_ref, p_ref, xa_ref, o_ref):
    i = pl.program_id(1)
    is_ctx = i >= n_lat_tiles
    p = p_ref[0]
    shift = jnp.where(is_ctx, p[2:3], p[0:1])
    scale = jnp.where(is_ctx, p[3:4], p[1:2])
    ci = jnp.maximum(i - n_lat_tiles, 0)
    v = jnp.where(is_ctx, c_ref[0, pl.ds(pl.multiple_of(ci * ROW_TILE, ROW_TILE), ROW_TILE), :], x_ref[0])
    xa_ref[0] = v
    o_ref[0] = (v * (1.0 + scale) + shift).astype(o_ref.dtype)


def _modulate(x, ctx, params):
    bsz, n_lat, d = x.shape
    n_ctx = ctx.shape[1]
    t = n_lat + n_ctx
    nl = n_lat // ROW_TILE
    row = pl.BlockSpec((1, ROW_TILE, d), lambda b, i: (b, i, 0))
    return pl.pallas_call(
        functools.partial(_modulate_kernel, nl),
        grid=(bsz, t // ROW_TILE),
        in_specs=[pl.BlockSpec((1, ROW_TILE, d), lambda b, i: (b, jnp.minimum(i, nl - 1), 0)),
                  pl.BlockSpec((1, n_ctx, d), lambda b, i: (b, 0, 0)),
                  pl.BlockSpec((1, 4, d), lambda b, i: (b, 0, 0))],
        out_specs=[row, row],
        out_shape=[jax.ShapeDtypeStruct((bsz, t, d), F32), jax.ShapeDtypeStruct((bsz, t, d), BF16)],
        compiler_params=_cparams("parallel", "parallel"),
        name="modulate",
    )(x.astype(F32), ctx.astype(F32), params)


def _matmul_kernel(a_ref, w_ref, o_ref):
    o_ref[...] = jnp.dot(a_ref[...], w_ref[...], preferred_element_type=F32).astype(o_ref.dtype)


def _matmul(a, w, tm, tn, out_dtype=F32):
    m, k = a.shape
    n = w.shape[1]
    return pl.pallas_call(
        _matmul_kernel,
        grid=(m // tm, n // tn),
        in_specs=[pl.BlockSpec((tm, k), lambda i, j: (i, 0)),
                  pl.BlockSpec((k, tn), lambda i, j: (0, j))],
        out_specs=pl.BlockSpec((tm, tn), lambda i, j: (i, j)),
        out_shape=jax.ShapeDtypeStruct((m, n), out_dtype),
        compiler_params=_cparams("parallel", "arbitrary"),
        name="in_proj",
    )(a, w)


def _layer_norm_rows(v, g, b):
    mu = jnp.mean(v, axis=-1, keepdims=True)
    vc = v - mu
    var = jnp.mean(vc * vc, axis=-1, keepdims=True)
    return vc * lax.rsqrt(var + EPS) * g + b


def _param_table(p):
    bsz, _, d = p.shape
    pad = jnp.zeros((bsz, 3, d), F32)
    lat = jnp.concatenate([p[:, 0:3], p[:, 6:8], pad], axis=1)
    cx = jnp.concatenate([p[:, 3:6], p[:, 6:8], pad], axis=1)
    return jnp.stack([lat, cx], axis=1).reshape(2 * bsz, 8, d)


def _post_update(tiles_per_batch, n_lat_tiles, tile, y, x, p_ref):
    b = tile // tiles_per_batch
    is_ctx = (tile - b * tiles_per_batch >= n_lat_tiles).astype(jnp.int32)
    p = p_ref[2 * b + is_ctx]
    xn = _layer_norm_rows(ALPHA * x + p[0:1] * y, p[3:4], p[4:5])
    return xn, xn * (1.0 + p[1:2]) + p[2:3]


PROJ_TILES = 2


def _proj_update_kernel(tiles_per_batch, n_lat_tiles, n_a, *refs):
    a_refs = refs[:n_a]
    w_refs = refs[n_a:2 * n_a]
    x_ref, p_ref, rw_ref, xo_ref, tok_ref, lg_ref, y_ref = refs[2 * n_a:]
    i = pl.program_id(0)

    @pl.when(i == 0)
    def _():
        y_ref[...] = jnp.zeros_like(y_ref)

    prev = jnp.maximum(i - 1, 0)
    for hh in range(PROJ_TILES):
        rs = slice(hh * ROW_TILE, (hh + 1) * ROW_TILE)
        xn, tok = _post_update(tiles_per_batch, n_lat_tiles, prev * PROJ_TILES + hh, y_ref[rs, :], x_ref[rs, :], p_ref)
        xo_ref[rs, :] = xn
        tok_ref[rs, :] = tok.astype(tok_ref.dtype)
        lg_ref[rs, :] = _dft_dot(tok, rw_ref[...])
    for hh in range(PROJ_TILES):
        rs = slice(hh * ROW_TILE, (hh + 1) * ROW_TILE)
        y = jnp.dot(a_refs[0][rs, :].astype(BF16), w_refs[0][...], preferred_element_type=F32)
        for a_ref, w_ref in zip(a_refs[1:], w_refs[1:]):
            y = y + jnp.dot(a_ref[rs, :].astype(BF16), w_ref[...], preferred_element_type=F32)
        y_ref[rs, :] = y


def _proj_update(a_list, w_list, x, params, router_w_pad, n_lat):
    bsz, t, d = x.shape
    n = bsz * t
    tm = PROJ_TILES * ROW_TILE
    assert n % tm == 0
    steps = n // tm
    n_a = len(a_list)
    flat = lambda v: v.reshape(n, v.shape[-1])
    cur = lambda i: (jnp.minimum(i, steps - 1), 0)
    prev = lambda i: (jnp.maximum(i - 1, 0), 0)
    in_specs = [pl.BlockSpec((tm, a.shape[-1]), cur) for a in a_list]
    in_specs += [pl.BlockSpec(w.shape, lambda i: (0, 0)) for w in w_list]
    in_specs += [pl.BlockSpec((tm, d), prev),
                 pl.BlockSpec((2 * bsz, 8, d), lambda i: (0, 0, 0)),
                 pl.BlockSpec((d, LANE), lambda i: (0, 0))]
    xo, tok, lg = pl.pallas_call(
        functools.partial(_proj_update_kernel, t // ROW_TILE, n_lat // ROW_TILE, n_a),
        grid=(steps + 1,),
        in_specs=in_specs,
        out_specs=[pl.BlockSpec((tm, d), prev), pl.BlockSpec((tm, d), prev), pl.BlockSpec((tm, LANE), prev)],
        out_shape=[jax.ShapeDtypeStruct((n, d), F32),
                   jax.ShapeDtypeStruct((n, d), BF16),
                   jax.ShapeDtypeStruct((n, LANE), F32)],
        scratch_shapes=[pltpu.VMEM((tm, d), F32)],
        compiler_params=_cparams("arbitrary"),
        name="out_proj_ln",
    )(*[flat(a) for a in a_list], *w_list, flat(x), _param_table(params), router_w_pad)
    return xo.reshape(bsz, t, d), tok.reshape(bsz, t, d), lg.reshape(bsz, t, LANE)


def _route_kernel(lg_ref, rb_ref, tri_ref, oi_ref, og_ref, cnt_ref, carry_ref):
    @pl.when(pl.program_id(0) == 0)
    def _():
        carry_ref[...] = jnp.zeros_like(carry_ref)

    tm = lg_ref.shape[0]
    epg = EXPERTS_PER_GROUP
    lt = lg_ref[...].T[:N_EXPERTS]
    score = 1.0 / (1.0 + jnp.exp(-lt))
    sel = score + rb_ref[...]
    best_g = None
    for q in range(N_EXPERT_GROUPS):
        rows = [sel[q * epg + r:q * epg + r + 1] for r in range(epg)]
        gs = None
        for a in range(epg):
            for b in range(a + 1, epg):
                ps = rows[a] + rows[b]
                gs = ps if gs is None else jnp.maximum(gs, ps)
        if best_g is None:
            best_g, grp = gs, jnp.zeros_like(gs, dtype=jnp.int32)
        else:
            better = gs > best_g
            grp = jnp.where(better, q, grp)
            best_g = jnp.where(better, gs, best_g)
    in_sel, in_score = [], []
    for r in range(epg):
        v = sel[r:r + 1]
        s = score[r:r + 1]
        for q in range(1, N_EXPERT_GROUPS):
            v = jnp.where(grp == q, sel[q * epg + r:q * epg + r + 1], v)
            s = jnp.where(grp == q, score[q * epg + r:q * epg + r + 1], s)
        in_sel.append(v)
        in_score.append(s)
    i1, v1 = jnp.zeros_like(grp), in_sel[0]
    for r in range(1, epg):
        better = in_sel[r] > v1
        i1 = jnp.where(better, r, i1)
        v1 = jnp.where(better, in_sel[r], v1)
    i2, v2 = None, None
    for r in range(epg):
        cand = jnp.where(i1 == r, -jnp.inf, in_sel[r])
        if i2 is None:
            i2, v2 = jnp.zeros_like(grp), cand
        else:
            better = cand > v2
            i2 = jnp.where(better, r, i2)
            v2 = jnp.where(better, cand, v2)
    s1, s2 = in_score[0], in_score[0]
    for r in range(1, epg):
        s1 = jnp.where(i1 == r, in_score[r], s1)
        s2 = jnp.where(i2 == r, in_score[r], s2)
    e1 = grp * epg + i1
    e2 = grp * epg + i2
    tot = s1 + s2
    g1, g2 = s1 / tot, s2 / tot
    eid = lax.broadcasted_iota(jnp.int32, (N_EXPERTS, tm), 0)
    oh1 = eid == e1
    oh2 = eid == e2
    both = jnp.where(oh1 | oh2, 1.0, 0.0)
    before = jnp.dot(both.astype(BF16), tri_ref[...], preferred_element_type=F32) + carry_ref[:, 0:1]
    r1 = jnp.sum(jnp.where(oh1, before, 0.0), axis=0, keepdims=True)
    r2 = jnp.sum(jnp.where(oh2, before, 0.0), axis=0, keepdims=True)
    new_carry = carry_ref[...] + jnp.sum(both, axis=1, keepdims=True)
    carry_ref[...] = new_carry
    cnt_ref[...] = new_carry.astype(jnp.int32)
    zi = jnp.zeros((4, tm), jnp.int32)
    oi_ref[...] = jnp.concatenate([e1, e2, r1.astype(jnp.int32), r2.astype(jnp.int32), zi], axis=0)
    gt = jnp.concatenate([g1, g2, jnp.zeros((LANE - 2, tm), F32)], axis=0)
    og_ref[...] = gt.T


def _route(logits, router_b):
    n = logits.shape[0]
    tm = ROW_TILE
    tri = jnp.asarray(np.triu(np.ones((tm, tm), np.float32), 1)).astype(BF16)
    oi, og, cnt = pl.pallas_call(
        _route_kernel,
        grid=(n // tm,),
        in_specs=[pl.BlockSpec((tm, LANE), lambda i: (i, 0)),
                  pl.BlockSpec((N_EXPERTS, 1), lambda i: (0, 0)),
                  pl.BlockSpec((tm, tm), lambda i: (0, 0))],
        out_specs=[pl.BlockSpec((8, tm), lambda i: (0, i)),
                   pl.BlockSpec((tm, LANE), lambda i: (i, 0)),
                   pl.BlockSpec((N_EXPERTS, LANE), lambda i: (0, 0))],
        out_shape=[jax.ShapeDtypeStruct((8, n), jnp.int32),
                   jax.ShapeDtypeStruct((n, LANE), F32),
                   jax.ShapeDtypeStruct((N_EXPERTS, LANE), jnp.int32)],
        scratch_shapes=[pltpu.VMEM((N_EXPERTS, LANE), F32)],
        compiler_params=_cparams("arbitrary"),
        name="moe_route",
    )(logits, router_b.reshape(N_EXPERTS, 1).astype(F32), tri)
    return oi, og, cnt[:, 0]


W_CONV_ROWS = 256


def _expert_kernel(layer, be_ref, nb_ref, first_ref, nxt_ref, x_ref, wg_hbm, wu_hbm, wd_hbm, o_ref,
                   sg_ref, su_ref, sd_ref, wg_ref, wu_ref, wd_ref, sem):
    i = pl.program_id(0)

    def copies(e):
        return (pltpu.make_async_copy(wg_hbm.at[layer, e], sg_ref, sem.at[0]),
                pltpu.make_async_copy(wu_hbm.at[layer, e], su_ref, sem.at[1]),
                pltpu.make_async_copy(wd_hbm.at[layer, e], sd_ref, sem.at[2]))

    def convert(src, dst):
        def body(r, carry):
            r0 = pl.multiple_of(r * W_CONV_ROWS, W_CONV_ROWS)
            dst[pl.ds(r0, W_CONV_ROWS), :] = src[pl.ds(r0, W_CONV_ROWS), :].astype(BF16)
            return carry
        lax.fori_loop(0, src.shape[0] // W_CONV_ROWS, body, 0)

    active = i < nb_ref[0]

    @pl.when(active & (i == 0))
    def _():
        for cp in copies(be_ref[0]):
            cp.start()

    @pl.when(active & (first_ref[i] == 1))
    def _():
        for cp in copies(be_ref[i]):
            cp.wait()
        convert(sg_ref, wg_ref)
        convert(su_ref, wu_ref)
        convert(sd_ref, wd_ref)

        @pl.when(nxt_ref[i] >= 0)
        def _():
            for cp in copies(nxt_ref[i]):
                cp.start()

    @pl.when(active)
    def _():
        x = x_ref[...]
        hg = jnp.dot(x, wg_ref[...], preferred_element_type=F32)
        hu = jnp.dot(x, wu_ref[...], preferred_element_type=F32)
        h = (_silu(hg) * hu).astype(BF16)
        o_ref[...] = jnp.dot(h, wd_ref[...], preferred_element_type=F32).astype(o_ref.dtype)

    @pl.when(jnp.logical_not(active))
    def _():
        o_ref[...] = jnp.zeros_like(o_ref)


def _expert_blocks(buf, block_expert, n_used, first, nxt, wg, wu, wd, layer):
    rows, d = buf.shape
    n_blocks = rows // MOE_BLOCK
    de = wg.shape[-1]
    grid_spec = pltpu.PrefetchScalarGridSpec(
        num_scalar_prefetch=4,
        grid=(n_blocks,),
        in_specs=[pl.BlockSpec((MOE_BLOCK, d), lambda i, *_: (i, 0)),
                  pl.BlockSpec(memory_space=pl.ANY),
                  pl.BlockSpec(memory_space=pl.ANY),
                  pl.BlockSpec(memory_space=pl.ANY)],
        out_specs=pl.BlockSpec((MOE_BLOCK, d), lambda i, *_: (i, 0)),
        scratch_shapes=[pltpu.VMEM((d, de), F32), pltpu.VMEM((d, de), F32), pltpu.VMEM((de, d), F32),
                        pltpu.VMEM((d, de), BF16), pltpu.VMEM((d, de), BF16), pltpu.VMEM((de, d), BF16),
                        pltpu.SemaphoreType.DMA((3,))],
    )
    return pl.pallas_call(
        functools.partial(_expert_kernel, layer),
        grid_spec=grid_spec,
        out_shape=jax.ShapeDtypeStruct((rows, d), BF16),
        compiler_params=_cparams("arbitrary"),
        name="moe_experts",
    )(block_expert, n_used, first, nxt, buf, wg, wu, wd)


def _combine_update_kernel(tiles_per_batch, n_lat_tiles, last, y0_ref, y1_ref, g_ref, x_ref, p_ref, xo_ref, *h_ref):
    i = pl.program_id(0)
    tile = (i // n_lat_tiles) * tiles_per_batch + i % n_lat_tiles if last else i
    g = g_ref[...]
    y = y0_ref[...].astype(F32) * g[:, 0:1] + y1_ref[...].astype(F32) * g[:, 1:2]
    xn, h = _post_update(tiles_per_batch, n_lat_tiles, tile, y, x_ref[...], p_ref)
    xo_ref[...] = xn
    if not last:
        h_ref[0][...] = h.astype(h_ref[0].dtype)


def _combine_update(y0, y1, gates, x, params, n_lat, last):
    bsz, t, d = x.shape
    n = bsz * t
    tpb, nl = t // ROW_TILE, n_lat // ROW_TILE
    if last:
        src = lambda i: ((i // nl) * tpb + i % nl, 0)
        steps, out_rows = bsz * nl, bsz * n_lat
    else:
        src = lambda i: (i, 0)
        steps, out_rows = n // ROW_TILE, n
    row_in = pl.BlockSpec((ROW_TILE, d), src)
    row_out = pl.BlockSpec((ROW_TILE, d), lambda i: (i, 0))
    out_shape = [jax.ShapeDtypeStruct((out_rows, d), F32)] + ([] if last else [jax.ShapeDtypeStruct((n, d), BF16)])
    outs = pl.pallas_call(
        functools.partial(_combine_update_kernel, tpb, nl, last),
        grid=(steps,),
        in_specs=[row_in, row_in, pl.BlockSpec((ROW_TILE, LANE), src), row_in,
                  pl.BlockSpec((2 * bsz, 8, d), lambda i: (0, 0, 0))],
        out_specs=[row_out] * len(out_shape),
        out_shape=out_shape,
        compiler_params=_cparams("parallel"),
        name="moe_combine_ln",
    )(y0, y1, gates, x.reshape(n, d), _param_table(params))
    if last:
        return outs[0].reshape(bsz, n_lat, d), None
    return outs[0].reshape(bsz, t, d), outs[1].reshape(bsz, t, d)


def _moe(tok, logits, router_b, wg, wu, wd, layer):
    bsz, t, d = tok.shape
    n = bsz * t
    tk = n * TOP_K
    oi, gates, counts = _route(logits.reshape(n, LANE), router_b)
    experts, ranks = oi[0:2], oi[2:4]
    padded = (counts + MOE_BLOCK - 1) // MOE_BLOCK * MOE_BLOCK
    pends = jnp.cumsum(padded)
    pstarts = pends - padded
    eids = jnp.arange(N_EXPERTS, dtype=jnp.int32)
    table = lambda idx, tab: jnp.sum(jnp.where(idx[..., None] == eids, tab.astype(jnp.int32), 0), axis=-1)
    dest = table(experts, pstarts) + ranks
    n_blocks = -(-tk // MOE_BLOCK) + N_EXPERTS
    blk_start = jnp.arange(n_blocks, dtype=jnp.int32) * MOE_BLOCK
    block_expert = jnp.minimum(jnp.sum(pends[None, :] <= blk_start[:, None], axis=1), N_EXPERTS - 1).astype(jnp.int32)
    n_used = (pends[-1] // MOE_BLOCK).astype(jnp.int32).reshape(1)
    prev_e = jnp.concatenate([jnp.full((1,), -1, jnp.int32), block_expert[:-1]])
    first = (block_expert != prev_e).astype(jnp.int32)
    later = jnp.where((counts[None, :] > 0) & (jnp.arange(N_EXPERTS)[None, :] > jnp.arange(N_EXPERTS)[:, None]),
                      jnp.arange(N_EXPERTS)[None, :], N_EXPERTS)
    nxt_e = jnp.min(later, axis=1)
    nxt_e = jnp.where(nxt_e >= N_EXPERTS, -1, nxt_e).astype(jnp.int32)
    nxt = table(block_expert, nxt_e)
    tok_id = jnp.broadcast_to(jnp.arange(n, dtype=jnp.int32)[None], (TOP_K, n))
    src = jnp.zeros((n_blocks * MOE_BLOCK,), jnp.int32).at[dest.reshape(-1)].set(
        tok_id.reshape(-1), mode='promise_in_bounds', unique_indices=True)
    take = lambda a, idx: a.at[idx].get(mode='promise_in_bounds')
    buf = take(tok.reshape(n, d), src)
    out = _expert_blocks(buf, block_expert, n_used, first, nxt, wg, wu, wd, layer)
    return take(out, dest[0]), take(out, dest[1]), gates


GLA_CS = 64
EXP_CLAMP = 80.0


def _gla_kernel(reverse, has_prev, nsub, *refs):
    if has_prev:
        (q_ref, k_ref, v_ref, g1_ref, w2_ref, gb_ref, prev_ref, r_ref, ng_ref, o_ref, s_ref) = refs
    else:
        (q_ref, k_ref, v_ref, g1_ref, w2_ref, gb_ref, o_ref, s_ref) = refs

    @pl.when(pl.program_id(1) == 0)
    def _():
        s_ref[...] = jnp.zeros_like(s_ref)

    cs = GLA_CS
    row = lax.broadcasted_iota(jnp.int32, (cs, cs), 0)
    col = lax.broadcasted_iota(jnp.int32, (cs, cs), 1)
    keep = (col >= row) if reverse else (col <= row)
    cum_m = keep.astype(BF16)
    z = _dft_dot(g1_ref[0], w2_ref[0]) + gb_ref[0]
    log_gate = (jnp.minimum(z, 0.0) - jnp.log(1.0 + jnp.exp(-jnp.abs(z)))) * (1.0 / GLA_TAU)
    q_all = q_ref[0] * (GLA_HEAD_K ** -0.5)
    k_all = k_ref[0]
    v_all = v_ref[0]
    last = 0 if reverse else cs - 1
    states = [s_ref[h] for h in range(GLA_HEADS)]
    for c in (range(nsub - 1, -1, -1) if reverse else range(nsub)):
        rs = slice(c * cs, (c + 1) * cs)
        lg_hi, lg_lo = _split_bf16(log_gate[rs])
        b_all = jnp.dot(cum_m, lg_hi, preferred_element_type=F32) + jnp.dot(cum_m, lg_lo, preferred_element_type=F32)
        for h in range(GLA_HEADS):
            ks = slice(h * GLA_HEAD_K, (h + 1) * GLA_HEAD_K)
            vs = slice(h * GLA_HEAD_V, (h + 1) * GLA_HEAD_V)
            b = b_all[:, ks]
            b_last = b[last:last + 1]
            ref = 0.5 * b_last
            qc, kc, vc = q_all[rs, ks], k_all[rs, ks], v_all[rs, vs].astype(BF16)
            q_in = (qc * jnp.exp(jnp.minimum(b - ref, EXP_CLAMP))).astype(BF16)
            k_in = (kc * jnp.exp(jnp.minimum(ref - b, EXP_CLAMP))).astype(BF16)
            att = lax.dot_general(q_in, k_in, (((1,), (1,)), ((), ())), preferred_element_type=F32)
            att = jnp.where(keep, att, 0.0).astype(BF16)
            o_h = jnp.dot(att, vc, preferred_element_type=F32)
            o_h = o_h + lax.dot_general((qc * jnp.exp(b)).astype(BF16), states[h].astype(BF16),
                                        (((1,), (1,)), ((), ())), preferred_element_type=F32)
            k_tail = (kc * jnp.exp(b_last - b)).astype(BF16)
            upd = lax.dot_general(vc, k_tail, (((0,), (0,)), ((), ())), preferred_element_type=F32)
            states[h] = jnp.exp(b_last) * states[h] + upd
            if has_prev:
                o_h = o_h + prev_ref[0, rs, vs]
                o_h = o_h * lax.rsqrt(jnp.mean(o_h * o_h, axis=-1, keepdims=True) + EPS) * ng_ref[:, vs]
                o_h = o_h * _silu(r_ref[0, rs, vs])
            o_ref[0, rs, vs] = o_h.astype(o_ref.dtype)
    for h in range(GLA_HEADS):
        s_ref[h] = states[h]


def _gla_direction(u_all, w2pad, gbias, n_lat, reverse, prev=None, norm_g=None):
    bsz, t, _ = u_all.shape
    n = t // ROW_TILE
    nl = n_lat // ROW_TILE
    nsub = ROW_TILE // GLA_CS
    if reverse:
        blk = lambda s: n - 1 - s
    else:
        blk = lambda s: (s + nl) % n
    q0, k0, v0, r0, g0 = (AB_SIZES[0] // GLA_DK, (AB_SIZES[0] + GLA_DK) // GLA_DK, (AB_SIZES[0] + 2 * GLA_DK) // GLA_DV,
                          (AB_SIZES[0] + 2 * GLA_DK + GLA_DV) // GLA_DV, (AB_IN - 2 * GLA_RANK) // LANE)
    d = 1 if reverse else 0
    in_specs = [pl.BlockSpec((1, ROW_TILE, GLA_DK), lambda b, s: (b, blk(s), q0)),
                pl.BlockSpec((1, ROW_TILE, GLA_DK), lambda b, s: (b, blk(s), k0)),
                pl.BlockSpec((1, ROW_TILE, GLA_DV), lambda b, s: (b, blk(s), v0)),
                pl.BlockSpec((1, ROW_TILE, LANE), lambda b, s: (b, blk(s), g0)),
                pl.BlockSpec((1, LANE, GLA_DK), lambda b, s: (d, 0, 0)),
                pl.BlockSpec((1, 1, GLA_DK), lambda b, s: (d, 0, 0))]
    args = [u_all, u_all, u_all, u_all, w2pad, gbias]
    has_prev = prev is not None
    if has_prev:
        in_specs += [pl.BlockSpec((1, ROW_TILE, GLA_DV), lambda b, s: (b, blk(s), 0)),
                     pl.BlockSpec((1, ROW_TILE, GLA_DV), lambda b, s: (b, blk(s), r0)),
                     pl.BlockSpec((1, GLA_DV), lambda b, s: (0, 0))]
        args += [prev, u_all, norm_g.reshape(1, GLA_DV)]
    return pl.pallas_call(
        functools.partial(_gla_kernel, reverse, has_prev, nsub),
        grid=(bsz, n),
        in_specs=in_specs,
        out_specs=pl.BlockSpec((1, ROW_TILE, GLA_DV), lambda b, s: (b, blk(s), 0)),
        out_shape=jax.ShapeDtypeStruct((bsz, t, GLA_DV), BF16 if has_prev else F32),
        scratch_shapes=[pltpu.VMEM((GLA_HEADS, GLA_HEAD_V, GLA_HEAD_K), F32)],
        compiler_params=_cparams("parallel", "arbitrary"),
        name="gla_bwd_norm" if reverse else "gla_fwd",
    )(*args)


def _gla(u_all, gate_w2, gate_b, norm_g, n_lat):
    w2pad = jnp.zeros((2, LANE, GLA_DK), F32)
    w2pad = w2pad.at[0, :GLA_RANK].set(gate_w2[0]).at[1, GLA_RANK:2 * GLA_RANK].set(gate_w2[1])
    gbias = gate_b.reshape(2, 1, GLA_DK).astype(F32)
    o_f = _gla_direction(u_all, w2pad, gbias, n_lat, False)
    return _gla_direction(u_all, w2pad, gbias, n_lat, True, prev=o_f, norm_g=norm_g.astype(F32))


CONV_CHUNK = 512


def _shift_rows(v, dh):
    return v if dh == 0 else pltpu.roll(v, (-dh) % v.shape[0], 0)


def _ssd_conv_kernel(n_lat, x_ref, w_ref, b_ref, o_ref, pad_ref):
    t = x_ref.shape[1]
    n_ctx = t - n_lat
    pad_ref[0:GRID_W, :] = jnp.zeros((GRID_W, LANE), F32)
    pad_ref[GRID_W + n_lat:2 * GRID_W + n_lat, :] = jnp.zeros((GRID_W, LANE), F32)
    pad_ref[GRID_W:GRID_W + n_lat, :] = x_ref[0, 0:n_lat, :]
    w = w_ref[...]
    bias = b_ref[...]
    col = lax.broadcasted_iota(jnp.int32, (CONV_CHUNK, LANE), 0) % GRID_W
    masks = {-1: col >= 1, 0: None, 1: col <= GRID_W - 2}

    def body(ci, carry):
        t0 = pl.multiple_of(ci * CONV_CHUNK, CONV_CHUNK)
        slabs = [pad_ref[pl.ds(t0 + i * GRID_W, CONV_CHUNK), :] for i in range(3)]
        part = [sum(slabs[i] * w[i * 3 + j:i * 3 + j + 1] for i in range(3)) for j in range(3)]
        acc = part[1] + bias
        acc = acc + jnp.where(masks[-1], _shift_rows(part[0], -1), 0.0)
        acc = acc + jnp.where(masks[1], _shift_rows(part[2], 1), 0.0)
        o_ref[0, 0, pl.ds(t0, CONV_CHUNK), :] = _silu(acc).astype(o_ref.dtype)
        return carry

    lax.fori_loop(0, n_lat // CONV_CHUNK, body, 0)
    xc = x_ref[0, n_lat:t, :]
    pos = lax.broadcasted_iota(jnp.int32, (n_ctx, LANE), 0)
    acc = xc * w[4:5] + bias
    acc = acc + jnp.where(pos >= 1, _shift_rows(xc, -1), 0.0) * w[3:4]
    acc = acc + jnp.where(pos <= n_ctx - 2, _shift_rows(xc, 1), 0.0) * w[5:6]
    o_ref[0, 0, n_lat:t, :] = _silu(acc).astype(o_ref.dtype)


def _ssd_conv(u2, conv_w, conv_b, n_lat, tile0, n_tiles, out_dtype):
    bsz, t, _ = u2.shape
    c0 = SSD_D_INNER // LANE + tile0
    return pl.pallas_call(
        functools.partial(_ssd_conv_kernel, n_lat),
        grid=(bsz, n_tiles),
        in_specs=[pl.BlockSpec((1, t, LANE), lambda b, j: (b, 0, c0 + j)),
                  pl.BlockSpec((9, LANE), lambda b, j: (0, tile0 + j)),
                  pl.BlockSpec((1, LANE), lambda b, j: (0, tile0 + j))],
        out_specs=pl.BlockSpec((1, 1, t, LANE), lambda b, j: (b, j, 0, 0)),
        out_shape=jax.ShapeDtypeStruct((bsz, n_tiles, t, LANE), out_dtype),
        scratch_shapes=[pltpu.VMEM((n_lat + 2 * GRID_W, LANE), F32)],
        compiler_params=_cparams("parallel", "parallel"),
        name="ssd_conv",
    )(u2, conv_w.reshape(9, SSD_CONV_DIM), conv_b.reshape(1, SSD_CONV_DIM))


SSD_CS = 128


def _softplus(x):
    return jnp.maximum(x, 0.0) + jnp.log(1.0 + jnp.exp(-jnp.abs(x)))


def _spread_dot(v, onehot):
    hi = v.astype(BF16)
    lo = (v - hi.astype(F32)).astype(BF16)
    return jnp.dot(hi, onehot, preferred_element_type=F32) + jnp.dot(lo, onehot, preferred_element_type=F32)


def _ssd_kernel(reverse, has_prev, nsub, *refs):
    if has_prev:
        (x_ref, b_ref, c_ref, dt_ref, p_ref, expq_ref, dsk_ref, prev_ref, z_ref, ng_ref, o_ref,
         s_ref, at_ref, dtt_ref, acs_ref, eag_ref, twg_ref) = refs
    else:
        (x_ref, b_ref, c_ref, dt_ref, p_ref, expq_ref, dsk_ref, o_ref,
         s_ref, at_ref, dtt_ref, acs_ref, eag_ref, twg_ref) = refs

    @pl.when(pl.program_id(1) == 0)
    def _():
        s_ref[...] = jnp.zeros_like(s_ref)

    cs = SSD_CS
    nh = SSD_HEADS
    tpg = SSD_HPG * SSD_HEADDIM // LANE
    gw = SSD_HPG * SSD_HEADDIM
    row = lax.broadcasted_iota(jnp.int32, (cs, cs), 0)
    col = lax.broadcasted_iota(jnp.int32, (cs, cs), 1)
    keep = (col >= row) if reverse else (col <= row)
    cum_m = keep.astype(BF16)
    last = 0 if reverse else cs - 1
    lane = lax.broadcasted_iota(jnp.int32, (cs, LANE), 1)
    d0 = nh if reverse else 0
    p = p_ref[...]
    dtv = _softplus(dt_ref[0][:, d0:d0 + nh] + p[0:1, d0:d0 + nh])
    a_all = dtv * p[1:2, d0:d0 + nh]
    for c in (range(nsub - 1, -1, -1) if reverse else range(nsub)):
        rs = slice(c * cs, (c + 1) * cs)
        a = a_all[rs]
        hi = a.astype(BF16)
        r1 = a - hi.astype(F32)
        mid = r1.astype(BF16)
        lo = (r1 - mid.astype(F32)).astype(BF16)
        acs = (jnp.dot(cum_m, hi, preferred_element_type=F32) + jnp.dot(cum_m, mid, preferred_element_type=F32)
               + jnp.dot(cum_m, lo, preferred_element_type=F32))
        a_last = acs[last:last + 1]
        acs_ref[...] = acs
        e_acs = jnp.exp(acs)
        tail_w = jnp.exp(a_last - acs) * dtv[rs]
        for gg in range(SSD_GROUPS):
            eag_ref[gg] = e_acs[:, gg * SSD_HPG:(gg + 1) * SSD_HPG]
            twg_ref[gg] = tail_w[:, gg * SSD_HPG:(gg + 1) * SSD_HPG]
        at_ref[...] = acs.T
        dtt_ref[...] = dtv[rs].T

        def group(g, carry):
            h0 = pl.multiple_of(g * SSD_HPG, SSD_HPG)
            bg = b_ref[0, g, rs, :].astype(BF16)
            cg = c_ref[0, g, rs, :].astype(BF16)
            cb = lax.dot_general(cg, bg, (((1,), (1,)), ((), ())), preferred_element_type=F32)
            colb = _spread_dot(acs_ref[...], expq_ref[g])
            rows8 = at_ref[pl.ds(h0, SSD_HPG), :]
            dt8 = dtt_ref[pl.ds(h0, SSD_HPG), :]
            ms = []
            for r in range(SSD_HPG):
                diff = colb[:, r * LANE:(r + 1) * LANE] - rows8[r:r + 1, :]
                decay = jnp.where(keep, jnp.exp(jnp.minimum(diff, 0.0)), 0.0)
                ms.append((cb * decay * dt8[r:r + 1, :]).astype(BF16))
            ea_g = eag_ref[g]
            tw_g = twg_ref[g]
            state = s_ref[g]
            inter = jnp.dot(cg, state.astype(BF16), preferred_element_type=F32)
            xs = [x_ref[0, g * tpg + k, rs, :] for k in range(tpg)]
            dsk = dsk_ref[g]
            first_head = lane < SSD_HEADDIM
            ys, xws, e_last = [], [], []
            for k in range(tpg):
                bd = jnp.concatenate([jnp.where(first_head, xs[k], 0.0),
                                      jnp.where(first_head, 0.0, xs[k])], axis=0).astype(BF16)
                m2 = jnp.concatenate(ms[2 * k:2 * k + 2], axis=1)
                sl = slice(k * LANE, (k + 1) * LANE)
                e_t = jnp.where(first_head, ea_g[:, 2 * k:2 * k + 1], ea_g[:, 2 * k + 1:2 * k + 2])
                w_t = jnp.where(first_head, tw_g[:, 2 * k:2 * k + 1], tw_g[:, 2 * k + 1:2 * k + 2])
                y = jnp.dot(m2, bd, preferred_element_type=F32)
                ys.append(y + e_t * inter[:, sl] + dsk[:, sl] * xs[k])
                xws.append((xs[k] * w_t).astype(BF16))
                e_last.append(e_t[last:last + 1])
            upd = lax.dot_general(bg, jnp.concatenate(xws, axis=1), (((0,), (0,)), ((), ())),
                                  preferred_element_type=F32)
            s_ref[g] = jnp.concatenate(e_last, axis=1) * state + upd
            if has_prev:
                off = pl.multiple_of(g * gw, gw)
                z = z_ref[0, rs, pl.ds(off, gw)]
                yt = [(ys[k] + prev_ref[0, g * tpg + k, rs, :]) * _silu(z[:, k * LANE:(k + 1) * LANE])
                      for k in range(tpg)]
                ss = sum(jnp.sum(v * v, axis=-1, keepdims=True) for v in yt)
                inv = lax.rsqrt(ss * (1.0 / gw) + EPS)
                ng = ng_ref[:, pl.ds(off, gw)]
                o_ref[0, rs, pl.ds(off, gw)] = (jnp.concatenate(yt, axis=1) * inv * ng).astype(o_ref.dtype)
            else:
                for k in range(tpg):
                    o_ref[0, g * tpg + k, rs, :] = ys[k]
            return carry

        lax.fori_loop(0, SSD_GROUPS, group, 0, unroll=2)


def _ssd_direction(xs_t, bc_t, u2, pvec, exp_q, dsk, n_lat, reverse, prev=None, norm_g=None):
    bsz, _, t, _ = xs_t.shape
    n = t // ROW_TILE
    nl = n_lat // ROW_TILE
    nsub = ROW_TILE // SSD_CS
    nx = SSD_D_INNER // LANE
    gw = SSD_HPG * SSD_HEADDIM
    if reverse:
        blk = lambda s: n - 1 - s
    else:
        blk = lambda s: (s + nl) % n
    d = 1 if reverse else 0
    full = lambda a: pl.BlockSpec(a.shape, lambda b, s: (0,) * a.ndim)
    in_specs = [pl.BlockSpec((1, nx, ROW_TILE, LANE), lambda b, s: (b, 0, blk(s), 0)),
                pl.BlockSpec((1, SSD_GROUPS, ROW_TILE, LANE), lambda b, s: (b, 0, blk(s), 0)),
                pl.BlockSpec((1, SSD_GROUPS, ROW_TILE, LANE), lambda b, s: (b, 1, blk(s), 0)),
                pl.BlockSpec((1, ROW_TILE, LANE), lambda b, s: (b, blk(s), (SSD_IN - 2 * SSD_HEADS) // LANE)),
                full(pvec), full(exp_q),
                pl.BlockSpec((SSD_GROUPS, 1, gw), lambda b, s: (d, 0, 0))]
    args = [xs_t, bc_t, bc_t, u2, pvec, exp_q, dsk.reshape(2 * SSD_GROUPS, 1, gw)]
    has_prev = prev is not None
    if has_prev:
        in_specs += [pl.BlockSpec((1, nx, ROW_TILE, LANE), lambda b, s: (b, 0, blk(s), 0)),
                     pl.BlockSpec((1, ROW_TILE, SSD_D_INNER), lambda b, s: (b, blk(s), 0)),
                     pl.BlockSpec((1, SSD_D_INNER), lambda b, s: (0, 0))]
        args += [prev, u2, norm_g.reshape(1, SSD_D_INNER)]
        out_spec = pl.BlockSpec((1, ROW_TILE, SSD_D_INNER), lambda b, s: (b, blk(s), 0))
        out_shape = jax.ShapeDtypeStruct((bsz, t, SSD_D_INNER), BF16)
    else:
        out_spec = pl.BlockSpec((1, nx, ROW_TILE, LANE), lambda b, s: (b, 0, blk(s), 0))
        out_shape = jax.ShapeDtypeStruct((bsz, nx, t, LANE), F32)
    return pl.pallas_call(
        functools.partial(_ssd_kernel, reverse, has_prev, nsub),
        grid=(bsz, n),
        in_specs=in_specs,
        out_specs=out_spec,
        out_shape=out_shape,
        scratch_shapes=[pltpu.VMEM((SSD_GROUPS, SSD_STATE, gw), F32),
                        pltpu.VMEM((SSD_HEADS, SSD_CS), F32), pltpu.VMEM((SSD_HEADS, SSD_CS), F32),
                        pltpu.VMEM((SSD_CS, SSD_HEADS), F32),
                        pltpu.VMEM((SSD_GROUPS, SSD_CS, SSD_HPG), F32),
                        pltpu.VMEM((SSD_GROUPS, SSD_CS, SSD_HPG), F32)],
        compiler_params=_cparams("parallel", "arbitrary"),
        name="ssd_bwd_norm" if reverse else "ssd_fwd",
    )(*args)


def _ssd(xs_t, bc_t, u2, dt_bias, a_log, d_skip, norm_g, n_lat):
    gw = SSD_HPG * SSD_HEADDIM
    pvec = jnp.zeros((8, LANE), F32)
    pvec = pvec.at[0].set(dt_bias.reshape(-1)).at[1].set(-jnp.exp(a_log.astype(F32)).reshape(-1))
    exp_q = np.zeros((SSD_GROUPS, SSD_HEADS, SSD_HPG * LANE), np.float32)
    for gg in range(SSD_GROUPS):
        for r in range(SSD_HPG):
            exp_q[gg, gg * SSD_HPG + r, r * LANE:(r + 1) * LANE] = 1.0
    exp_q = jnp.asarray(exp_q).astype(BF16)
    dsk = jnp.repeat(d_skip.astype(F32).reshape(2, SSD_GROUPS, SSD_HPG), SSD_HEADDIM, axis=-1)
    y_f = _ssd_direction(xs_t, bc_t, u2, pvec, exp_q, dsk, n_lat, False)
    return _ssd_direction(xs_t, bc_t, u2, pvec, exp_q, dsk, n_lat, True, prev=y_f, norm_g=norm_g.astype(F32))


HY_PAD = 8


def _hy_conv_kernel(n_lat, x_ref, w_ref, b_ref, o_ref, pad_ref):
    t = x_ref.shape[1]
    w = w_ref[...]
    bias = b_ref[...]
    ch = CONV_CHUNK
    pad_ref[0:HY_PAD, :] = jnp.zeros((HY_PAD, LANE), F32)
    pad_ref[HY_PAD + n_lat:2 * HY_PAD + n_lat, :] = jnp.zeros((HY_PAD, LANE), F32)
    pad_ref[HY_PAD:HY_PAD + n_lat, :] = x_ref[0, 0:n_lat, :]
    rowi = lax.broadcasted_iota(jnp.int32, (ch, LANE), 0)

    def body(ci, carry):
        t0 = pl.multiple_of(ci * ch, ch)
        cur = pad_ref[pl.ds(t0 + HY_PAD, ch), :]
        before = pad_ref[pl.ds(t0, HY_PAD), :][HY_PAD - 1:HY_PAD]
        after = pad_ref[pl.ds(t0 + HY_PAD + ch, HY_PAD), :][0:1]
        down = jnp.where(rowi == 0, before, _shift_rows(cur, -1))
        up = jnp.where(rowi == ch - 1, after, _shift_rows(cur, 1))
        o_ref[0, 0, pl.ds(t0, ch), :] = down * w[0:1] + cur * w[1:2] + up * w[2:3] + bias
        return carry

    lax.fori_loop(0, n_lat // ch, body, 0)
    n_ctx = t - n_lat
    xc = x_ref[0, n_lat:t, :]
    pos = lax.broadcasted_iota(jnp.int32, (n_ctx, LANE), 0)
    acc = xc * w[1:2] + bias
    acc = acc + jnp.where(pos >= 1, _shift_rows(xc, -1), 0.0) * w[0:1]
    acc = acc + jnp.where(pos <= n_ctx - 2, _shift_rows(xc, 1), 0.0) * w[2:3]
    o_ref[0, 0, n_lat:t, :] = acc


def _hy_conv(u_all, conv_w, conv_b, n_lat):
    bsz, t, _ = u_all.shape
    nch = (HY_ORDER + 1) * HY_WIDTH
    tpp = HY_WIDTH // LANE
    return pl.pallas_call(
        functools.partial(_hy_conv_kernel, n_lat),
        grid=(bsz, nch // LANE),
        in_specs=[pl.BlockSpec((1, t, LANE), lambda b, j: (b, 0, j)),
                  pl.BlockSpec((3, LANE), lambda b, j: (0, j)),
                  pl.BlockSpec((1, LANE), lambda b, j: (0, j))],
        out_specs=pl.BlockSpec((1, 1, t, LANE), lambda b, j: (j // tpp, b, 0, j % tpp)),
        out_shape=jax.ShapeDtypeStruct((HY_ORDER + 1, bsz, t, HY_WIDTH), F32),
        scratch_shapes=[pltpu.VMEM((n_lat + 2 * HY_PAD, LANE), F32)],
        compiler_params=_cparams("parallel", "parallel"),
        name="hy_short_conv",
    )(u_all, conv_w, conv_b.reshape(1, nch))


FILT_ROWS = 256


def _hy_filter_kernel(seq_len, bands_ref, w1t_ref, b1_ref, fr1_ref, w2t_ref, b2_ref, fr2_ref, w3a_ref, w3b_ref,
                      dl_ref, o_ref):
    i = pl.program_id(0)

    def lag_of(shape, axis):
        n = i * FILT_ROWS + lax.broadcasted_iota(jnp.int32, shape, axis)
        return n, jnp.where(n < seq_len, n, 2 * seq_len - n).astype(F32)

    _, pos_r = lag_of((1, FILT_ROWS), 1)
    tt_r = pos_r / max(seq_len - 1, 1)
    ang = 2.0 * math.pi * bands_ref[...] * pos_r / seq_len
    w1t = w1t_ref[...]
    nb = HY_BANDS
    hp = functools.partial(jnp.dot, precision=HIGHEST, preferred_element_type=F32)
    pre = w1t[:, 0:1] * tt_r + hp(w1t[:, 1:1 + nb], jnp.cos(ang)) - hp(w1t[:, 1 + nb:1 + 2 * nb], jnp.sin(ang))
    hid = jnp.sin(fr1_ref[...] * (pre + b1_ref[...]))
    hid = jnp.sin(fr2_ref[...] * (hp(w2t_ref[...], hid) + b2_ref[...]))
    n_c, pos_c = lag_of((FILT_ROWS, 1), 0)
    decay = jnp.exp(-(pos_c / max(seq_len - 1, 1)) * dl_ref[...])
    hid_rows = hid.T.astype(BF16)
    for o, w3_ref in enumerate((w3a_ref, w3b_ref)):
        h = jnp.dot(hid_rows, w3_ref[...].astype(BF16), preferred_element_type=F32)
        o_ref[o] = jnp.where(n_c == seq_len, 0.0, h * decay)


def _hy_filter(seq_len, f_w1, f_b1, f_fr1, f_w2, f_b2, f_fr2, f_w3):
    assert HY_ORDER == 2
    fh = f_w1.shape[1]
    emb = f_w1.shape[0]
    half = seq_len // FILT_ROWS
    bands = jnp.asarray(np.linspace(1e-4, HY_BANDS - 1, HY_BANDS, dtype=np.float32)).reshape(HY_BANDS, 1)
    deltas = jnp.asarray(np.abs(np.linspace(HY_MIN_DECAY, HY_MAX_DECAY, HY_WIDTH, dtype=np.float32))).reshape(1, HY_WIDTH)
    col = lambda v: v.reshape(fh, 1).astype(F32)
    full = lambda shape: pl.BlockSpec(shape, lambda i: (0,) * len(shape))
    w3 = f_w3.astype(F32)
    return pl.pallas_call(
        functools.partial(_hy_filter_kernel, seq_len),
        grid=(2 * half,),
        in_specs=[full((HY_BANDS, 1)), full((fh, emb)), full((fh, 1)), full((fh, 1)), full((fh, fh)), full((fh, 1)),
                  full((fh, 1)),
                  pl.BlockSpec((fh, HY_WIDTH), lambda i: (0, i // half)),
                  pl.BlockSpec((fh, HY_WIDTH), lambda i: (0, 2 + i // half)),
                  full((1, HY_WIDTH))],
        out_specs=pl.BlockSpec((HY_ORDER, FILT_ROWS, HY_WIDTH), lambda i: (0, i, 0)),
        out_shape=jax.ShapeDtypeStruct((HY_ORDER, 2 * seq_len, HY_WIDTH), F32),
        compiler_params=_cparams("parallel"),
        name="hy_filter",
    )(bands, f_w1.astype(F32).T, col(f_b1), col(f_fr1), f_w2.astype(F32).T, col(f_b2), col(f_fr2), w3, w3, deltas)


DFT_N2 = 256
DFT_S = 8


def _dft_tables(n1):
    n = n1 * DFT_N2
    k1h = n1 // 2 + 1
    k1p = -(-k1h // 8) * 8
    k1 = np.arange(k1p)[:, None].astype(np.float64)
    valid = (np.arange(k1p) < k1h)[:, None]
    th1 = 2.0 * np.pi * k1 * np.arange(n1)[None, :] / n1
    f1 = np.concatenate([np.where(valid, np.cos(th1), 0.0), np.where(valid, -np.sin(th1), 0.0)], axis=0)
    tw = 2.0 * np.pi * k1 * np.arange(DFT_N2)[None, :] / n
    tw_re = np.repeat(np.where(valid, np.cos(tw), 0.0)[:, :, None], LANE, axis=2)
    tw_im = np.repeat(np.where(valid, -np.sin(tw), 0.0)[:, :, None], LANE, axis=2)
    ph = 2.0 * np.pi * np.outer(np.arange(DFT_N2), np.arange(DFT_N2)) / DFT_N2
    f2re, f2im = np.cos(ph), -np.sin(ph)
    w_fwd = np.block([[f2re, -f2im], [f2im, f2re]])
    w_inv = np.block([[f2re, f2im], [-f2im, f2re]])
    wgt = np.where((np.arange(k1p) == 0) | (np.arange(k1p) == n1 // 2), 1.0, 2.0) * (np.arange(k1p) < k1h)
    th_i = 2.0 * np.pi * np.arange(n1 // 2)[:, None] * np.arange(k1p)[None, :] / n1
    g = np.concatenate([wgt * np.cos(th_i), -wgt * np.sin(th_i)], axis=1) / n
    f = lambda a: jnp.asarray(a.astype(np.float32))
    return dict(k1h=k1h, k1p=k1p, f1=f(f1), tw_re=f(tw_re), tw_im=f(tw_im), w_fwd=f(w_fwd), w_inv=f(w_inv), g=f(g))


def _split_bf16(v):
    hi = v.astype(BF16)
    return hi, (v - hi.astype(F32)).astype(BF16)


def _dft_dot_bf16(a, b):
    return jnp.dot(a.astype(BF16), b.astype(BF16), preferred_element_type=F32)


def _dft_dot(a, b):
    ah, al = _split_bf16(a)
    bh, bl = _split_bf16(b)
    out = jnp.dot(ah, bh, preferred_element_type=F32)
    out = out + jnp.dot(ah, bl, preferred_element_type=F32)
    return out + jnp.dot(al, bh, preferred_element_type=F32)


def _dft_first_kernel(k1p, x_ref, f1_ref, twr_ref, twi_ref, ore_ref, oim_ref):
    _, rows, s, cw = x_ref.shape
    reps = cw // LANE
    x = x_ref[0].reshape(rows * s, cw).astype(BF16)
    a = jnp.dot(f1_ref[...], x, preferred_element_type=F32)
    are = a[:k1p * s].reshape(k1p, s, cw)
    aim = a[k1p * s:].reshape(k1p, s, cw)
    tr = jnp.concatenate([twr_ref[...]] * reps, axis=2)
    ti = jnp.concatenate([twi_ref[...]] * reps, axis=2)
    ore_ref[0] = are * tr - aim * ti
    oim_ref[0] = are * ti + aim * tr


def _dft_first(x4, lead0, nlead, f1, tab):
    rows = f1.shape[1]
    cw = x4.shape[3]
    k1p = tab['k1p']
    out = jax.ShapeDtypeStruct((nlead, k1p, DFT_N2, cw), F32)
    ospec = pl.BlockSpec((1, k1p, DFT_S, cw), lambda b, j: (b, 0, j, 0))
    tspec = pl.BlockSpec((k1p, DFT_S, LANE), lambda b, j: (0, j, 0))
    return pl.pallas_call(
        functools.partial(_dft_first_kernel, k1p),
        grid=(nlead, DFT_N2 // DFT_S),
        in_specs=[pl.BlockSpec((1, rows, DFT_S, cw), lambda b, j: (lead0 + b, 0, j, 0)),
                  pl.BlockSpec((2 * k1p * DFT_S, rows * DFT_S), lambda b, j: (0, 0)), tspec, tspec],
        out_specs=[ospec, ospec],
        out_shape=[out, out],
        compiler_params=_cparams("parallel", "parallel"),
        name="hy_dft_first",
    )(x4, jnp.kron(f1, jnp.eye(DFT_S, dtype=F32)).astype(BF16), tab['tw_re'], tab['tw_im'])


DFT_TC = 1024


def _dft_mid_kernel(k1h, fused, *refs):
    if fused:
        are_ref, aim_ref, hre_ref, him_ref, wf_ref, wi_ref, twr_ref, twi_ref, ore_ref, oim_ref = refs
    else:
        are_ref, aim_ref, wf_ref, ore_ref, oim_ref = refs
    n2 = DFT_N2

    @pl.when(pl.program_id(0) < k1h)
    def _():
        a = jnp.concatenate([are_ref[0, 0], aim_ref[0, 0]], axis=0)
        dot = _dft_dot_bf16
        x = dot(wf_ref[...], a)
        xre, xim = x[:n2], x[n2:]
        if not fused:
            ore_ref[0, 0] = xre
            oim_ref[0, 0] = xim
        else:
            hre, him = hre_ref[0, 0], him_ref[0, 0]
            y = jnp.concatenate([xre * hre - xim * him, xre * him + xim * hre], axis=0)
            bb = dot(wi_ref[...], y)
            bre, bim = bb[:n2], bb[n2:]
            twr, twi = twr_ref[0], twi_ref[0]
            for c in range(bre.shape[1] // LANE):
                sl = slice(c * LANE, (c + 1) * LANE)
                ore_ref[0, 0, :, sl] = bre[:, sl] * twr + bim[:, sl] * twi
                oim_ref[0, 0, :, sl] = bim[:, sl] * twr - bre[:, sl] * twi

    @pl.when(pl.program_id(0) >= k1h)
    def _():
        ore_ref[...] = jnp.zeros_like(ore_ref)
        oim_ref[...] = jnp.zeros_like(oim_ref)


def _dft_mid(are, aim, tab, spec=None, order=0):
    bsz, k1p, _, cw = are.shape
    tc = min(DFT_TC, cw)
    blk = pl.BlockSpec((1, 1, DFT_N2, tc), lambda k, c, b: (b, k, 0, c))
    wspec = pl.BlockSpec((2 * DFT_N2, 2 * DFT_N2), lambda k, c, b: (0, 0))
    fused = spec is not None
    if fused:
        hspec = pl.BlockSpec((1, 1, DFT_N2, tc), lambda k, c, b: (order, k, 0, c))
        tspec = pl.BlockSpec((1, DFT_N2, LANE), lambda k, c, b: (k, 0, 0))
        in_specs = [blk, blk, hspec, hspec, wspec, wspec, tspec, tspec]
        args = [are, aim, spec[0], spec[1], tab['w_fwd'].astype(BF16), tab['w_inv'].astype(BF16),
                tab['tw_re'], tab['tw_im']]
    else:
        in_specs = [blk, blk, wspec]
        args = [are, aim, tab['w_fwd']]
    out = jax.ShapeDtypeStruct((bsz, k1p, DFT_N2, cw), F32)
    ore, oim = pl.pallas_call(
        functools.partial(_dft_mid_kernel, tab['k1h'], fused),
        grid=(k1p, cw // tc, bsz),
        in_specs=in_specs,
        out_specs=[blk, blk],
        out_shape=[out, out],
        compiler_params=_cparams("parallel", "parallel", "arbitrary"),
        name="hy_dft_mid_conv" if fused else "hy_dft_mid_filter",
    )(*args)
    return ore, oim


def _dft_last_kernel(bre_ref, bim_ref, g_ref, u_ref, x_ref, d_ref, o_ref):
    _, k1p, s, cw = bre_ref.shape
    rows = u_ref.shape[1]
    bb = jnp.concatenate([bre_ref[0].reshape(k1p * s, cw), bim_ref[0].reshape(k1p * s, cw)], axis=0).astype(BF16)
    y = jnp.dot(g_ref[...], bb, preferred_element_type=F32).reshape(rows, s, cw)
    o_ref[0, 0:rows] = x_ref[0] * (y + u_ref[0] * d_ref[...])
    if o_ref.shape[1] > rows:
        o_ref[0, rows:, :, :] = jnp.zeros((o_ref.shape[1] - rows,) + o_ref.shape[2:], F32)


def _dft_last(bre, bim, tab, u4, u_lead0, x4, x_lead0, dvec, out_rows):
    bsz, k1p, _, cw = bre.shape
    rows = tab['g'].shape[0]
    bspec = pl.BlockSpec((1, k1p, DFT_S, cw), lambda b, j: (b, 0, j, 0))
    return pl.pallas_call(
        _dft_last_kernel,
        grid=(bsz, DFT_N2 // DFT_S),
        in_specs=[bspec, bspec, pl.BlockSpec((rows * DFT_S, 2 * k1p * DFT_S), lambda b, j: (0, 0)),
                  pl.BlockSpec((1, rows, DFT_S, cw), lambda b, j: (u_lead0 + b, 0, j, 0)),
                  pl.BlockSpec((1, rows, DFT_S, cw), lambda b, j: (x_lead0 + b, 0, j, 0)),
                  pl.BlockSpec((1, 1, cw), lambda b, j: (0, 0, 0))],
        out_specs=pl.BlockSpec((1, out_rows, DFT_S, cw), lambda b, j: (b, 0, j, 0)),
        out_shape=jax.ShapeDtypeStruct((bsz, out_rows, DFT_N2, cw), F32),
        compiler_params=_cparams("parallel", "parallel"),
        name="hy_dft_last",
    )(bre, bim, jnp.kron(tab['g'], jnp.eye(DFT_S, dtype=F32)).astype(BF16), u4, x4,
      dvec.reshape(1, 1, cw).astype(F32))


def _ctx_dft_tables(n):
    size = 2 * n
    kp = -(-(n + 1) // 8) * 8
    k = np.arange(kp)[:, None].astype(np.float64)
    valid = (np.arange(kp) <= n)[:, None]
    th = 2.0 * np.pi * k * np.arange(size)[None, :] / size
    fw = np.concatenate([np.where(valid, np.cos(th), 0.0), np.where(valid, -np.sin(th), 0.0)], axis=0)
    wgt = np.where((np.arange(kp) == 0) | (np.arange(kp) == n), 1.0, 2.0) * (np.arange(kp) <= n)
    thi = 2.0 * np.pi * np.arange(n)[:, None] * np.arange(kp)[None, :] / size
    inv = np.concatenate([wgt * np.cos(thi), -wgt * np.sin(thi)], axis=1) / size
    return kp, jnp.asarray(fw.astype(np.float32)), jnp.asarray(inv.astype(np.float32))


def _hy_ctx_kernel(kp, p_ref, f_ref, d_ref, fw_ref, inv_ref, y_ref, o_ref):
    del y_ref
    n = p_ref.shape[2]
    fw = fw_ref[...]
    y = p_ref[0, 0]
    for o in range(HY_ORDER):
        h = _dft_dot(fw, f_ref[o])
        x = _dft_dot(fw[:, :n], y)
        hre, him, xre, xim = h[:kp], h[kp:], x[:kp], x[kp:]
        prod = jnp.concatenate([xre * hre - xim * him, xre * him + xim * hre], axis=0)
        conv = _dft_dot(inv_ref[...], prod)
        y = p_ref[o + 1, 0] * (conv + y * d_ref[o:o + 1, :])
    o_ref[0] = y


CTX_TC = 256


def _hy_ctx(parts, filt_c, long_bias, y_all, n_lat):
    nparts, bsz, t, cw = parts.shape
    n_ctx = t - n_lat
    kp, fw, inv = _ctx_dft_tables(n_ctx)
    return pl.pallas_call(
        functools.partial(_hy_ctx_kernel, kp),
        grid=(bsz, cw // CTX_TC),
        in_specs=[pl.BlockSpec((nparts, 1, n_ctx, CTX_TC), lambda b, c: (0, b, n_lat // n_ctx, c)),
                  pl.BlockSpec((HY_ORDER, 2 * n_ctx, CTX_TC), lambda b, c: (0, 0, c)),
                  pl.BlockSpec((HY_ORDER, CTX_TC), lambda b, c: (0, c)),
                  pl.BlockSpec(fw.shape, lambda b, c: (0, 0)),
                  pl.BlockSpec(inv.shape, lambda b, c: (0, 0)),
                  pl.BlockSpec(memory_space=pl.ANY)],
        out_specs=pl.BlockSpec((1, n_ctx, CTX_TC), lambda b, c: (b, n_lat // n_ctx, c)),
        out_shape=jax.ShapeDtypeStruct(y_all.shape, F32),
        input_output_aliases={5: 0},
        compiler_params=_cparams("parallel", "parallel"),
        name="hy_ctx_conv",
    )(parts, filt_c, long_bias.astype(F32), fw, inv, y_all)


def _hyena(u_all, hy_p, n_lat):
    conv_w, conv_b, f_w1, f_b1, f_fr1, f_w2, f_b2, f_fr2, f_w3, long_bias = hy_p
    bsz, t, _ = u_all.shape
    n_ctx = t - n_lat
    assert n_ctx == DFT_N2 and n_lat % (2 * DFT_N2) == 0
    n1 = 2 * n_lat // DFT_N2
    tab = _dft_tables(n1)
    cw = HY_WIDTH
    parts = _hy_conv(u_all, conv_w, conv_b, n_lat)
    filt_l = _hy_filter(n_lat, f_w1, f_b1, f_fr1, f_w2, f_b2, f_fr2, f_w3)
    filt_c = _hy_filter(n_ctx, f_w1, f_b1, f_fr1, f_w2, f_b2, f_fr2, f_w3)
    spec = _dft_mid(*_dft_first(filt_l.reshape(HY_ORDER, n1, DFT_N2, cw), 0, HY_ORDER, tab['f1'], tab), tab)
    f1_half = tab['f1'][:, :n1 // 2]
    parts4 = parts.reshape((HY_ORDER + 1) * bsz, t // DFT_N2, DFT_N2, cw)
    y4, y_lead0 = parts4, 0
    for o in range(HY_ORDER):
        are, aim = _dft_first(y4, y_lead0, bsz, f1_half, tab)
        bre, bim = _dft_mid(are, aim, tab, spec=spec, order=o)
        last = o + 1 == HY_ORDER
        y4 = _dft_last(bre, bim, tab, y4, y_lead0, parts4, (o + 1) * bsz, long_bias[o],
                       t // DFT_N2 if last else n1 // 2)
        y_lead0 = 0
    return _hy_ctx(parts, filt_c, long_bias, y4.reshape(bsz, t, cw), n_lat)


def _row_tile(m):
    for tm in (1280, 1024, 512, 256):
        if m % tm == 0:
            return tm
    raise ValueError(m)


def _col_tile(n):
    for k in range(n // LANE, 0, -1):
        if n % (k * LANE) == 0 and k * LANE <= 1280:
            return k * LANE
    raise ValueError(n)


def _in_proj(h, w):
    bsz, t, d = h.shape
    n = w.shape[1]
    return _matmul(h.reshape(bsz * t, d), w, _row_tile(bsz * t), _col_tile(n)).reshape(bsz, t, n)


def kernel(x, c, ctx, c_ctx, router_w, router_b, mod_w, mod_b, ln_g, ln_b, exp_w_gate, exp_w_up, exp_w_down, ab_w_in, ab_w_out, hy_conv_w, hy_conv_b, hy_f_w1, hy_f_b1, hy_f_fr1, hy_f_w2, hy_f_b2, hy_f_fr2, hy_f_w3, hy_long_bias, gla_gate_w2, gla_gate_b, gla_norm_g, ssd_w_in, ssd_conv_w, ssd_conv_b, ssd_dt_bias, ssd_a_log, ssd_d, ssd_norm_g, ssd_w_out):
    bsz, n_lat, d = x.shape
    n_ctx = ctx.shape[1]
    assert bsz < 8 and n_lat % ROW_TILE == 0 and n_ctx % ROW_TILE == 0 and d == D_MODEL
    c8 = jnp.zeros((8, d), F32).at[:bsz].set(c).at[bsz].set(c_ctx)
    router_w_pad = jnp.zeros((d, LANE), F32).at[:, :N_EXPERTS].set(router_w)
    mods = [_mod_vectors(c8, mod_w[i].astype(F32), mod_b[i].astype(F32)).reshape(8, 6, d) for i in range(DEPTH)]

    def rows(i, idx_l):
        lat = jnp.stack([mods[i][:bsz, k] for k in idx_l], axis=1)
        cx = jnp.broadcast_to(jnp.stack([mods[i][bsz, k] for k in idx_l], axis=0)[None], lat.shape)
        return jnp.concatenate([lat, cx], axis=1)

    def with_ln(p, g, b):
        extra = jnp.broadcast_to(jnp.stack([g, b], axis=0).astype(F32)[None], (bsz, 2, d))
        return jnp.concatenate([p, extra], axis=1)

    xa, h = _modulate(x, ctx, rows(0, (0, 1)))
    for i in range(DEPTH):
        j = i // 2
        if i % 2 == 0:
            w_pad = -(-AB_IN // LANE) * LANE
            w_in = jnp.zeros((d, w_pad), BF16).at[:, :AB_IN].set(ab_w_in[j].astype(BF16))
            u_all = _in_proj(h, w_in)
            hy_p = (hy_conv_w[j], hy_conv_b[j], hy_f_w1[j], hy_f_b1[j], hy_f_fr1[j], hy_f_w2[j], hy_f_b2[j],
                    hy_f_fr2[j], hy_f_w3[j], hy_long_bias[j])
            y_hy = _hyena(u_all, hy_p, n_lat)
            y_gla = _gla(u_all, gla_gate_w2[j], gla_gate_b[j], gla_norm_g[j], n_lat)
            w_out = ab_w_out[j].astype(BF16)
            a_list, w_list = [y_hy, y_gla], [w_out[:HY_WIDTH], w_out[HY_WIDTH:]]
        else:
            u2 = _in_proj(h, ssd_w_in[j].astype(BF16))
            cw9, cb1 = ssd_conv_w[j].astype(F32), ssd_conv_b[j].astype(F32)
            nx = SSD_D_INNER // LANE
            xs_t = _ssd_conv(u2, cw9, cb1, n_lat, 0, nx, F32)
            bc_t = _ssd_conv(u2, cw9, cb1, n_lat, nx, SSD_CONV_DIM // LANE - nx, BF16)
            y_ssd = _ssd(xs_t, bc_t, u2, ssd_dt_bias[j].astype(F32), ssd_a_log[j], ssd_d[j].astype(F32),
                         ssd_norm_g[j], n_lat)
            a_list, w_list = [y_ssd], [ssd_w_out[j].astype(BF16)]
        p1 = with_ln(rows(i, (2, 4, 3)), ln_g[i, 0], ln_b[i, 0])
        xa, tok, logits = _proj_update(a_list, w_list, xa, p1, router_w_pad, n_lat)
        y0, y1, gates = _moe(tok, logits, router_b, exp_w_gate.astype(F32), exp_w_up.astype(F32),
                             exp_w_down.astype(F32), i)
        gate2 = rows(i, (5,))
        if i + 1 < DEPTH:
            nxt = rows(i + 1, (1, 0))
        else:
            nxt = jnp.zeros((bsz, 4, d), F32)
        p2 = jnp.concatenate([gate2[:, 0:1], nxt[:, 0:2], gate2[:, 1:2], nxt[:, 2:4]], axis=1)
        xa, h = _combine_update(y0, y1, gates, xa, with_ln(p2, ln_g[i, 1], ln_b[i, 1]), n_lat, i + 1 == DEPTH)
    return xa.astype(x.dtype)
```
